```python
import math
import jax, jax.numpy as jnp
from jax import lax
import numpy as np

D_MODEL = 1024
BATCH = 1
SEQ = 16384
DEPTH = 4

SSD_HEADS = 16
SSD_HEAD_DIM = 64
SSD_INNER = SSD_HEADS * SSD_HEAD_DIM
SSD_GROUPS = 2
SSD_STATE = 128
SSD_CONV = 4
SSD_CHUNK = 256
SSD_CONV_DIM = SSD_INNER + 2 * SSD_GROUPS * SSD_STATE
ATT_HEADS = 8
ATT_HEAD_DIM = 128
ATT_INNER = ATT_HEADS * ATT_HEAD_DIM
IDX_HEADS = 8
IDX_HEAD_DIM = 64
TOPK_MAX = 256
Q_BLOCK = 128
EVEN_SPLITS = (SSD_INNER, SSD_CONV_DIM, SSD_HEADS, ATT_INNER, ATT_INNER, ATT_INNER,
               IDX_HEADS * IDX_HEAD_DIM, IDX_HEAD_DIM, IDX_HEADS)
EVEN_IN = sum(EVEN_SPLITS)
EVEN_MIX = SSD_INNER + ATT_INNER
SG_CHUNK = 128
SG_GROUPS = 8
SG_INNER = 2 * D_MODEL
FFN_HIDDEN = 2816
ROPE_THETA = 500000.0
ROPE_FRACTION = 4
EPS = 1e-6

kernel_name = 'hybrid_ssd_dsa_gmlp_macaron_trunk'


def rms_norm(x, g):
    xf = x.astype(jnp.float32)
    y = xf * lax.rsqrt(jnp.mean(xf * xf, -1, keepdims=True) + EPS)
    return (y * g.astype(jnp.float32)).astype(x.dtype)


def group_rms_norm(y, g, groups):
    shp = y.shape
    yf = y.astype(jnp.float32).reshape(shp[:-1] + (groups, shp[-1] // groups))
    yf = yf * lax.rsqrt(jnp.mean(yf * yf, -1, keepdims=True) + EPS)
    return (yf.reshape(shp) * g.astype(jnp.float32)).astype(y.dtype)


def layer_norm(x, g, b):
    xf = x.astype(jnp.float32)
    mu = jnp.mean(xf, -1, keepdims=True)
    var = jnp.mean(jnp.square(xf - mu), -1, keepdims=True)
    y = (xf - mu) * lax.rsqrt(var + EPS)
    return (y * g.astype(jnp.float32) + b.astype(jnp.float32)).astype(x.dtype)


def swiglu(x, w_gate_up, w_down):
    gate, up = jnp.split(x @ w_gate_up, 2, axis=-1)
    return (jax.nn.silu(gate) * up) @ w_down


def rope_tables(seq, rot_dim):
    inv = 1.0 / (ROPE_THETA ** (jnp.arange(0, rot_dim, 2, dtype=jnp.float32) / rot_dim))
    ang = jnp.arange(seq, dtype=jnp.float32)[:, None] * inv[None, :]
    return jnp.cos(ang), jnp.sin(ang)


def partial_rope(x, cos, sin):
    half = cos.shape[-1]
    x1, x2, xp = x[..., :half], x[..., half:2 * half], x[..., 2 * half:]
    c = cos[None, :, None, :].astype(x.dtype)
    s = sin[None, :, None, :].astype(x.dtype)
    return jnp.concatenate([x1 * c - x2 * s, x1 * s + x2 * c, xp], axis=-1)


def segsum_exp(a):
    cs = jnp.cumsum(a, axis=-1)
    diff = cs[..., :, None] - cs[..., None, :]
    q = a.shape[-1]
    mask = jnp.tril(jnp.ones((q, q), dtype=bool))
    return jnp.exp(jnp.where(mask, diff, -jnp.inf))


def ssd_scan(xh, dt, a, bm, cm):
    bsz, seq, heads, hd = xh.shape
    grp, nst = bm.shape[2], bm.shape[3]
    kpg = heads // grp
    pad = (-seq) % SSD_CHUNK
    if pad:
        xh = jnp.pad(xh, ((0, 0), (0, pad), (0, 0), (0, 0)))
        dt = jnp.pad(dt, ((0, 0), (0, pad), (0, 0)))
        bm = jnp.pad(bm, ((0, 0), (0, pad), (0, 0), (0, 0)))
        cm = jnp.pad(cm, ((0, 0), (0, pad), (0, 0), (0, 0)))
    lp = seq + pad
    nc, q = lp // SSD_CHUNK, SSD_CHUNK
    x = (xh * dt[..., None]).reshape(bsz, nc, q, grp, kpg, hd)
    adt = jnp.moveaxis((dt * a).reshape(bsz, nc, q, grp, kpg), 2, -1)
    bc = bm.reshape(bsz, nc, q, grp, nst)
    cc = cm.reshape(bsz, nc, q, grp, nst)
    a_cs = jnp.cumsum(adt, axis=-1)
    cb = jnp.einsum('bclgn,bcsgn->bcgls', cc, bc)
    scores = cb[:, :, :, None] * segsum_exp(adt)
    y_diag = jnp.einsum('bcgkls,bcsgkp->bclgkp', scores, x)
    decay_states = jnp.moveaxis(jnp.exp(a_cs[..., -1:] - a_cs), -1, 2)
    states = jnp.einsum('bcsgn,bcsgkp->bcgkpn', bc, x * decay_states[..., None])
    chunk_decay = jnp.exp(a_cs[..., -1])

    def step(h, inp):
        s_c, d_c = inp
        return h * d_c[..., None, None] + s_c, h

    h0 = jnp.zeros((bsz, grp, kpg, hd, nst), x.dtype)
    _, prev = lax.scan(step, h0, (jnp.moveaxis(states, 1, 0), jnp.moveaxis(chunk_decay, 1, 0)))
    prev = jnp.moveaxis(prev, 0, 1)
    in_decay = jnp.moveaxis(jnp.exp(a_cs), -1, 2)
    y_off = jnp.einsum('bclgn,bcgkpn->bclgkp', cc, prev) * in_decay[..., None]
    y = (y_diag + y_off).reshape(bsz, lp, heads, hd)
    return y[:, :seq]


def dsa_attention(q, k, v, qi, ki, wi):
    bsz, seq, heads, hd = q.shape
    top_k = min(TOPK_MAX, seq // 4)
    nb = seq // Q_BLOCK
    key_pos = jnp.arange(seq)
    q_pos = key_pos.reshape(nb, Q_BLOCK)
    idx_scale = IDX_HEAD_DIM ** -0.5
    att_scale = hd ** -0.5
    gather = jax.vmap(lambda t, i: t[i])

    def blocks(t):
        return jnp.moveaxis(t.reshape((bsz, nb, Q_BLOCK) + t.shape[2:]), 1, 0)

    def one_block(args):
        qb, qib, wb, pos = args
        rel = jax.nn.relu(jnp.einsum('bqhd,bsd->bqhs', qib, ki) * idx_scale)
        score = jnp.einsum('bqh,bqhs->bqs', wb, rel).astype(jnp.float32)
        causal = key_pos[None, :] <= pos[:, None]
        score = jnp.where(causal[None], score, -jnp.inf)
        _, sel = lax.top_k(score, top_k)
        ks = gather(k, sel)
        vs = gather(v, sel)
        logits = jnp.einsum('bqhd,bqkhd->bhqk', qb, ks).astype(jnp.float32) * att_scale
        valid = sel <= pos[None, :, None]
        logits = jnp.where(valid[:, None], logits, -jnp.inf)
        p = jax.nn.softmax(logits, axis=-1).astype(vs.dtype)
        return jnp.einsum('bhqk,bqkhd->bqhd', p, vs)

    out = lax.map(one_block, (blocks(q), blocks(qi), blocks(wi), q_pos))
    return jnp.moveaxis(out, 0, 1).reshape(bsz, seq, heads, hd)


def even_mixer(h, w_in, conv_w, conv_b, dt_bias, a_log, d_skip, gate_norm, w_out, rope_a, rope_i):
    bsz, seq, _ = h.shape
    offs = np.cumsum(EVEN_SPLITS)[:-1].tolist()
    z, xbc, dt_raw, q, k, v, qi, ki, wi = jnp.split(h @ w_in, offs, axis=-1)
    xbc = lax.conv_general_dilated(xbc, conv_w[:, None, :], window_strides=(1,),
                                   padding=[(SSD_CONV - 1, 0)],
                                   dimension_numbers=('NWC', 'WIO', 'NWC'),
                                   feature_group_count=SSD_CONV_DIM) + conv_b
    xbc = jax.nn.silu(xbc)
    xs, bm, cm = jnp.split(xbc, [SSD_INNER, SSD_INNER + SSD_GROUPS * SSD_STATE], axis=-1)
    xs = xs.reshape(bsz, seq, SSD_HEADS, SSD_HEAD_DIM)
    bm = bm.reshape(bsz, seq, SSD_GROUPS, SSD_STATE)
    cm = cm.reshape(bsz, seq, SSD_GROUPS, SSD_STATE)
    dt = jax.nn.softplus(dt_raw + dt_bias)
    a = -jnp.exp(a_log)
    y = ssd_scan(xs, dt, a, bm, cm) + d_skip[:, None] * xs
    y = group_rms_norm(y.reshape(bsz, seq, SSD_INNER) * jax.nn.silu(z), gate_norm, SSD_GROUPS)
    cos_a, sin_a = rope_a
    cos_i, sin_i = rope_i
    q = partial_rope(q.reshape(bsz, seq, ATT_HEADS, ATT_HEAD_DIM), cos_a, sin_a)
    k = partial_rope(k.reshape(bsz, seq, ATT_HEADS, ATT_HEAD_DIM), cos_a, sin_a)
    v = v.reshape(bsz, seq, ATT_HEADS, ATT_HEAD_DIM)
    qi = partial_rope(qi.reshape(bsz, seq, IDX_HEADS, IDX_HEAD_DIM), cos_i, sin_i)
    ki = partial_rope(ki.reshape(bsz, seq, 1, IDX_HEAD_DIM), cos_i, sin_i)[:, :, 0]
    wi = wi * IDX_HEADS ** -0.5
    o = dsa_attention(q, k, v, qi, ki, wi).reshape(bsz, seq, ATT_INNER)
    return jnp.concatenate([y, o], axis=-1) @ w_out


def odd_mixer(h, w_in, ln_g, ln_b, w_s, b_s, w_out):
    bsz, seq, _ = h.shape
    u, v = jnp.split(jax.nn.gelu(h @ w_in), 2, axis=-1)
    v = layer_norm(v, ln_g, ln_b)
    nc = seq // SG_CHUNK
    v = v.reshape(bsz, nc, SG_CHUNK, SG_GROUPS, SG_INNER // SG_GROUPS)
    causal = jnp.tril(jnp.ones((SG_CHUNK, SG_CHUNK), dtype=w_s.dtype))
    mixed = jnp.einsum('gts,bcsgd->bctgd', w_s * causal, v) + b_s.T[:, :, None]
    return (u * mixed.reshape(bsz, seq, SG_INNER)) @ w_out


def setup_inputs(seed: int = 0) -> dict:
    key = jax.random.key(seed)
    ks = jax.random.split(key, 20)
    f32 = jnp.float32
    n_even = (DEPTH + 1) // 2
    n_odd = DEPTH // 2

    def nrm(k, shape, fan_in):
        return jax.random.normal(k, shape, f32) * fan_in ** -0.5

    def gain(k, shape):
        return 1.0 + 0.02 * jax.random.normal(k, shape, f32)

    def small(k, shape):
        return 0.02 * jax.random.normal(k, shape, f32)

    dt0 = jnp.exp(jax.random.uniform(ks[8], (n_even, SSD_HEADS), f32,
                                     minval=math.log(1e-3), maxval=math.log(1e-1)))
    return {
        'x': jax.random.normal(ks[0], (BATCH, SEQ, D_MODEL), f32),
        'norm_g': gain(ks[1], (DEPTH, 3, D_MODEL)),
        'final_g': gain(ks[2], (D_MODEL,)),
        'ffn_w_gu': nrm(ks[3], (DEPTH, 2, D_MODEL, 2 * FFN_HIDDEN), D_MODEL),
        'ffn_w_down': nrm(ks[4], (DEPTH, 2, FFN_HIDDEN, D_MODEL), FFN_HIDDEN),
        'ev_w_in': nrm(ks[5], (n_even, D_MODEL, EVEN_IN), D_MODEL),
        'ev_conv_w': nrm(ks[6], (n_even, SSD_CONV, SSD_CONV_DIM), SSD_CONV),
        'ev_conv_b': small(ks[7], (n_even, SSD_CONV_DIM)),
        'ev_dt_bias': dt0 + jnp.log(-jnp.expm1(-dt0)),
        'ev_a_log': jnp.log(jax.random.uniform(ks[9], (n_even, SSD_HEADS), f32, minval=1.0, maxval=16.0)),
        'ev_d': gain(ks[10], (n_even, SSD_HEADS)),
        'ev_gate_norm': gain(ks[11], (n_even, SSD_INNER)),
        'ev_w_out': nrm(ks[12], (n_even, EVEN_MIX, D_MODEL), EVEN_MIX),
        'od_w_in': nrm(ks[13], (n_odd, D_MODEL, 2 * SG_INNER), D_MODEL),
        'od_ln_g': gain(ks[14], (n_odd, SG_INNER)),
        'od_ln_b': small(ks[15], (n_odd, SG_INNER)),
        'od_w_s': nrm(ks[16], (n_odd, SG_GROUPS, SG_CHUNK, SG_CHUNK), SG_CHUNK),
        'od_b_s': gain(ks[17], (n_odd, SG_GROUPS, SG_CHUNK)),
        'od_w_out': nrm(ks[18], (n_odd, SG_INNER, D_MODEL), SG_INNER),
    }


def reference(x, norm_g, final_g, ffn_w_gu, ffn_w_down, ev_w_in, ev_conv_w, ev_conv_b,
              ev_dt_bias, ev_a_log, ev_d, ev_gate_norm, ev_w_out, od_w_in, od_ln_g, od_ln_b,
              od_w_s, od_b_s, od_w_out):
    seq = x.shape[1]
    rope_a = rope_tables(seq, ATT_HEAD_DIM // ROPE_FRACTION)
    rope_i = rope_tables(seq, IDX_HEAD_DIM // ROPE_FRACTION)
    h = x
    for layer in range(DEPTH):
        j = layer // 2
        h = h + 0.5 * swiglu(rms_norm(h, norm_g[layer, 0]), ffn_w_gu[layer, 0], ffn_w_down[layer, 0])
        hn = rms_norm(h, norm_g[layer, 1])
        if layer % 2 == 0:
            mix = even_mixer(hn, ev_w_in[j], ev_conv_w[j], ev_conv_b[j], ev_dt_bias[j], ev_a_log[j],
                             ev_d[j], ev_gate_norm[j], ev_w_out[j], rope_a, rope_i)
        else:
            mix = odd_mixer(hn, od_w_in[j], od_ln_g[j], od_ln_b[j], od_w_s[j], od_b_s[j], od_w_out[j])
        h = h + mix
        h = h + 0.5 * swiglu(rms_norm(h, norm_g[layer, 2]), ffn_w_gu[layer, 1], ffn_w_down[layer, 1])
    return rms_norm(h, final_g)
```

```python
import functools
import math

import numpy as np
import jax
import jax.numpy as jnp
from jax import lax
from jax.experimental import pallas as pl
from jax.experimental.pallas import tpu as pltpu

D_MODEL = 1024
SSD_HEADS = 16
SSD_HEAD_DIM = 64
SSD_INNER = SSD_HEADS * SSD_HEAD_DIM
SSD_GROUPS = 2
SSD_STATE = 128
SSD_CONV = 4
SSD_CHUNK = 256
SSD_CONV_DIM = SSD_INNER + 2 * SSD_GROUPS * SSD_STATE
ATT_HEADS = 8
ATT_HEAD_DIM = 128
ATT_INNER = ATT_HEADS * ATT_HEAD_DIM
IDX_HEADS = 8
IDX_HEAD_DIM = 64
IDX_INNER = IDX_HEADS * IDX_HEAD_DIM
TOPK_MAX = 256
SG_CHUNK = 128
SG_GROUPS = 8
SG_INNER = 2 * D_MODEL
FFN_HIDDEN = 2816
ROPE_THETA = 500000.0
ROPE_FRACTION = 4
EPS = 1e-6

ATT_ROT_HALF = ATT_HEAD_DIM // ROPE_FRACTION // 2
IDX_ROT_HALF = IDX_HEAD_DIM // ROPE_FRACTION // 2

MXU_DTYPE = jnp.bfloat16
LANES = 128
VMEM_LIMIT_BYTES = 56 * 1024 * 1024

ROW_TILE = 512
FFN_HIDDEN_CHUNK = FFN_HIDDEN // 2
DSA_Q_TILE = 256
DSA_K_TILE = 256
INT_MIN = -(2 ** 31)
NEG_BIG = -0.7 * float(np.finfo(np.float32).max)

_N_Z = 0
_N_XBC = _N_Z + SSD_INNER
_N_K = _N_XBC + SSD_CONV_DIM
_N_SMALL = _N_K + ATT_INNER
_N_END = _N_SMALL + LANES
_SM_KI = 0
_SM_DT = IDX_HEAD_DIM
_T_Q = 0
_T_V = _T_Q + ATT_INNER
_T_QI = _T_V + ATT_INNER
_T_WI = _T_QI + IDX_INNER
_T_DT = _T_WI + 16
_T_END = _T_DT + SSD_HEADS


def _dot(a, b):
    return jnp.dot(a, b, preferred_element_type=jnp.float32)


def _rms(x, g):
    return x * lax.rsqrt(jnp.mean(x * x, axis=-1, keepdims=True) + EPS) * g


def _softplus(x):
    return jnp.maximum(x, 0.0) + jnp.log1p(jnp.exp(-jnp.abs(x)))


def _split3(x):
    hi = x.astype(MXU_DTYPE)
    r = x - hi.astype(jnp.float32)
    mid = r.astype(MXU_DTYPE)
    lo = (r - mid.astype(jnp.float32)).astype(MXU_DTYPE)
    return hi, mid, lo


def _const_spec(shape):
    zeros = (0,) * len(shape)
    return pl.BlockSpec(shape, lambda *_: zeros, pipeline_mode=pl.Buffered(1))


def _params(sem):
    return pltpu.CompilerParams(dimension_semantics=sem, vmem_limit_bytes=VMEM_LIMIT_BYTES)


def _ffn_body(h_ref, g_ref, wg_ref, wu_ref, wd_ref, fg_ref, o_ref, *, final):
    h = h_ref[...]
    xn = _rms(h, g_ref[...]).astype(MXU_DTYPE)
    acc = None
    for c in range(FFN_HIDDEN // FFN_HIDDEN_CHUNK):
        sl = slice(c * FFN_HIDDEN_CHUNK, (c + 1) * FFN_HIDDEN_CHUNK)
        gate = _dot(xn, wg_ref[:, sl])
        up = _dot(xn, wu_ref[:, sl])
        act = (gate * jax.nn.sigmoid(gate) * up).astype(MXU_DTYPE)
        part = _dot(act, wd_ref[sl, :])
        acc = part if acc is None else acc + part
    out = h + 0.5 * acc
    if final:
        out = _rms(out, fg_ref[...])
    o_ref[...] = out


def _ffn(h, g, wg, wu, wd, final_g, final):
    seq = h.shape[0]
    tm = min(ROW_TILE, seq)
    row = pl.BlockSpec((tm, D_MODEL), lambda i: (i, 0))
    return pl.pallas_call(
        functools.partial(_ffn_body, final=final),
        out_shape=jax.ShapeDtypeStruct((seq, D_MODEL), jnp.float32),
        grid=(seq // tm,),
        in_specs=[row, _const_spec((1, D_MODEL)), _const_spec((D_MODEL, FFN_HIDDEN)),
                  _const_spec((D_MODEL, FFN_HIDDEN)), _const_spec((FFN_HIDDEN, D_MODEL)),
                  _const_spec((1, D_MODEL))],
        out_specs=row,
        compiler_params=_params(("parallel",)),
        name="ffn",
    )(h, g, wg, wu, wd, final_g)


def _rope_lanes(x, c, s_lo, s_hi, half):
    return x * c + pltpu.roll(x, LANES - half, 1) * s_lo + pltpu.roll(x, half, 1) * s_hi


def _rope_rows(x, c, s, half):
    x1, x2 = x[:half], x[half:2 * half]
    return jnp.concatenate([x1 * c - x2 * s, x1 * s + x2 * c, x[2 * half:]], axis=0)


def _even_in_body(h_ref, g_ref, wn_ref, wt_ref, ca_ref, sal_ref, sah_ref, ci_ref, sil_ref, sih_ref,
                  cat_ref, sat_ref, cit_ref, sit_ref,
                  z_ref, xbc_ref, small_ref, k_ref, ki_ref, qt_ref, vt_ref, qit_ref, lohi_ref, dtt_ref):
    xn = _rms(h_ref[...], g_ref[...])
    xnb = xn.astype(MXU_DTYPE)
    xnt = xn.T.astype(MXU_DTYPE)

    z_ref[...] = _dot(xnb, wn_ref[:, _N_Z:_N_XBC])
    xbc_ref[...] = _dot(xnb, wn_ref[:, _N_XBC:_N_K])
    kk = _dot(xnb, wn_ref[:, _N_K:_N_SMALL])
    ca, sal, sah = ca_ref[...], sal_ref[...], sah_ref[...]
    for hd in range(ATT_HEADS):
        sl = slice(hd * ATT_HEAD_DIM, (hd + 1) * ATT_HEAD_DIM)
        k_ref[:, sl] = _rope_lanes(kk[:, sl], ca, sal, sah, ATT_ROT_HALF).astype(k_ref.dtype)
    sm = _dot(xnb, wn_ref[:, _N_SMALL:_N_END])
    small_ref[...] = sm
    smr = _rope_lanes(sm, ci_ref[...], sil_ref[...], sih_ref[...], IDX_ROT_HALF)
    ki_ref[...] = smr[:, _SM_KI:_SM_KI + IDX_HEAD_DIM].astype(ki_ref.dtype)

    att_scale = ATT_HEAD_DIM ** -0.5
    qt = _dot(wt_ref[_T_Q:_T_V, :], xnt)
    cat, sat = cat_ref[...], sat_ref[...]
    for hd in range(ATT_HEADS):
        sl = slice(hd * ATT_HEAD_DIM, (hd + 1) * ATT_HEAD_DIM)
        qt_ref[sl, :] = (_rope_rows(qt[sl], cat, sat, ATT_ROT_HALF) * att_scale).astype(qt_ref.dtype)
    vt_ref[...] = _dot(wt_ref[_T_V:_T_QI, :], xnt).astype(vt_ref.dtype)
    tail = _dot(wt_ref[_T_WI:_T_END, :], xnt)
    wit = tail[0:IDX_HEADS] * (IDX_HEADS ** -0.5 * IDX_HEAD_DIM ** -0.5)
    dtt_ref[...] = tail[_T_DT - _T_WI:_T_END - _T_WI]
    pos = wit > 0.0
    lohi_ref[0:8, :] = jnp.where(pos, 0.0, -jnp.inf)
    lohi_ref[8:16, :] = jnp.where(pos, jnp.inf, 0.0)
    qit = _dot(wt_ref[_T_QI:_T_WI, :], xnt)
    cit, sit = cit_ref[...], sit_ref[...]
    for hd in range(IDX_HEADS):
        sl = slice(hd * IDX_HEAD_DIM, (hd + 1) * IDX_HEAD_DIM)
        qit_ref[sl, :] = (_rope_rows(qit[sl], cit, sit, IDX_ROT_HALF) * wit[hd:hd + 1]).astype(qit_ref.dtype)


def _even_in(h, g, wn, wt, rope):
    seq = h.shape[0]
    tm = min(ROW_TILE, seq)
    f32 = jnp.float32

    def rows(width):
        return pl.BlockSpec((tm, width), lambda i: (i, 0))

    def cols(height):
        return pl.BlockSpec((height, tm), lambda i: (0, i))

    out_shape = (
        jax.ShapeDtypeStruct((seq, SSD_INNER), f32),
        jax.ShapeDtypeStruct((seq, SSD_CONV_DIM), f32),
        jax.ShapeDtypeStruct((seq, LANES), f32),
        jax.ShapeDtypeStruct((seq, ATT_INNER), MXU_DTYPE),
        jax.ShapeDtypeStruct((seq, IDX_HEAD_DIM), MXU_DTYPE),
        jax.ShapeDtypeStruct((ATT_INNER, seq), MXU_DTYPE),
        jax.ShapeDtypeStruct((ATT_INNER, seq), MXU_DTYPE),
        jax.ShapeDtypeStruct((IDX_INNER, seq), MXU_DTYPE),
        jax.ShapeDtypeStruct((16, seq), f32),
        jax.ShapeDtypeStruct((SSD_HEADS, seq), f32),
    )
    out_specs = (rows(SSD_INNER), rows(SSD_CONV_DIM), rows(LANES), rows(ATT_INNER), rows(IDX_HEAD_DIM),
                 cols(ATT_INNER), cols(ATT_INNER), cols(IDX_INNER), cols(16), cols(SSD_HEADS))
    in_specs = [rows(D_MODEL), _const_spec((1, D_MODEL)), _const_spec(wn.shape), _const_spec(wt.shape)]
    in_specs += [rows(LANES)] * 6
    in_specs += [cols(ATT_ROT_HALF)] * 2 + [cols(IDX_ROT_HALF)] * 2
    return pl.pallas_call(
        _even_in_body,
        out_shape=out_shape,
        grid=(seq // tm,),
        in_specs=in_specs,
        out_specs=out_specs,
        compiler_params=_params(("parallel",)),
        name="even_in",
    )(h, g, wn, wt, *rope)


def _ssd_body(xbc_ref, small_ref, dtt_ref, z_ref, cw_ref, cb_ref, dtb_ref, dtbt_ref, a_ref, at_ref,
              drep_ref, gn_ref, o_ref, tail_scr, st_scr, y_scr, xdec_scr, decay_scr):
    q = SSD_CHUNK
    f32 = jnp.float32

    @pl.when(pl.program_id(0) == 0)
    def _():
        tail_scr[...] = jnp.zeros_like(tail_scr)
        st_scr[...] = jnp.zeros_like(st_scr)

    x = xbc_ref[...]
    tail = tail_scr[...]
    row8 = lax.broadcasted_iota(jnp.int32, (8, SSD_CONV_DIM), 0)
    conv = x * cw_ref[SSD_CONV - 1:SSD_CONV, :] + cb_ref[...]
    for shift in range(1, SSD_CONV):
        rolled = pltpu.roll(x, shift, 0)
        head = jnp.where(row8 < shift, pltpu.roll(tail, shift, 0), rolled[0:8])
        shifted = jnp.concatenate([head, rolled[8:]], axis=0)
        conv = conv + shifted * cw_ref[SSD_CONV - 1 - shift:SSD_CONV - shift, :]
    tail_scr[...] = x[q - 8:q]
    xbc = conv * jax.nn.sigmoid(conv)
    xs = xbc[:, :SSD_INNER]
    bm = xbc[:, SSD_INNER:SSD_INNER + SSD_GROUPS * SSD_STATE]
    cm = xbc[:, SSD_INNER + SSD_GROUPS * SSD_STATE:]

    dt_col = _softplus(small_ref[:, _SM_DT:_SM_DT + SSD_HEADS] + dtb_ref[...])
    dt_row = _softplus(dtt_ref[...] + dtbt_ref[...])
    adt_col = dt_col * (-jnp.exp(a_ref[...]))
    adt_row = dt_row * (-jnp.exp(at_ref[...]))
    ri = lax.broadcasted_iota(jnp.int32, (q, q), 0)
    ci = lax.broadcasted_iota(jnp.int32, (q, q), 1)
    causal = ri >= ci
    tril = jnp.where(causal, 1.0, 0.0).astype(MXU_DTYPE)
    triu = jnp.where(ri <= ci, 1.0, 0.0).astype(MXU_DTYPE)
    cs_col = sum(_dot(tril, p) for p in _split3(adt_col))
    cs_row = sum(_dot(p, triu) for p in _split3(adt_row))

    cmb = cm.astype(MXU_DTYPE)
    bmb = bm.astype(MXU_DTYPE)
    for g in range(SSD_GROUPS):
        gs = slice(g * SSD_STATE, (g + 1) * SSD_STATE)
        cb = lax.dot_general(cmb[:, gs], bmb[:, gs], (((1,), (1,)), ((), ())),
                             preferred_element_type=f32)
        kpg = SSD_HEADS // SSD_GROUPS
        for hd in range(g * kpg, (g + 1) * kpg):
            hs = slice(hd * SSD_HEAD_DIM, (hd + 1) * SSD_HEAD_DIM)
            col = cs_col[:, hd:hd + 1]
            row = cs_row[hd:hd + 1, :]
            last = cs_row[hd:hd + 1, q - 1:q]
            decay = jnp.exp(jnp.where(causal, col - row, -jnp.inf))
            xdt = xs[:, hs] * dt_col[:, hd:hd + 1]
            y = _dot((cb * decay).astype(MXU_DTYPE), xdt.astype(MXU_DTYPE))
            y_off = _dot(cmb[:, gs], st_scr[:, hs].astype(MXU_DTYPE))
            y_scr[:, hs] = y + y_off * jnp.exp(col)
            xdec_scr[:, hs] = (xdt * jnp.exp(last - col)).astype(xdec_scr.dtype)
            decay_scr[:, hs] = jnp.broadcast_to(jnp.exp(last), (1, SSD_HEAD_DIM))
        ws = slice(g * kpg * SSD_HEAD_DIM, (g + 1) * kpg * SSD_HEAD_DIM)
        bmt = bm[:, gs].T.astype(MXU_DTYPE)
        st_scr[:, ws] = st_scr[:, ws] * decay_scr[:, ws] + _dot(bmt, xdec_scr[:, ws])

    z = z_ref[...]
    y = (y_scr[...] + drep_ref[...] * xs) * (z * jax.nn.sigmoid(z))
    gw = SSD_INNER // SSD_GROUPS
    for g in range(SSD_GROUPS):
        sl = slice(g * gw, (g + 1) * gw)
        seg = y[:, sl]
        seg = seg * lax.rsqrt(jnp.mean(seg * seg, axis=-1, keepdims=True) + EPS)
        o_ref[:, sl] = (seg * gn_ref[:, sl]).astype(o_ref.dtype)


def _ssd(xbc, small, dtt, z, conv_w, conv_b, dt_bias, a_log, d_skip, gate_norm):
    seq = xbc.shape[0]
    q = SSD_CHUNK
    f32 = jnp.float32

    def rows(width):
        return pl.BlockSpec((q, width), lambda i: (i, 0))

    drep = jnp.repeat(d_skip, SSD_HEAD_DIM)[None, :]
    return pl.pallas_call(
        _ssd_body,
        out_shape=jax.ShapeDtypeStruct((seq, SSD_INNER), MXU_DTYPE),
        grid=(seq // q,),
        in_specs=[rows(SSD_CONV_DIM), rows(LANES), pl.BlockSpec((SSD_HEADS, q), lambda i: (0, i)),
                  rows(SSD_INNER), _const_spec((SSD_CONV, SSD_CONV_DIM)), _const_spec((1, SSD_CONV_DIM)),
                  _const_spec((1, SSD_HEADS)), _const_spec((SSD_HEADS, 1)),
                  _const_spec((1, SSD_HEADS)), _const_spec((SSD_HEADS, 1)),
                  _const_spec((1, SSD_INNER)), _const_spec((1, SSD_INNER))],
        out_specs=rows(SSD_INNER),
        scratch_shapes=[pltpu.VMEM((8, SSD_CONV_DIM), f32),
                        pltpu.VMEM((SSD_STATE, SSD_INNER), f32),
                        pltpu.VMEM((q, SSD_INNER), f32),
                        pltpu.VMEM((q, SSD_INNER), MXU_DTYPE),
                        pltpu.VMEM((1, SSD_INNER), f32)],
        compiler_params=_params(("arbitrary",)),
        name="ssd",
    )(xbc, small, dtt, z, conv_w, conv_b[None, :], dt_bias[None, :], dt_bias[:, None],
      a_log[None, :], a_log[:, None], drep, gate_norm[None, :])


def _dsa_body(qi_ref, kj_ref, qit_ref, lohi_ref, ki_ref, qt_ref, k_ref, vt_ref, o_ref,
              keys_scr, thr_scr, m_scr, l_scr, acc_scr, *, top_k, idx_bits):
    tq, tk = DSA_Q_TILE, DSA_K_TILE
    i32 = jnp.int32
    p = pl.program_id(0)
    qi = qi_ref[p]
    kj = kj_ref[p]

    @pl.when(kj == 0)
    def _select():
        n_chunks = (qi + 1) * (tq // tk)
        lo = lohi_ref[0:8, :]
        hi = lohi_ref[8:16, :]
        t_pos = qi * tq + lax.broadcasted_iota(i32, (tk, tq), 1)
        s_off = lax.broadcasted_iota(i32, (tk, tq), 0)

        def score_chunk(c, carry):
            r0 = pl.multiple_of(c * tk, tk)
            kic = ki_ref[pl.ds(r0, tk), :]
            sc = jnp.zeros((tk, tq), jnp.float32)
            for hd in range(IDX_HEADS):
                y = _dot(kic, qit_ref[hd * IDX_HEAD_DIM:(hd + 1) * IDX_HEAD_DIM, :])
                sc = sc + jnp.minimum(jnp.maximum(y, lo[hd:hd + 1]), hi[hd:hd + 1])
            bits = pltpu.bitcast(sc, i32)
            key = bits ^ ((bits >> 31) & 0x7FFFFFFF)
            key = jnp.where(sc == 0.0, 0, key)
            key = jnp.where(r0 + s_off <= t_pos, key, INT_MIN)
            keys_scr[pl.ds(r0, tk), :] = key
            return carry

        lax.fori_loop(0, n_chunks, score_chunk, 0)

        def count(pred):
            def body(c, acc):
                r0 = pl.multiple_of(c * tk, tk)
                hit = jnp.where(pred(keys_scr[pl.ds(r0, tk), :], r0), 1.0, 0.0)
                return acc + hit.reshape(tk // 8, 8, tq).sum(axis=0)
            acc = lax.fori_loop(0, n_chunks, body, jnp.zeros((8, tq), jnp.float32))
            return acc.sum(axis=0, keepdims=True)

        def bit_step(b, thr):
            cand = thr + lax.shift_left(i32(1), i32(31) - b)
            cnt = count(lambda kk, r0: kk >= cand)
            return jnp.where(cnt >= top_k, cand, thr)

        thr = lax.fori_loop(0, 32, bit_step, jnp.full((1, tq), INT_MIN, i32))
        thr = jnp.maximum(thr, INT_MIN + 1)
        thr_scr[0:1, :] = thr
        n_ge = count(lambda kk, r0: kk >= thr)

        @pl.when(jnp.max(n_ge) > top_k)
        def _ties():
            need = top_k - count(lambda kk, r0: kk > thr)

            def idx_step(b, cut):
                cand = cut + lax.shift_left(i32(1), i32(idx_bits - 1) - b)
                cnt = count(lambda kk, r0: jnp.where(kk == thr, r0 + s_off, cand) < cand)
                return jnp.where(cnt < need, cand, cut)

            cut = lax.fori_loop(0, idx_bits, idx_step, jnp.zeros((1, tq), i32))
            cut = jnp.where(n_ge > top_k, cut, jnp.iinfo(i32).max)

            def drop(c, carry):
                r0 = pl.multiple_of(c * tk, tk)
                kk = keys_scr[pl.ds(r0, tk), :]
                excess = jnp.where(kk == thr, r0 + s_off, INT_MIN) > cut
                keys_scr[pl.ds(r0, tk), :] = jnp.where(excess, INT_MIN, kk)
                return carry

            lax.fori_loop(0, n_chunks, drop, 0)

        m_scr[...] = jnp.full_like(m_scr, NEG_BIG)
        l_scr[...] = jnp.zeros_like(l_scr)
        acc_scr[...] = jnp.zeros_like(acc_scr)

    sel = keys_scr[pl.ds(pl.multiple_of(kj * tk, tk), tk), :] >= thr_scr[0:1, :]
    for hd in range(ATT_HEADS):
        hs = slice(hd * ATT_HEAD_DIM, (hd + 1) * ATT_HEAD_DIM)
        logit = jnp.where(sel, _dot(k_ref[:, hs], qt_ref[hs, :]), -jnp.inf)
        m_old = m_scr[hd:hd + 1, :]
        m_new = jnp.maximum(m_old, jnp.max(logit, axis=0, keepdims=True))
        prob = jnp.exp(logit - m_new)
        alpha = jnp.exp(m_old - m_new)
        l_scr[hd:hd + 1, :] = alpha * l_scr[hd:hd + 1, :] + jnp.sum(prob, axis=0, keepdims=True)
        acc_scr[hd] = alpha * acc_scr[hd] + _dot(vt_ref[hs, :], prob.astype(MXU_DTYPE))
        m_scr[hd:hd + 1, :] = m_new

    @pl.when(kj == (qi * tq) // tk + (tq // tk - 1))
    def _finish():
        for hd in range(ATT_HEADS):
            hs = slice(hd * ATT_HEAD_DIM, (hd + 1) * ATT_HEAD_DIM)
            o_ref[:, hs] = (acc_scr[hd] / l_scr[hd:hd + 1, :]).T.astype(o_ref.dtype)


def _dsa(qit, lohi, ki, qt, k, vt):
    seq = k.shape[0]
    tq, tk = min(DSA_Q_TILE, seq), min(DSA_K_TILE, seq)
    assert (tq, tk) == (DSA_Q_TILE, DSA_K_TILE) and tq == tk and seq % tq == 0
    nq = seq // tq
    pairs = [(i, j) for i in range(nq) for j in range(i + 1)]
    qi_idx = jnp.asarray([pq for pq, _ in pairs], jnp.int32)
    kj_idx = jnp.asarray([pk for _, pk in pairs], jnp.int32)
    top_k = min(TOPK_MAX, seq // 4)
    idx_bits = max(1, (seq - 1).bit_length())
    grid_spec = pltpu.PrefetchScalarGridSpec(
        num_scalar_prefetch=2,
        grid=(len(pairs),),
        in_specs=[
            pl.BlockSpec((IDX_INNER, tq), lambda p, qi, kj: (0, qi[p])),
            pl.BlockSpec((16, tq), lambda p, qi, kj: (0, qi[p])),
            pl.BlockSpec((seq, IDX_HEAD_DIM), lambda p, qi, kj: (0, 0), pipeline_mode=pl.Buffered(1)),
            pl.BlockSpec((ATT_INNER, tq), lambda p, qi, kj: (0, qi[p])),
            pl.BlockSpec((tk, ATT_INNER), lambda p, qi, kj: (kj[p], 0)),
            pl.BlockSpec((ATT_INNER, tk), lambda p, qi, kj: (0, kj[p])),
        ],
        out_specs=pl.BlockSpec((tq, ATT_INNER), lambda p, qi, kj: (qi[p], 0)),
        scratch_shapes=[pltpu.VMEM((seq, tq), jnp.int32),
                        pltpu.VMEM((8, tq), jnp.int32),
                        pltpu.VMEM((ATT_HEADS, tq), jnp.float32),
                        pltpu.VMEM((ATT_HEADS, tq), jnp.float32),
                        pltpu.VMEM((ATT_HEADS, ATT_HEAD_DIM, tq), jnp.float32)],
    )
    return pl.pallas_call(
        functools.partial(_dsa_body, top_k=top_k, idx_bits=idx_bits),
        out_shape=jax.ShapeDtypeStruct((seq, ATT_INNER), MXU_DTYPE),
        grid_spec=grid_spec,
        compiler_params=_params(("arbitrary",)),
        name="dsa",
    )(qi_idx, kj_idx, qit, lohi, ki, qt, k, vt)


def _even_out_body(h_ref, y_ref, o_ref, wy_ref, wo_ref, out_ref):
    out_ref[...] = h_ref[...] + _dot(y_ref[...], wy_ref[...]) + _dot(o_ref[...], wo_ref[...])


def _even_out(h, y, o, wy, wo):
    seq = h.shape[0]
    tm = min(ROW_TILE, seq)
    row = pl.BlockSpec((tm, D_MODEL), lambda i: (i, 0))
    return pl.pallas_call(
        _even_out_body,
        out_shape=jax.ShapeDtypeStruct((seq, D_MODEL), jnp.float32),
        grid=(seq // tm,),
        in_specs=[row, row, row, _const_spec((SSD_INNER, D_MODEL)), _const_spec((ATT_INNER, D_MODEL))],
        out_specs=row,
        compiler_params=_params(("parallel",)),
        name="even_out",
    )(h, y, o, wy, wo)


def _odd_body(h_ref, g_ref, win_ref, lng_ref, lnb_ref, ws_ref, bs_ref, wout_ref, o_ref, gated_scr):
    h = h_ref[...]
    tm = h.shape[0]
    xn = _rms(h, g_ref[...]).astype(MXU_DTYPE)
    u = jax.nn.gelu(_dot(xn, win_ref[:, :SG_INNER]))
    v = jax.nn.gelu(_dot(xn, win_ref[:, SG_INNER:]))
    mu = jnp.mean(v, axis=-1, keepdims=True)
    vc = v - mu
    var = jnp.mean(vc * vc, axis=-1, keepdims=True)
    v = (vc * lax.rsqrt(var + EPS) * lng_ref[...] + lnb_ref[...]).astype(MXU_DTYPE)
    ri = lax.broadcasted_iota(jnp.int32, (SG_CHUNK, SG_CHUNK), 0)
    ci = lax.broadcasted_iota(jnp.int32, (SG_CHUNK, SG_CHUNK), 1)
    gw = SG_INNER // SG_GROUPS
    for g in range(SG_GROUPS):
        w = jnp.where(ri >= ci, ws_ref[g], 0.0).astype(MXU_DTYPE)
        gs = slice(g * gw, (g + 1) * gw)
        for c in range(tm // SG_CHUNK):
            rs = slice(c * SG_CHUNK, (c + 1) * SG_CHUNK)
            mixed = _dot(w, v[rs, gs]) + bs_ref[:, gs]
            gated_scr[rs, gs] = (u[rs, gs] * mixed).astype(gated_scr.dtype)
    o_ref[...] = h + _dot(gated_scr[...], wout_ref[...])


def _odd(h, g, win, ln_g, ln_b, w_s, bs_full, wout):
    seq = h.shape[0]
    tm = min(ROW_TILE, seq)
    row = pl.BlockSpec((tm, D_MODEL), lambda i: (i, 0))
    return pl.pallas_call(
        _odd_body,
        out_shape=jax.ShapeDtypeStruct((seq, D_MODEL), jnp.float32),
        grid=(seq // tm,),
        in_specs=[row, _const_spec((1, D_MODEL)), _const_spec((D_MODEL, 2 * SG_INNER)),
                  _const_spec((1, SG_INNER)), _const_spec((1, SG_INNER)),
                  _const_spec((SG_GROUPS, SG_CHUNK, SG_CHUNK)), _const_spec((SG_CHUNK, SG_INNER)),
                  _const_spec((SG_INNER, D_MODEL))],
        out_specs=row,
        scratch_shapes=[pltpu.VMEM((tm, SG_INNER), MXU_DTYPE)],
        compiler_params=_params(("parallel",)),
        name="odd",
    )(h, g, win, ln_g, ln_b, w_s, bs_full, wout)


def _rope_inputs(seq):
    def tables(rot_dim):
        inv = 1.0 / (ROPE_THETA ** (jnp.arange(0, rot_dim, 2, dtype=jnp.float32) / rot_dim))
        ang = jnp.arange(seq, dtype=jnp.float32)[:, None] * inv[None, :]
        return jnp.cos(ang), jnp.sin(ang)

    def lane_tables(cos, sin):
        half = cos.shape[1]
        pad = LANES - 2 * half
        c = jnp.concatenate([cos, cos, jnp.ones((seq, pad), jnp.float32)], axis=1)
        s_lo = jnp.concatenate([-sin, jnp.zeros((seq, LANES - half), jnp.float32)], axis=1)
        s_hi = jnp.concatenate([jnp.zeros((seq, half), jnp.float32), sin,
                                jnp.zeros((seq, pad), jnp.float32)], axis=1)
        return c, s_lo, s_hi

    cos_a, sin_a = tables(ATT_HEAD_DIM // ROPE_FRACTION)
    cos_i, sin_i = tables(IDX_HEAD_DIM // ROPE_FRACTION)
    return (*lane_tables(cos_a, sin_a), *lane_tables(cos_i, sin_i), cos_a.T, sin_a.T, cos_i.T, sin_i.T)


def _even_weights(w_in):
    offs = np.cumsum((SSD_INNER, SSD_CONV_DIM, SSD_HEADS, ATT_INNER, ATT_INNER, ATT_INNER,
                      IDX_INNER, IDX_HEAD_DIM, IDX_HEADS))[:-1].tolist()
    z, xbc, dt, q, k, v, qi, ki, wi = jnp.split(w_in, offs, axis=-1)
    pad = jnp.zeros((D_MODEL, LANES - IDX_HEAD_DIM - SSD_HEADS), w_in.dtype)
    wn = jnp.concatenate([z, xbc, k, ki, dt, pad], axis=1).astype(MXU_DTYPE)
    wi_pad = jnp.zeros((D_MODEL, _T_DT - _T_WI - IDX_HEADS), w_in.dtype)
    wt = jnp.concatenate([q, v, qi, wi, wi_pad, dt], axis=1).T.astype(MXU_DTYPE)
    assert wn.shape == (D_MODEL, _N_END) and wt.shape == (_T_END, D_MODEL)
    return wn, wt


def _even_mixer(h, g, w_in, conv_w, conv_b, dt_bias, a_log, d_skip, gate_norm, w_out, rope):
    wn, wt = _even_weights(w_in)
    z, xbc, small, k, ki, qt, vt, qit, lohi, dtt = _even_in(h, g, wn, wt, rope)
    y = _ssd(xbc, small, dtt, z, conv_w, conv_b, dt_bias, a_log, d_skip, gate_norm)
    o = _dsa(qit, lohi, ki, qt, k, vt)
    w_out = w_out.astype(MXU_DTYPE)
    return _even_out(h, y, o, w_out[:SSD_INNER], w_out[SSD_INNER:])


def _forward(x, norm_g, final_g, ffn_w_gu, ffn_w_down, ev_w_in, ev_conv_w, ev_conv_b, ev_dt_bias,
             ev_a_log, ev_d, ev_gate_norm, ev_w_out, od_w_in, od_ln_g, od_ln_b, od_w_s, od_b_s, od_w_out):
    bsz, seq, _ = x.shape
    depth = norm_g.shape[0]
    rope = _rope_inputs(seq)
    wg = ffn_w_gu[..., :FFN_HIDDEN].astype(MXU_DTYPE)
    wu = ffn_w_gu[..., FFN_HIDDEN:].astype(MXU_DTYPE)
    wd = ffn_w_down.astype(MXU_DTYPE)
    fg = final_g[None, :]
    outs = []
    for b in range(bsz):
        h = x[b]
        for layer in range(depth):
            j = layer // 2
            h = _ffn(h, norm_g[layer, 0][None, :], wg[layer, 0], wu[layer, 0], wd[layer, 0], fg, False)
            g1 = norm_g[layer, 1][None, :]
            if layer % 2 == 0:
                h = _even_mixer(h, g1, ev_w_in[j], ev_conv_w[j], ev_conv_b[j], ev_dt_bias[j], ev_a_log[j],
                                ev_d[j], ev_gate_norm[j], ev_w_out[j], rope)
            else:
                bs_full = jnp.repeat(od_b_s[j].T, SG_INNER // SG_GROUPS, axis=1)
                h = _odd(h, g1, od_w_in[j].astype(MXU_DTYPE), od_ln_g[j][None, :], od_ln_b[j][None, :],
                         od_w_s[j], bs_full, od_w_out[j].astype(MXU_DTYPE))
            h = _ffn(h, norm_g[layer, 2][None, :], wg[layer, 1], wu[layer, 1], wd[layer, 1], fg,
                     layer == depth - 1)
        outs.append(h)
    return jnp.stack(outs, axis=0)


def kernel(x, norm_g, final_g, ffn_w_gu, ffn_w_down, ev_w_in, ev_conv_w, ev_conv_b, ev_dt_bias, ev_a_log,
           ev_d, ev_gate_norm, ev_w_out, od_w_in, od_ln_g, od_ln_b, od_w_s, od_b_s, od_w_out):
    return _forward(x, norm_g, final_g, ffn_w_gu, ffn_w_down, ev_w_in, ev_conv_w, ev_conv_b, ev_dt_bias,
                    ev_a_log, ev_d, ev_gate_norm, ev_w_out, od_w_in, od_ln_g, od_ln_b, od_w_s, od_b_s,
                    od_w_out)
```

```python
import functools
import math

import numpy as np
import jax
import jax.numpy as jnp
from jax import lax
from jax.experimental import pallas as pl
from jax.experimental.pallas import tpu as pltpu

D_MODEL = 1024
SSD_HEADS = 16
SSD_HEAD_DIM = 64
SSD_INNER = SSD_HEADS * SSD_HEAD_DIM
SSD_GROUPS = 2
SSD_STATE = 128
SSD_CONV = 4
SSD_CHUNK = 256
SSD_CONV_DIM = SSD_INNER + 2 * SSD_GROUPS * SSD_STATE
ATT_HEADS = 8
ATT_HEAD_DIM = 128
ATT_INNER = ATT_HEADS * ATT_HEAD_DIM
IDX_HEADS = 8
IDX_HEAD_DIM = 64
IDX_INNER = IDX_HEADS * IDX_HEAD_DIM
TOPK_MAX = 256
SG_CHUNK = 128
SG_GROUPS = 8
SG_INNER = 2 * D_MODEL
FFN_HIDDEN = 2816
ROPE_THETA = 500000.0
ROPE_FRACTION = 4
EPS = 1e-6

ATT_ROT_HALF = ATT_HEAD_DIM // ROPE_FRACTION // 2
IDX_ROT_HALF = IDX_HEAD_DIM // ROPE_FRACTION // 2

MXU_DTYPE = jnp.bfloat16
LANES = 128
VMEM_LIMIT_BYTES = 56 * 1024 * 1024

ROW_TILE = 512
FFN_HIDDEN_CHUNK = FFN_HIDDEN // 2
DSA_Q_TILE = 256
DSA_K_TILE = 256
SAMPLE_STRIDE = 16
SAMPLE_ROWS = 128
SEED_BITS = 20
INT_MIN = -(2 ** 31)
NEG_BIG = -0.7 * float(np.finfo(np.float32).max)

_N_Z = 0
_N_XBC = _N_Z + SSD_INNER
_N_K = _N_XBC + SSD_CONV_DIM
_N_SMALL = _N_K + ATT_INNER
_N_END = _N_SMALL + LANES
_SM_KI = 0
_SM_DT = IDX_HEAD_DIM
_T_Q = 0
_T_V = _T_Q + ATT_INNER
_T_QI = _T_V + ATT_INNER
_T_WI = _T_QI + IDX_INNER
_T_DT = _T_WI + 16
_T_END = _T_DT + SSD_HEADS


def _dot(a, b):
    return jnp.dot(a, b, preferred_element_type=jnp.float32)


def _rms(x, g):
    return x * lax.rsqrt(jnp.mean(x * x, axis=-1, keepdims=True) + EPS) * g


def _softplus(x):
    return jnp.maximum(x, 0.0) + jnp.log1p(jnp.exp(-jnp.abs(x)))


def _split3(x):
    hi = x.astype(MXU_DTYPE)
    r = x - hi.astype(jnp.float32)
    mid = r.astype(MXU_DTYPE)
    lo = (r - mid.astype(jnp.float32)).astype(MXU_DTYPE)
    return hi, mid, lo


def _const_spec(shape):
    zeros = (0,) * len(shape)
    return pl.BlockSpec(shape, lambda *_: zeros, pipeline_mode=pl.Buffered(1))


def _params(sem):
    return pltpu.CompilerParams(dimension_semantics=sem, vmem_limit_bytes=VMEM_LIMIT_BYTES)


def _ffn_body(h_ref, g_ref, wg_ref, wu_ref, wd_ref, fg_ref, o_ref, *, final):
    h = h_ref[...]
    xn = _rms(h, g_ref[...]).astype(MXU_DTYPE)
    acc = None
    for c in range(FFN_HIDDEN // FFN_HIDDEN_CHUNK):
        sl = slice(c * FFN_HIDDEN_CHUNK, (c + 1) * FFN_HIDDEN_CHUNK)
        gate = _dot(xn, wg_ref[:, sl])
        up = _dot(xn, wu_ref[:, sl])
        act = (gate * jax.nn.sigmoid(gate) * up).astype(MXU_DTYPE)
        part = _dot(act, wd_ref[sl, :])
        acc = part if acc is None else acc + part
    out = h + 0.5 * acc
    if final:
        out = _rms(out, fg_ref[...])
    o_ref[...] = out


def _ffn(h, g, wg, wu, wd, final_g, final):
    seq = h.shape[0]
    tm = min(ROW_TILE, seq)
    row = pl.BlockSpec((tm, D_MODEL), lambda i: (i, 0))
    return pl.pallas_call(
        functools.partial(_ffn_body, final=final),
        out_shape=jax.ShapeDtypeStruct((seq, D_MODEL), jnp.float32),
        grid=(seq // tm,),
        in_specs=[row, _const_spec((1, D_MODEL)), _const_spec((D_MODEL, FFN_HIDDEN)),
                  _const_spec((D_MODEL, FFN_HIDDEN)), _const_spec((FFN_HIDDEN, D_MODEL)),
                  _const_spec((1, D_MODEL))],
        out_specs=row,
        compiler_params=_params(("parallel",)),
        name="ffn",
    )(h, g, wg, wu, wd, final_g)


def _rope_lanes(x, c, s_lo, s_hi, half):
    return x * c + pltpu.roll(x, LANES - half, 1) * s_lo + pltpu.roll(x, half, 1) * s_hi


def _rope_rows(x, c, s, half):
    x1, x2 = x[:half], x[half:2 * half]
    return jnp.concatenate([x1 * c - x2 * s, x1 * s + x2 * c, x[2 * half:]], axis=0)


def _even_in_body(h_ref, g_ref, wn_ref, wt_ref, ca_ref, sal_ref, sah_ref, ci_ref, sil_ref, sih_ref,
                  cat_ref, sat_ref, cit_ref, sit_ref,
                  z_ref, xbc_ref, small_ref, k_ref, ki_ref, qt_ref, vt_ref, qit_ref, lohi_ref, dtt_ref):
    xn = _rms(h_ref[...], g_ref[...])
    xnb = xn.astype(MXU_DTYPE)
    xnt = xn.T.astype(MXU_DTYPE)

    z_ref[...] = _dot(xnb, wn_ref[:, _N_Z:_N_XBC])
    xbc_ref[...] = _dot(xnb, wn_ref[:, _N_XBC:_N_K])
    kk = _dot(xnb, wn_ref[:, _N_K:_N_SMALL])
    ca, sal, sah = ca_ref[...], sal_ref[...], sah_ref[...]
    for hd in range(ATT_HEADS):
        sl = slice(hd * ATT_HEAD_DIM, (hd + 1) * ATT_HEAD_DIM)
        k_ref[:, sl] = _rope_lanes(kk[:, sl], ca, sal, sah, ATT_ROT_HALF).astype(k_ref.dtype)
    sm = _dot(xnb, wn_ref[:, _N_SMALL:_N_END])
    small_ref[...] = sm
    smr = _rope_lanes(sm, ci_ref[...], sil_ref[...], sih_ref[...], IDX_ROT_HALF)
    ki_ref[...] = smr[:, _SM_KI:_SM_KI + IDX_HEAD_DIM].astype(ki_ref.dtype)

    att_scale = ATT_HEAD_DIM ** -0.5 * math.log2(math.e)
    qt = _dot(wt_ref[_T_Q:_T_V, :], xnt)
    cat, sat = cat_ref[...], sat_ref[...]
    for hd in range(ATT_HEADS):
        sl = slice(hd * ATT_HEAD_DIM, (hd + 1) * ATT_HEAD_DIM)
        qt_ref[sl, :] = (_rope_rows(qt[sl], cat, sat, ATT_ROT_HALF) * att_scale).astype(qt_ref.dtype)
    vt_ref[...] = _dot(wt_ref[_T_V:_T_QI, :], xnt).astype(vt_ref.dtype)
    tail = _dot(wt_ref[_T_WI:_T_END, :], xnt)
    wit = tail[0:IDX_HEADS] * (IDX_HEADS ** -0.5 * IDX_HEAD_DIM ** -0.5)
    dtt_ref[...] = tail[_T_DT - _T_WI:_T_END - _T_WI]
    pos = wit > 0.0
    lohi_ref[0:8, :] = jnp.where(pos, 0.0, -jnp.inf)
    lohi_ref[8:16, :] = jnp.where(pos, jnp.inf, 0.0)
    qit = _dot(wt_ref[_T_QI:_T_WI, :], xnt)
    cit, sit = cit_ref[...], sit_ref[...]
    for hd in range(IDX_HEADS):
        sl = slice(hd * IDX_HEAD_DIM, (hd + 1) * IDX_HEAD_DIM)
        qit_ref[sl, :] = (_rope_rows(qit[sl], cit, sit, IDX_ROT_HALF) * wit[hd:hd + 1]).astype(qit_ref.dtype)


def _even_in(h, g, wn, wt, rope):
    seq = h.shape[0]
    tm = min(ROW_TILE, seq)
    f32 = jnp.float32

    def rows(width):
        return pl.BlockSpec((tm, width), lambda i: (i, 0))

    def cols(height):
        return pl.BlockSpec((height, tm), lambda i: (0, i))

    out_shape = (
        jax.ShapeDtypeStruct((seq, SSD_INNER), f32),
        jax.ShapeDtypeStruct((seq, SSD_CONV_DIM), f32),
        jax.ShapeDtypeStruct((seq, LANES), f32),
        jax.ShapeDtypeStruct((seq, ATT_INNER), MXU_DTYPE),
        jax.ShapeDtypeStruct((seq, IDX_HEAD_DIM), MXU_DTYPE),
        jax.ShapeDtypeStruct((ATT_INNER, seq), MXU_DTYPE),
        jax.ShapeDtypeStruct((ATT_INNER, seq), MXU_DTYPE),
        jax.ShapeDtypeStruct((IDX_INNER, seq), MXU_DTYPE),
        jax.ShapeDtypeStruct((16, seq), f32),
        jax.ShapeDtypeStruct((SSD_HEADS, seq), f32),
    )
    out_specs = (rows(SSD_INNER), rows(SSD_CONV_DIM), rows(LANES), rows(ATT_INNER), rows(IDX_HEAD_DIM),
                 cols(ATT_INNER), cols(ATT_INNER), cols(IDX_INNER), cols(16), cols(SSD_HEADS))
    in_specs = [rows(D_MODEL), _const_spec((1, D_MODEL)), _const_spec(wn.shape), _const_spec(wt.shape)]
    in_specs += [rows(LANES)] * 6
    in_specs += [cols(ATT_ROT_HALF)] * 2 + [cols(IDX_ROT_HALF)] * 2
    return pl.pallas_call(
        _even_in_body,
        out_shape=out_shape,
        grid=(seq // tm,),
        in_specs=in_specs,
        out_specs=out_specs,
        compiler_params=_params(("parallel",)),
        name="even_in",
    )(h, g, wn, wt, *rope)


def _ssd_body(xbc_ref, small_ref, dtt_ref, z_ref, cw_ref, cb_ref, dtb_ref, dtbt_ref, a_ref, at_ref,
              drep_ref, gn_ref, o_ref, tail_scr, st_scr, y_scr, xdec_scr, decay_scr):
    q = SSD_CHUNK
    f32 = jnp.float32

    @pl.when(pl.program_id(0) == 0)
    def _():
        tail_scr[...] = jnp.zeros_like(tail_scr)
        st_scr[...] = jnp.zeros_like(st_scr)

    x = xbc_ref[...]
    tail = tail_scr[...]
    row8 = lax.broadcasted_iota(jnp.int32, (8, SSD_CONV_DIM), 0)
    conv = x * cw_ref[SSD_CONV - 1:SSD_CONV, :] + cb_ref[...]
    for shift in range(1, SSD_CONV):
        rolled = pltpu.roll(x, shift, 0)
        head = jnp.where(row8 < shift, pltpu.roll(tail, shift, 0), rolled[0:8])
        shifted = jnp.concatenate([head, rolled[8:]], axis=0)
        conv = conv + shifted * cw_ref[SSD_CONV - 1 - shift:SSD_CONV - shift, :]
    tail_scr[...] = x[q - 8:q]
    xbc = conv * jax.nn.sigmoid(conv)
    xs = xbc[:, :SSD_INNER]
    bm = xbc[:, SSD_INNER:SSD_INNER + SSD_GROUPS * SSD_STATE]
    cm = xbc[:, SSD_INNER + SSD_GROUPS * SSD_STATE:]

    dt_col = _softplus(small_ref[:, _SM_DT:_SM_DT + SSD_HEADS] + dtb_ref[...])
    dt_row = _softplus(dtt_ref[...] + dtbt_ref[...])
    adt_col = dt_col * (-jnp.exp(a_ref[...]))
    adt_row = dt_row * (-jnp.exp(at_ref[...]))
    ri = lax.broadcasted_iota(jnp.int32, (q, q), 0)
    ci = lax.broadcasted_iota(jnp.int32, (q, q), 1)
    causal = ri >= ci
    tril = jnp.where(causal, 1.0, 0.0).astype(MXU_DTYPE)
    triu = jnp.where(ri <= ci, 1.0, 0.0).astype(MXU_DTYPE)
    cs_col = sum(_dot(tril, p) for p in _split3(adt_col))
    cs_row = sum(_dot(p, triu) for p in _split3(adt_row))

    cmb = cm.astype(MXU_DTYPE)
    bmb = bm.astype(MXU_DTYPE)
    for g in range(SSD_GROUPS):
        gs = slice(g * SSD_STATE, (g + 1) * SSD_STATE)
        cb = lax.dot_general(cmb[:, gs], bmb[:, gs], (((1,), (1,)), ((), ())),
                             preferred_element_type=f32)
        kpg = SSD_HEADS // SSD_GROUPS
        for hd in range(g * kpg, (g + 1) * kpg):
            hs = slice(hd * SSD_HEAD_DIM, (hd + 1) * SSD_HEAD_DIM)
            col = cs_col[:, hd:hd + 1]
            row = cs_row[hd:hd + 1, :]
            last = cs_row[hd:hd + 1, q - 1:q]
            decay = jnp.exp(jnp.where(causal, col - row, -jnp.inf))
            xdt = xs[:, hs] * dt_col[:, hd:hd + 1]
            y = _dot((cb * decay).astype(MXU_DTYPE), xdt.astype(MXU_DTYPE))
            y_off = _dot(cmb[:, gs], st_scr[:, hs].astype(MXU_DTYPE))
            y_scr[:, hs] = y + y_off * jnp.exp(col)
            xdec_scr[:, hs] = (xdt * jnp.exp(last - col)).astype(xdec_scr.dtype)
            decay_scr[:, hs] = jnp.broadcast_to(jnp.exp(last), (1, SSD_HEAD_DIM))
        ws = slice(g * kpg * SSD_HEAD_DIM, (g + 1) * kpg * SSD_HEAD_DIM)
        bmt = bm[:, gs].T.astype(MXU_DTYPE)
        st_scr[:, ws] = st_scr[:, ws] * decay_scr[:, ws] + _dot(bmt, xdec_scr[:, ws])

    z = z_ref[...]
    y = (y_scr[...] + drep_ref[...] * xs) * (z * jax.nn.sigmoid(z))
    gw = SSD_INNER // SSD_GROUPS
    for g in range(SSD_GROUPS):
        sl = slice(g * gw, (g + 1) * gw)
        seg = y[:, sl]
        seg = seg * lax.rsqrt(jnp.mean(seg * seg, axis=-1, keepdims=True) + EPS)
        o_ref[:, sl] = (seg * gn_ref[:, sl]).astype(o_ref.dtype)


def _ssd(xbc, small, dtt, z, conv_w, conv_b, dt_bias, a_log, d_skip, gate_norm):
    seq = xbc.shape[0]
    q = SSD_CHUNK
    f32 = jnp.float32

    def rows(width):
        return pl.BlockSpec((q, width), lambda i: (i, 0))

    drep = jnp.repeat(d_skip, SSD_HEAD_DIM)[None, :]
    return pl.pallas_call(
        _ssd_body,
        out_shape=jax.ShapeDtypeStruct((seq, SSD_INNER), MXU_DTYPE),
        grid=(seq // q,),
        in_specs=[rows(SSD_CONV_DIM), rows(LANES), pl.BlockSpec((SSD_HEADS, q), lambda i: (0, i)),
                  rows(SSD_INNER), _const_spec((SSD_CONV, SSD_CONV_DIM)), _const_spec((1, SSD_CONV_DIM)),
                  _const_spec((1, SSD_HEADS)), _const_spec((SSD_HEADS, 1)),
                  _const_spec((1, SSD_HEADS)), _const_spec((SSD_HEADS, 1)),
                  _const_spec((1, SSD_INNER)), _const_spec((1, SSD_INNER))],
        out_specs=rows(SSD_INNER),
        scratch_shapes=[pltpu.VMEM((8, SSD_CONV_DIM), f32),
                        pltpu.VMEM((SSD_STATE, SSD_INNER), f32),
                        pltpu.VMEM((q, SSD_INNER), f32),
                        pltpu.VMEM((q, SSD_INNER), MXU_DTYPE),
                        pltpu.VMEM((1, SSD_INNER), f32)],
        compiler_params=_params(("arbitrary",)),
        name="ssd",
    )(xbc, small, dtt, z, conv_w, conv_b[None, :], dt_bias[None, :], dt_bias[:, None],
      a_log[None, :], a_log[:, None], drep, gate_norm[None, :])


def _dsa_body(qi_ref, kj_ref, qit_ref, lohi_ref, ki_ref, qt_ref, k_ref, vt_ref, o_ref,
              keys_scr, sample_scr, thr_scr, m_scr, l_scr, acc_scr, logit_scr, *, top_k, idx_bits):
    tq, tk = DSA_Q_TILE, DSA_K_TILE
    i32 = jnp.int32
    p = pl.program_id(0)
    qi = qi_ref[p]
    kj = kj_ref[p]

    @pl.when(kj == 0)
    def _select():
        n_chunks = (qi + 1) * (tq // tk)
        lo = lohi_ref[0:8, :]
        hi = lohi_ref[8:16, :]
        t_pos = qi * tq + lax.broadcasted_iota(i32, (tk, tq), 1)
        s_off = lax.broadcasted_iota(i32, (tk, tq), 0)

        def score_chunk(c, carry):
            r0 = pl.multiple_of(c * tk, tk)
            kic = ki_ref[pl.ds(r0, tk), :]
            sc = jnp.zeros((tk, tq), jnp.float32)
            for hd in range(IDX_HEADS):
                y = _dot(kic, qit_ref[hd * IDX_HEAD_DIM:(hd + 1) * IDX_HEAD_DIM, :])
                sc = sc + jnp.minimum(jnp.maximum(y, lo[hd:hd + 1]), hi[hd:hd + 1])
            bits = pltpu.bitcast(sc, i32)
            key = bits ^ ((bits >> 31) & 0x7FFFFFFF)
            key = jnp.where(sc == 0.0, 0, key)
            key = jnp.where(r0 + s_off <= t_pos, key, INT_MIN)
            keys_scr[pl.ds(r0, tk), :] = key
            s0 = pl.multiple_of(c * (tk // SAMPLE_STRIDE), tk // SAMPLE_STRIDE)
            sample_scr[pl.ds(s0, tk // SAMPLE_STRIDE), :] = jnp.concatenate(
                [key[g * 8 * SAMPLE_STRIDE:g * 8 * SAMPLE_STRIDE + 8] for g in range(tk // (8 * SAMPLE_STRIDE))],
                axis=0)
            return carry

        sample_scr[...] = jnp.full_like(sample_scr, INT_MIN)
        lax.fori_loop(0, n_chunks, score_chunk, 0)

        def count_rows(ref, n_iter, rows, preds):
            lanes_of_partials = 4 * 8

            def body(c, accs):
                r0 = pl.multiple_of(c * rows, rows)
                kk = ref[pl.ds(r0, rows), :]
                out = []
                for pred, acc in zip(preds, accs):
                    hit = jnp.where(pred(kk, r0), 1.0, 0.0)
                    out.append(acc + hit.reshape(rows // lanes_of_partials, lanes_of_partials, tq).sum(axis=0))
                return tuple(out)

            zero = jnp.zeros((lanes_of_partials, tq), jnp.float32)
            accs = lax.fori_loop(0, n_iter, body, tuple(zero for _ in preds))
            return [acc.sum(axis=0, keepdims=True) for acc in accs]

        def count(pred):
            return count_rows(keys_scr, n_chunks, tk, [pred])[0]

        k_f = jnp.float32(top_k)
        n_t = (qi * tq + 1 + lax.broadcasted_iota(i32, (1, tq), 1)).astype(jnp.float32)
        n_samp_iter = (n_chunks * (tk // SAMPLE_STRIDE) + SAMPLE_ROWS - 1) // SAMPLE_ROWS
        n_s = count_rows(sample_scr, n_samp_iter, SAMPLE_ROWS, [lambda kk, r0: kk > INT_MIN])[0]
        mu = k_f * n_s / n_t
        spread = 4.0 * jnp.sqrt(mu)
        rank_hi = jnp.maximum(1.0, jnp.floor(mu - spread))
        rank_lo = jnp.ceil(mu + spread) + 1.0

        def seed_step(b, carry):
            t_lo, t_hi = carry
            step = lax.shift_left(i32(1), i32(31) - b)
            c_lo, c_hi = t_lo + step, t_hi + step
            n_lo, n_hi = count_rows(sample_scr, n_samp_iter, SAMPLE_ROWS,
                                    [lambda kk, r0: kk >= c_lo, lambda kk, r0: kk >= c_hi])
            return jnp.where(n_lo >= rank_lo, c_lo, t_lo), jnp.where(n_hi >= rank_hi, c_hi, t_hi)

        start = jnp.full((1, tq), INT_MIN, i32)
        seed_lo, seed_hi = lax.fori_loop(0, SEED_BITS, seed_step, (start, start))

        def search_cond(carry):
            return jnp.logical_and(carry[0] < 3 * 32 + 2, carry[1] > 0)

        def search_step(carry):
            it, _, lo_k, hi_k, c_lo, c_hi = carry
            done = (c_lo <= k_f) | (lo_k + 1 >= hi_k)
            lo_f, hi_f = lo_k.astype(jnp.float32), hi_k.astype(jnp.float32)
            frac = (c_lo - k_f + 0.5) / jnp.maximum(c_lo - c_hi, 1.0)
            guess = jnp.clip(lo_f + frac * (hi_f - lo_f), -2.0 ** 31 + 256.0, 2.0 ** 31 - 256.0).astype(i32)
            mid = (lo_k >> 1) + (hi_k >> 1) + (lo_k & hi_k & 1)
            cand = jnp.where(it == 0, seed_lo, jnp.where(it == 1, seed_hi, jnp.where(it % 3 == 1, mid, guess)))
            cand = jnp.where(done, lo_k, jnp.maximum(lo_k + 1, jnp.minimum(hi_k - 1, cand)))
            cnt = count(lambda kk, r0: kk >= cand)
            up = jnp.logical_and(~done, cnt >= k_f)
            down = jnp.logical_and(~done, cnt < k_f)
            lo_k, c_lo = jnp.where(up, cand, lo_k), jnp.where(up, cnt, c_lo)
            hi_k, c_hi = jnp.where(down, cand, hi_k), jnp.where(down, cnt, c_hi)
            still = jnp.where((c_lo <= k_f) | (lo_k + 1 >= hi_k), 0.0, 1.0)
            return it + 1, jnp.max(still).astype(i32), lo_k, hi_k, c_lo, c_hi

        init = (i32(0), i32(1), jnp.full((1, tq), INT_MIN + 1, i32), jnp.full((1, tq), jnp.iinfo(i32).max, i32),
                n_t, jnp.zeros((1, tq), jnp.float32))
        _, _, thr, _, n_ge, _ = lax.while_loop(search_cond, search_step, init)
        thr_scr[0:1, :] = thr

        @pl.when(jnp.max(n_ge) > top_k)
        def _ties():
            need = top_k - count(lambda kk, r0: kk > thr)

            def idx_step(b, cut):
                cand = cut + lax.shift_left(i32(1), i32(idx_bits - 1) - b)
                cnt = count(lambda kk, r0: jnp.where(kk == thr, r0 + s_off, cand) < cand)
                return jnp.where(cnt < need, cand, cut)

            cut = lax.fori_loop(0, idx_bits, idx_step, jnp.zeros((1, tq), i32))
            cut = jnp.where(n_ge > top_k, cut, jnp.iinfo(i32).max)

            def drop(c, carry):
                r0 = pl.multiple_of(c * tk, tk)
                kk = keys_scr[pl.ds(r0, tk), :]
                excess = jnp.where(kk == thr, r0 + s_off, INT_MIN) > cut
                keys_scr[pl.ds(r0, tk), :] = jnp.where(excess, INT_MIN, kk)
                return carry

            lax.fori_loop(0, n_chunks, drop, 0)

        m_scr[...] = jnp.full_like(m_scr, NEG_BIG)
        l_scr[...] = jnp.zeros_like(l_scr)
        acc_scr[...] = jnp.zeros_like(acc_scr)

    sel = keys_scr[pl.ds(pl.multiple_of(kj * tk, tk), tk), :] >= thr_scr[0:1, :]
    bias = jnp.where(sel, 0.0, -jnp.inf)
    col_max = []
    for hd in range(ATT_HEADS):
        hs = slice(hd * ATT_HEAD_DIM, (hd + 1) * ATT_HEAD_DIM)
        logit = _dot(k_ref[:, hs], qt_ref[hs, :]) + bias
        logit_scr[hd] = logit
        col_max.append(jnp.max(logit, axis=0, keepdims=True))
    m_old = m_scr[...]
    m_new = jnp.maximum(m_old, jnp.concatenate(col_max, axis=0))
    alpha = jnp.exp2(m_old - m_new)
    m_scr[...] = m_new
    col_sum = []
    for hd in range(ATT_HEADS):
        hs = slice(hd * ATT_HEAD_DIM, (hd + 1) * ATT_HEAD_DIM)
        prob = jnp.exp2(logit_scr[hd] - m_new[hd:hd + 1])
        col_sum.append(jnp.sum(prob, axis=0, keepdims=True))
        acc_scr[hd] = alpha[hd:hd + 1] * acc_scr[hd] + _dot(vt_ref[hs, :], prob.astype(MXU_DTYPE))
    l_scr[...] = alpha * l_scr[...] + jnp.concatenate(col_sum, axis=0)

    @pl.when(kj == (qi * tq) // tk + (tq // tk - 1))
    def _finish():
        for hd in range(ATT_HEADS):
            hs = slice(hd * ATT_HEAD_DIM, (hd + 1) * ATT_HEAD_DIM)
            o_ref[:, hs] = (acc_scr[hd] / l_scr[hd:hd + 1, :]).T.astype(o_ref.dtype)


def _dsa(qit, lohi, ki, qt, k, vt):
    seq = k.shape[0]
    tq, tk = min(DSA_Q_TILE, seq), min(DSA_K_TILE, seq)
    assert (tq, tk) == (DSA_Q_TILE, DSA_K_TILE) and tq == tk and seq % tq == 0
    nq = seq // tq
    pairs = [(i, j) for i in range(nq) for j in range(i + 1)]
    qi_idx = jnp.asarray([pq for pq, _ in pairs], jnp.int32)
    kj_idx = jnp.asarray([pk for _, pk in pairs], jnp.int32)
    top_k = min(TOPK_MAX, seq // 4)
    idx_bits = max(1, (seq - 1).bit_length())
    sample_rows = -(-(seq // SAMPLE_STRIDE) // SAMPLE_ROWS) * SAMPLE_ROWS
    grid_spec = pltpu.PrefetchScalarGridSpec(
        num_scalar_prefetch=2,
        grid=(len(pairs),),
        in_specs=[
            pl.BlockSpec((IDX_INNER, tq), lambda p, qi, kj: (0, qi[p])),
            pl.BlockSpec((16, tq), lambda p, qi, kj: (0, qi[p])),
            pl.BlockSpec((seq, IDX_HEAD_DIM), lambda p, qi, kj: (0, 0), pipeline_mode=pl.Buffered(1)),
            pl.BlockSpec((ATT_INNER, tq), lambda p, qi, kj: (0, qi[p])),
            pl.BlockSpec((tk, ATT_INNER), lambda p, qi, kj: (kj[p], 0)),
            pl.BlockSpec((ATT_INNER, tk), lambda p, qi, kj: (0, kj[p])),
        ],
        out_specs=pl.BlockSpec((tq, ATT_INNER), lambda p, qi, kj: (qi[p], 0)),
        scratch_shapes=[pltpu.VMEM((seq, tq), jnp.int32),
                        pltpu.VMEM((sample_rows, tq), jnp.int32),
                        pltpu.VMEM((8, tq), jnp.int32),
                        pltpu.VMEM((ATT_HEADS, tq), jnp.float32),
                        pltpu.VMEM((ATT_HEADS, tq), jnp.float32),
                        pltpu.VMEM((ATT_HEADS, ATT_HEAD_DIM, tq), jnp.float32),
                        pltpu.VMEM((ATT_HEADS, tk, tq), jnp.float32)],
    )
    return pl.pallas_call(
        functools.partial(_dsa_body, top_k=top_k, idx_bits=idx_bits),
        out_shape=jax.ShapeDtypeStruct((seq, ATT_INNER), MXU_DTYPE),
        grid_spec=grid_spec,
        compiler_params=_params(("arbitrary",)),
        name="dsa",
    )(qi_idx, kj_idx, qit, lohi, ki, qt, k, vt)


def _even_out_body(h_ref, y_ref, o_ref, wy_ref, wo_ref, out_ref):
    out_ref[...] = h_ref[...] + _dot(y_ref[...], wy_ref[...]) + _dot(o_ref[...], wo_ref[...])


def _even_out(h, y, o, wy, wo):
    seq = h.shape[0]
    tm = min(ROW_TILE, seq)
    row = pl.BlockSpec((tm, D_MODEL), lambda i: (i, 0))
    return pl.pallas_call(
        _even_out_body,
        out_shape=jax.ShapeDtypeStruct((seq, D_MODEL), jnp.float32),
        grid=(seq // tm,),
        in_specs=[row, row, row, _const_spec((SSD_INNER, D_MODEL)), _const_spec((ATT_INNER, D_MODEL))],
        out_specs=row,
        compiler_params=_params(("parallel",)),
        name="even_out",
    )(h, y, o, wy, wo)


def _odd_body(h_ref, g_ref, win_ref, lng_ref, lnb_ref, ws_ref, bs_ref, wout_ref, o_ref, gated_scr):
    h = h_ref[...]
    tm = h.shape[0]
    xn = _rms(h, g_ref[...]).astype(MXU_DTYPE)
    u = jax.nn.gelu(_dot(xn, win_ref[:, :SG_INNER]))
    v = jax.nn.gelu(_dot(xn, win_ref[:, SG_INNER:]))
    mu = jnp.mean(v, axis=-1, keepdims=True)
    vc = v - mu
    var = jnp.mean(vc * vc, axis=-1, keepdims=True)
    v = (vc * lax.rsqrt(var + EPS) * lng_ref[...] + lnb_ref[...]).astype(MXU_DTYPE)
    ri = lax.broadcasted_iota(jnp.int32, (SG_CHUNK, SG_CHUNK), 0)
    ci = lax.broadcasted_iota(jnp.int32, (SG_CHUNK, SG_CHUNK), 1)
    gw = SG_INNER // SG_GROUPS
    for g in range(SG_GROUPS):
        w = jnp.where(ri >= ci, ws_ref[g], 0.0).astype(MXU_DTYPE)
        gs = slice(g * gw, (g + 1) * gw)
        for c in range(tm // SG_CHUNK):
            rs = slice(c * SG_CHUNK, (c + 1) * SG_CHUNK)
            mixed = _dot(w, v[rs, gs]) + bs_ref[:, gs]
            gated_scr[rs, gs] = (u[rs, gs] * mixed).astype(gated_scr.dtype)
    o_ref[...] = h + _dot(gated_scr[...], wout_ref[...])


def _odd(h, g, win, ln_g, ln_b, w_s, bs_full, wout):
    seq = h.shape[0]
    tm = min(ROW_TILE, seq)
    row = pl.BlockSpec((tm, D_MODEL), lambda i: (i, 0))
    return pl.pallas_call(
        _odd_body,
        out_shape=jax.ShapeDtypeStruct((seq, D_MODEL), jnp.float32),
        grid=(seq // tm,),
        in_specs=[row, _const_spec((1, D_MODEL)), _const_spec((D_MODEL, 2 * SG_INNER)),
                  _const_spec((1, SG_INNER)), _const_spec((1, SG_INNER)),
                  _const_spec((SG_GROUPS, SG_CHUNK, SG_CHUNK)), _const_spec((SG_CHUNK, SG_INNER)),
                  _const_spec((SG_INNER, D_MODEL))],
        out_specs=row,
        scratch_shapes=[pltpu.VMEM((tm, SG_INNER), MXU_DTYPE)],
        compiler_params=_params(("parallel",)),
        name="odd",
    )(h, g, win, ln_g, ln_b, w_s, bs_full, wout)


def _rope_inputs(seq):
    def tables(rot_dim):
        inv = 1.0 / (ROPE_THETA ** (jnp.arange(0, rot_dim, 2, dtype=jnp.float32) / rot_dim))
        ang = jnp.arange(seq, dtype=jnp.float32)[:, None] * inv[None, :]
        return jnp.cos(ang), jnp.sin(ang)

    def lane_tables(cos, sin):
        half = cos.shape[1]
        pad = LANES - 2 * half
        c = jnp.concatenate([cos, cos, jnp.ones((seq, pad), jnp.float32)], axis=1)
        s_lo = jnp.concatenate([-sin, jnp.zeros((seq, LANES - half), jnp.float32)], axis=1)
        s_hi = jnp.concatenate([jnp.zeros((seq, half), jnp.float32), sin,
                                jnp.zeros((seq, pad), jnp.float32)], axis=1)
        return c, s_lo, s_hi

    cos_a, sin_a = tables(ATT_HEAD_DIM // ROPE_FRACTION)
    cos_i, sin_i = tables(IDX_HEAD_DIM // ROPE_FRACTION)
    return (*lane_tables(cos_a, sin_a), *lane_tables(cos_i, sin_i), cos_a.T, sin_a.T, cos_i.T, sin_i.T)


def _even_weights(w_in):
    offs = np.cumsum((SSD_INNER, SSD_CONV_DIM, SSD_HEADS, ATT_INNER, ATT_INNER, ATT_INNER,
                      IDX_INNER, IDX_HEAD_DIM, IDX_HEADS))[:-1].tolist()
    z, xbc, dt, q, k, v, qi, ki, wi = jnp.split(w_in, offs, axis=-1)
    pad = jnp.zeros((D_MODEL, LANES - IDX_HEAD_DIM - SSD_HEADS), w_in.dtype)
    wn = jnp.concatenate([z, xbc, k, ki, dt, pad], axis=1).astype(MXU_DTYPE)
    wi_pad = jnp.zeros((D_MODEL, _T_DT - _T_WI - IDX_HEADS), w_in.dtype)
    wt = jnp.concatenate([q, v, qi, wi, wi_pad, dt], axis=1).T.astype(MXU_DTYPE)
    assert wn.shape == (D_MODEL, _N_END) and wt.shape == (_T_END, D_MODEL)
    return wn, wt


def _even_mixer(h, g, w_in, conv_w, conv_b, dt_bias, a_log, d_skip, gate_norm, w_out, rope):
    wn, wt = _even_weights(w_in)
    z, xbc, small, k, ki, qt, vt, qit, lohi, dtt = _even_in(h, g, wn, wt, rope)
    y = _ssd(xbc, small, dtt, z, conv_w, conv_b, dt_bias, a_log, d_skip, gate_norm)
    o = _dsa(qit, lohi, ki, qt, k, vt)
    w_out = w_out.astype(MXU_DTYPE)
    return _even_out(h, y, o, w_out[:SSD_INNER], w_out[SSD_INNER:])


def _forward(x, norm_g, final_g, ffn_w_gu, ffn_w_down, ev_w_in, ev_conv_w, ev_conv_b, ev_dt_bias,
             ev_a_log, ev_d, ev_gate_norm, ev_w_out, od_w_in, od_ln_g, od_ln_b, od_w_s, od_b_s, od_w_out):
    bsz, seq, _ = x.shape
    depth = norm_g.shape[0]
    rope = _rope_inputs(seq)
    wg = ffn_w_gu[..., :FFN_HIDDEN].astype(MXU_DTYPE)
    wu = ffn_w_gu[..., FFN_HIDDEN:].astype(MXU_DTYPE)
    wd = ffn_w_down.astype(MXU_DTYPE)
    fg = final_g[None, :]
    outs = []
    for b in range(bsz):
        h = x[b]
        for layer in range(depth):
            j = layer // 2
            h = _ffn(h, norm_g[layer, 0][None, :], wg[layer, 0], wu[layer, 0], wd[layer, 0], fg, False)
            g1 = norm_g[layer, 1][None, :]
            if layer % 2 == 0:
                h = _even_mixer(h, g1, ev_w_in[j], ev_conv_w[j], ev_conv_b[j], ev_dt_bias[j], ev_a_log[j],
                                ev_d[j], ev_gate_norm[j], ev_w_out[j], rope)
            else:
                bs_full = jnp.repeat(od_b_s[j].T, SG_INNER // SG_GROUPS, axis=1)
                h = _odd(h, g1, od_w_in[j].astype(MXU_DTYPE), od_ln_g[j][None, :], od_ln_b[j][None, :],
                         od_w_s[j], bs_full, od_w_out[j].astype(MXU_DTYPE))
            h = _ffn(h, norm_g[layer, 2][None, :], wg[layer, 1], wu[layer, 1], wd[layer, 1], fg,
                     layer == depth - 1)
        outs.append(h)
    return jnp.stack(outs, axis=0)


def kernel(x, norm_g, final_g, ffn_w_gu, ffn_w_down, ev_w_in, ev_conv_w, ev_conv_b, ev_dt_bias, ev_a_log,
           ev_d, ev_gate_norm, ev_w_out, od_w_in, od_ln_g, od_ln_b, od_w_s, od_b_s, od_w_out):
    return _forward(x, norm_g, final_g, ffn_w_gu, ffn_w_down, ev_w_in, ev_conv_w, ev_conv_b, ev_dt_bias,
                    ev_a_log, ev_d, ev_gate_norm, ev_w_out, od_w_in, od_ln_g, od_ln_b, od_w_s, od_b_s,
                    od_w_out)
```

```python
import functools
import math

import numpy as np
import jax
import jax.numpy as jnp
from jax import lax
from jax.experimental import pallas as pl
from jax.experimental.pallas import tpu as pltpu

D_MODEL = 1024
SSD_HEADS = 16
SSD_HEAD_DIM = 64
SSD_INNER = SSD_HEADS * SSD_HEAD_DIM
SSD_GROUPS = 2
SSD_STATE = 128
SSD_CONV = 4
SSD_CHUNK = 256
SSD_CONV_DIM = SSD_INNER + 2 * SSD_GROUPS * SSD_STATE
ATT_HEADS = 8
ATT_HEAD_DIM = 128
ATT_INNER = ATT_HEADS * ATT_HEAD_DIM
IDX_HEADS = 8
IDX_HEAD_DIM = 64
IDX_INNER = IDX_HEADS * IDX_HEAD_DIM
TOPK_MAX = 256
SG_CHUNK = 128
SG_GROUPS = 8
SG_INNER = 2 * D_MODEL
FFN_HIDDEN = 2816
ROPE_THETA = 500000.0
ROPE_FRACTION = 4
EPS = 1e-6

ATT_ROT_HALF = ATT_HEAD_DIM // ROPE_FRACTION // 2
IDX_ROT_HALF = IDX_HEAD_DIM // ROPE_FRACTION // 2

MXU_DTYPE = jnp.bfloat16
LANES = 128
VMEM_LIMIT_BYTES = 56 * 1024 * 1024

ROW_TILE = 512
FFN_HIDDEN_CHUNK = FFN_HIDDEN // 2
DSA_Q_TILE = 256
DSA_K_TILE = 512
DSA_CHUNK = 256
SAMPLE_STRIDE = 16
SAMPLE_ROWS = 128
SEED_BITS = 20
INT_MIN = -(2 ** 31)
NEG_BIG = -0.7 * float(np.finfo(np.float32).max)

_N_Z = 0
_N_XBC = _N_Z + SSD_INNER
_N_K = _N_XBC + SSD_CONV_DIM
_N_SMALL = _N_K + ATT_INNER
_N_END = _N_SMALL + LANES
_SM_KI = 0
_SM_DT = IDX_HEAD_DIM
_T_Q = 0
_T_V = _T_Q + ATT_INNER
_T_QI = _T_V + ATT_INNER
_T_WI = _T_QI + IDX_INNER
_T_DT = _T_WI + 16
_T_END = _T_DT + SSD_HEADS


def _dot(a, b):
    return jnp.dot(a, b, preferred_element_type=jnp.float32)


def _rms(x, g):
    return x * lax.rsqrt(jnp.mean(x * x, axis=-1, keepdims=True) + EPS) * g


def _softplus(x):
    return jnp.maximum(x, 0.0) + jnp.log1p(jnp.exp(-jnp.abs(x)))


def _split3(x):
    hi = x.astype(MXU_DTYPE)
    r = x - hi.astype(jnp.float32)
    mid = r.astype(MXU_DTYPE)
    lo = (r - mid.astype(jnp.float32)).astype(MXU_DTYPE)
    return hi, mid, lo


def _const_spec(shape):
    zeros = (0,) * len(shape)
    return pl.BlockSpec(shape, lambda *_: zeros, pipeline_mode=pl.Buffered(1))


def _params(sem):
    return pltpu.CompilerParams(dimension_semantics=sem, vmem_limit_bytes=VMEM_LIMIT_BYTES)


def _ffn_body(h_ref, g_ref, wg_ref, wu_ref, wd_ref, fg_ref, o_ref, *, final):
    h = h_ref[...]
    xn = _rms(h, g_ref[...]).astype(MXU_DTYPE)
    acc = None
    for c in range(FFN_HIDDEN // FFN_HIDDEN_CHUNK):
        sl = slice(c * FFN_HIDDEN_CHUNK, (c + 1) * FFN_HIDDEN_CHUNK)
        gate = _dot(xn, wg_ref[:, sl])
        up = _dot(xn, wu_ref[:, sl])
        act = (gate * jax.nn.sigmoid(gate) * up).astype(MXU_DTYPE)
        part = _dot(act, wd_ref[sl, :])
        acc = part if acc is None else acc + part
    out = h + 0.5 * acc
    if final:
        out = _rms(out, fg_ref[...])
    o_ref[...] = out


def _ffn(h, g, wg, wu, wd, final_g, final):
    seq = h.shape[0]
    tm = min(ROW_TILE, seq)
    row = pl.BlockSpec((tm, D_MODEL), lambda i: (i, 0))
    return pl.pallas_call(
        functools.partial(_ffn_body, final=final),
        out_shape=jax.ShapeDtypeStruct((seq, D_MODEL), jnp.float32),
        grid=(seq // tm,),
        in_specs=[row, _const_spec((1, D_MODEL)), _const_spec((D_MODEL, FFN_HIDDEN)),
                  _const_spec((D_MODEL, FFN_HIDDEN)), _const_spec((FFN_HIDDEN, D_MODEL)),
                  _const_spec((1, D_MODEL))],
        out_specs=row,
        compiler_params=_params(("parallel",)),
        name="ffn",
    )(h, g, wg, wu, wd, final_g)


def _rope_lanes(x, c, s_lo, s_hi, half):
    return x * c + pltpu.roll(x, LANES - half, 1) * s_lo + pltpu.roll(x, half, 1) * s_hi


def _rope_rows(x, c, s, half):
    x1, x2 = x[:half], x[half:2 * half]
    return jnp.concatenate([x1 * c - x2 * s, x1 * s + x2 * c, x[2 * half:]], axis=0)


def _even_in_body(h_ref, g_ref, wn_ref, wt_ref, ca_ref, sal_ref, sah_ref, ci_ref, sil_ref, sih_ref,
                  cat_ref, sat_ref, cit_ref, sit_ref,
                  z_ref, xbc_ref, small_ref, k_ref, ki_ref, qt_ref, vt_ref, qit_ref, lohi_ref, dtt_ref):
    xn = _rms(h_ref[...], g_ref[...])
    xnb = xn.astype(MXU_DTYPE)
    xnt = xn.T.astype(MXU_DTYPE)

    z_ref[...] = _dot(xnb, wn_ref[:, _N_Z:_N_XBC])
    xbc_ref[...] = _dot(xnb, wn_ref[:, _N_XBC:_N_K])
    kk = _dot(xnb, wn_ref[:, _N_K:_N_SMALL])
    ca, sal, sah = ca_ref[...], sal_ref[...], sah_ref[...]
    for hd in range(ATT_HEADS):
        sl = slice(hd * ATT_HEAD_DIM, (hd + 1) * ATT_HEAD_DIM)
        k_ref[:, sl] = _rope_lanes(kk[:, sl], ca, sal, sah, ATT_ROT_HALF).astype(k_ref.dtype)
    sm = _dot(xnb, wn_ref[:, _N_SMALL:_N_END])
    small_ref[...] = sm
    smr = _rope_lanes(sm, ci_ref[...], sil_ref[...], sih_ref[...], IDX_ROT_HALF)
    ki_ref[...] = smr[:, _SM_KI:_SM_KI + IDX_HEAD_DIM].astype(ki_ref.dtype)

    att_scale = ATT_HEAD_DIM ** -0.5 * math.log2(math.e)
    qt = _dot(wt_ref[_T_Q:_T_V, :], xnt)
    cat, sat = cat_ref[...], sat_ref[...]
    def put(ref, rows, val):
        tb = ref.shape[-1]
        for b in range(ref.shape[0]):
            ref[b, rows, :] = val[:, b * tb:(b + 1) * tb].astype(ref.dtype)

    for hd in range(ATT_HEADS):
        sl = slice(hd * ATT_HEAD_DIM, (hd + 1) * ATT_HEAD_DIM)
        put(qt_ref, sl, _rope_rows(qt[sl], cat, sat, ATT_ROT_HALF) * att_scale)
    put(vt_ref, slice(None), _dot(wt_ref[_T_V:_T_QI, :], xnt))
    tail = _dot(wt_ref[_T_WI:_T_END, :], xnt)
    wit = tail[0:IDX_HEADS] * (IDX_HEADS ** -0.5 * IDX_HEAD_DIM ** -0.5)
    dtt_ref[...] = tail[_T_DT - _T_WI:_T_END - _T_WI]
    pos = wit > 0.0
    put(lohi_ref, slice(0, 8), jnp.where(pos, 0.0, -jnp.inf))
    put(lohi_ref, slice(8, 16), jnp.where(pos, jnp.inf, 0.0))
    qit = _dot(wt_ref[_T_QI:_T_WI, :], xnt)
    cit, sit = cit_ref[...], sit_ref[...]
    for hd in range(IDX_HEADS):
        sl = slice(hd * IDX_HEAD_DIM, (hd + 1) * IDX_HEAD_DIM)
        put(qit_ref, sl, _rope_rows(qit[sl], cit, sit, IDX_ROT_HALF) * wit[hd:hd + 1])


def _even_in(h, g, wn, wt, rope):
    seq = h.shape[0]
    tm = min(ROW_TILE, seq)
    f32 = jnp.float32

    def rows(width):
        return pl.BlockSpec((tm, width), lambda i: (i, 0))

    def cols(height):
        return pl.BlockSpec((height, tm), lambda i: (0, i))

    def blocks(height, tb):
        return pl.BlockSpec((tm // tb, height, tb), lambda i: (i, 0, 0))

    assert tm % DSA_K_TILE == 0 and tm % DSA_Q_TILE == 0
    nq, nk = seq // DSA_Q_TILE, seq // DSA_K_TILE
    out_shape = (
        jax.ShapeDtypeStruct((seq, SSD_INNER), f32),
        jax.ShapeDtypeStruct((seq, SSD_CONV_DIM), f32),
        jax.ShapeDtypeStruct((seq, LANES), f32),
        jax.ShapeDtypeStruct((seq, ATT_INNER), MXU_DTYPE),
        jax.ShapeDtypeStruct((seq, IDX_HEAD_DIM), MXU_DTYPE),
        jax.ShapeDtypeStruct((nq, ATT_INNER, DSA_Q_TILE), MXU_DTYPE),
        jax.ShapeDtypeStruct((nk, ATT_INNER, DSA_K_TILE), MXU_DTYPE),
        jax.ShapeDtypeStruct((nq, IDX_INNER, DSA_Q_TILE), MXU_DTYPE),
        jax.ShapeDtypeStruct((nq, 16, DSA_Q_TILE), f32),
        jax.ShapeDtypeStruct((SSD_HEADS, seq), f32),
    )
    out_specs = (rows(SSD_INNER), rows(SSD_CONV_DIM), rows(LANES), rows(ATT_INNER), rows(IDX_HEAD_DIM),
                 blocks(ATT_INNER, DSA_Q_TILE), blocks(ATT_INNER, DSA_K_TILE), blocks(IDX_INNER, DSA_Q_TILE),
                 blocks(16, DSA_Q_TILE), cols(SSD_HEADS))
    in_specs = [rows(D_MODEL), _const_spec((1, D_MODEL)), _const_spec(wn.shape), _const_spec(wt.shape)]
    in_specs += [rows(LANES)] * 6
    in_specs += [cols(ATT_ROT_HALF)] * 2 + [cols(IDX_ROT_HALF)] * 2
    return pl.pallas_call(
        _even_in_body,
        out_shape=out_shape,
        grid=(seq // tm,),
        in_specs=in_specs,
        out_specs=out_specs,
        compiler_params=_params(("parallel",)),
        name="even_in",
    )(h, g, wn, wt, *rope)


def _ssd_body(xbc_ref, small_ref, dtt_ref, z_ref, cw_ref, cb_ref, dtb_ref, dtbt_ref, a_ref, at_ref,
              drep_ref, gn_ref, o_ref, tail_scr, st_scr, y_scr, xdec_scr, decay_scr):
    q = SSD_CHUNK
    f32 = jnp.float32

    @pl.when(pl.program_id(0) == 0)
    def _():
        tail_scr[...] = jnp.zeros_like(tail_scr)
        st_scr[...] = jnp.zeros_like(st_scr)

    x = xbc_ref[...]
    tail = tail_scr[...]
    row8 = lax.broadcasted_iota(jnp.int32, (8, SSD_CONV_DIM), 0)
    conv = x * cw_ref[SSD_CONV - 1:SSD_CONV, :] + cb_ref[...]
    for shift in range(1, SSD_CONV):
        rolled = pltpu.roll(x, shift, 0)
        head = jnp.where(row8 < shift, pltpu.roll(tail, shift, 0), rolled[0:8])
        shifted = jnp.concatenate([head, rolled[8:]], axis=0)
        conv = conv + shifted * cw_ref[SSD_CONV - 1 - shift:SSD_CONV - shift, :]
    tail_scr[...] = x[q - 8:q]
    xbc = conv * jax.nn.sigmoid(conv)
    xs = xbc[:, :SSD_INNER]
    bm = xbc[:, SSD_INNER:SSD_INNER + SSD_GROUPS * SSD_STATE]
    cm = xbc[:, SSD_INNER + SSD_GROUPS * SSD_STATE:]

    dt_col = _softplus(small_ref[:, _SM_DT:_SM_DT + SSD_HEADS] + dtb_ref[...])
    dt_row = _softplus(dtt_ref[...] + dtbt_ref[...])
    adt_col = dt_col * (-jnp.exp(a_ref[...]))
    adt_row = dt_row * (-jnp.exp(at_ref[...]))
    ri = lax.broadcasted_iota(jnp.int32, (q, q), 0)
    ci = lax.broadcasted_iota(jnp.int32, (q, q), 1)
    causal = ri >= ci
    tril = jnp.where(causal, 1.0, 0.0).astype(MXU_DTYPE)
    triu = jnp.where(ri <= ci, 1.0, 0.0).astype(MXU_DTYPE)
    cs_col = sum(_dot(tril, p) for p in _split3(adt_col))
    cs_row = sum(_dot(p, triu) for p in _split3(adt_row))

    cmb = cm.astype(MXU_DTYPE)
    bmb = bm.astype(MXU_DTYPE)
    for g in range(SSD_GROUPS):
        gs = slice(g * SSD_STATE, (g + 1) * SSD_STATE)
        cb = lax.dot_general(cmb[:, gs], bmb[:, gs], (((1,), (1,)), ((), ())),
                             preferred_element_type=f32)
        kpg = SSD_HEADS // SSD_GROUPS
        for hd in range(g * kpg, (g + 1) * kpg):
            hs = slice(hd * SSD_HEAD_DIM, (hd + 1) * SSD_HEAD_DIM)
            col = cs_col[:, hd:hd + 1]
            row = cs_row[hd:hd + 1, :]
            last = cs_row[hd:hd + 1, q - 1:q]
            decay = jnp.exp(jnp.where(causal, col - row, -jnp.inf))
            xdt = xs[:, hs] * dt_col[:, hd:hd + 1]
            y = _dot((cb * decay).astype(MXU_DTYPE), xdt.astype(MXU_DTYPE))
            y_off = _dot(cmb[:, gs], st_scr[:, hs].astype(MXU_DTYPE))
            y_scr[:, hs] = y + y_off * jnp.exp(col)
            xdec_scr[:, hs] = (xdt * jnp.exp(last - col)).astype(xdec_scr.dtype)
            decay_scr[:, hs] = jnp.broadcast_to(jnp.exp(last), (1, SSD_HEAD_DIM))
        ws = slice(g * kpg * SSD_HEAD_DIM, (g + 1) * kpg * SSD_HEAD_DIM)
        bmt = bm[:, gs].T.astype(MXU_DTYPE)
        st_scr[:, ws] = st_scr[:, ws] * decay_scr[:, ws] + _dot(bmt, xdec_scr[:, ws])

    z = z_ref[...]
    y = (y_scr[...] + drep_ref[...] * xs) * (z * jax.nn.sigmoid(z))
    gw = SSD_INNER // SSD_GROUPS
    for g in range(SSD_GROUPS):
        sl = slice(g * gw, (g + 1) * gw)
        seg = y[:, sl]
        seg = seg * lax.rsqrt(jnp.mean(seg * seg, axis=-1, keepdims=True) + EPS)
        o_ref[:, sl] = (seg * gn_ref[:, sl]).astype(o_ref.dtype)


def _ssd(xbc, small, dtt, z, conv_w, conv_b, dt_bias, a_log, d_skip, gate_norm):
    seq = xbc.shape[0]
    q = SSD_CHUNK
    f32 = jnp.float32

    def rows(width):
        return pl.BlockSpec((q, width), lambda i: (i, 0))

    drep = jnp.repeat(d_skip, SSD_HEAD_DIM)[None, :]
    return pl.pallas_call(
        _ssd_body,
        out_shape=jax.ShapeDtypeStruct((seq, SSD_INNER), MXU_DTYPE),
        grid=(seq // q,),
        in_specs=[rows(SSD_CONV_DIM), rows(LANES), pl.BlockSpec((SSD_HEADS, q), lambda i: (0, i)),
                  rows(SSD_INNER), _const_spec((SSD_CONV, SSD_CONV_DIM)), _const_spec((1, SSD_CONV_DIM)),
                  _const_spec((1, SSD_HEADS)), _const_spec((SSD_HEADS, 1)),
                  _const_spec((1, SSD_HEADS)), _const_spec((SSD_HEADS, 1)),
                  _const_spec((1, SSD_INNER)), _const_spec((1, SSD_INNER))],
        out_specs=rows(SSD_INNER),
        scratch_shapes=[pltpu.VMEM((8, SSD_CONV_DIM), f32),
                        pltpu.VMEM((SSD_STATE, SSD_INNER), f32),
                        pltpu.VMEM((q, SSD_INNER), f32),
                        pltpu.VMEM((q, SSD_INNER), MXU_DTYPE),
                        pltpu.VMEM((1, SSD_INNER), f32)],
        compiler_params=_params(("arbitrary",)),
        name="ssd",
    )(xbc, small, dtt, z, conv_w, conv_b[None, :], dt_bias[None, :], dt_bias[:, None],
      a_log[None, :], a_log[:, None], drep, gate_norm[None, :])


def _dsa_body(qi_ref, kj_ref, qit_ref, lohi_ref, ki_ref, qt_ref, k_ref, vt_ref, o_ref,
              keys_scr, sample_scr, thr_scr, m_scr, l_scr, acc_scr, logit_scr, *, top_k, idx_bits):
    tq, tk, ck = DSA_Q_TILE, DSA_K_TILE, DSA_CHUNK
    i32 = jnp.int32
    p = pl.program_id(0)
    qi = qi_ref[p]
    kj = kj_ref[p]

    @pl.when(kj == 0)
    def _select():
        n_chunks = (qi + 1) * (tq // ck)
        lo = lohi_ref[0:8, :]
        hi = lohi_ref[8:16, :]
        t_pos = qi * tq + lax.broadcasted_iota(i32, (ck, tq), 1)
        s_off = lax.broadcasted_iota(i32, (ck, tq), 0)

        @pl.when(((qi + 1) * tq) % tk != 0)
        def _():
            keys_scr[pl.ds(pl.multiple_of((qi + 1) * tq, tq), tk - tq), :] = jnp.full((tk - tq, tq), INT_MIN, i32)

        def score_chunk(c, carry):
            r0 = pl.multiple_of(c * ck, ck)
            kic = ki_ref[pl.ds(r0, ck), :]
            sc = jnp.zeros((ck, tq), jnp.float32)
            for hd in range(IDX_HEADS):
                y = _dot(kic, qit_ref[hd * IDX_HEAD_DIM:(hd + 1) * IDX_HEAD_DIM, :])
                sc = sc + jnp.minimum(jnp.maximum(y, lo[hd:hd + 1]), hi[hd:hd + 1])
            bits = pltpu.bitcast(sc, i32)
            key = bits ^ ((bits >> 31) & 0x7FFFFFFF)
            key = jnp.where(sc == 0.0, 0, key)
            key = jnp.where(r0 + s_off <= t_pos, key, INT_MIN)
            keys_scr[pl.ds(r0, ck), :] = key
            s0 = pl.multiple_of(c * (ck // SAMPLE_STRIDE), ck // SAMPLE_STRIDE)
            sample_scr[pl.ds(s0, ck // SAMPLE_STRIDE), :] = jnp.concatenate(
                [key[g * 8 * SAMPLE_STRIDE:g * 8 * SAMPLE_STRIDE + 8] for g in range(ck // (8 * SAMPLE_STRIDE))],
                axis=0)
            return carry

        sample_scr[...] = jnp.full_like(sample_scr, INT_MIN)
        lax.fori_loop(0, n_chunks, score_chunk, 0)

        def count_rows(ref, n_iter, rows, preds):
            lanes_of_partials = 4 * 8

            def body(c, accs):
                r0 = pl.multiple_of(c * rows, rows)
                kk = ref[pl.ds(r0, rows), :]
                out = []
                for pred, acc in zip(preds, accs):
                    hit = jnp.where(pred(kk, r0), 1.0, 0.0)
                    out.append(acc + hit.reshape(rows // lanes_of_partials, lanes_of_partials, tq).sum(axis=0))
                return tuple(out)

            zero = jnp.zeros((lanes_of_partials, tq), jnp.float32)
            accs = lax.fori_loop(0, n_iter, body, tuple(zero for _ in preds))
            return [acc.sum(axis=0, keepdims=True) for acc in accs]

        def count(pred):
            return count_rows(keys_scr, n_chunks, ck, [pred])[0]

        k_f = jnp.float32(top_k)
        n_t = (qi * tq + 1 + lax.broadcasted_iota(i32, (1, tq), 1)).astype(jnp.float32)
        n_samp_iter = (n_chunks * (ck // SAMPLE_STRIDE) + SAMPLE_ROWS - 1) // SAMPLE_ROWS
        n_s = count_rows(sample_scr, n_samp_iter, SAMPLE_ROWS, [lambda kk, r0: kk > INT_MIN])[0]
        mu = k_f * n_s / n_t
        spread = 4.0 * jnp.sqrt(mu)
        rank_hi = jnp.maximum(1.0, jnp.floor(mu - spread))
        rank_lo = jnp.ceil(mu + spread) + 1.0

        def seed_step(b, carry):
            t_lo, t_hi = carry
            step = lax.shift_left(i32(1), i32(31) - b)
            c_lo, c_hi = t_lo + step, t_hi + step
            n_lo, n_hi = count_rows(sample_scr, n_samp_iter, SAMPLE_ROWS,
                                    [lambda kk, r0: kk >= c_lo, lambda kk, r0: kk >= c_hi])
            return jnp.where(n_lo >= rank_lo, c_lo, t_lo), jnp.where(n_hi >= rank_hi, c_hi, t_hi)

        start = jnp.full((1, tq), INT_MIN, i32)
        seed_lo, seed_hi = lax.fori_loop(0, SEED_BITS, seed_step, (start, start))

        def search_cond(carry):
            return jnp.logical_and(carry[0] < 3 * 32 + 2, carry[1] > 0)

        def search_step(carry):
            it, _, lo_k, hi_k, c_lo, c_hi = carry
            done = (c_lo <= k_f) | (lo_k + 1 >= hi_k)
            lo_f, hi_f = lo_k.astype(jnp.float32), hi_k.astype(jnp.float32)
            frac = (c_lo - k_f + 0.5) / jnp.maximum(c_lo - c_hi, 1.0)
            guess = jnp.clip(lo_f + frac * (hi_f - lo_f), -2.0 ** 31 + 256.0, 2.0 ** 31 - 256.0).astype(i32)
            mid = (lo_k >> 1) + (hi_k >> 1) + (lo_k & hi_k & 1)
            cand = jnp.where(it == 0, seed_lo, jnp.where(it == 1, seed_hi, jnp.where(it % 3 == 1, mid, guess)))
            cand = jnp.where(done, lo_k, jnp.maximum(lo_k + 1, jnp.minimum(hi_k - 1, cand)))
            cnt = count(lambda kk, r0: kk >= cand)
            up = jnp.logical_and(~done, cnt >= k_f)
            down = jnp.logical_and(~done, cnt < k_f)
            lo_k, c_lo = jnp.where(up, cand, lo_k), jnp.where(up, cnt, c_lo)
            hi_k, c_hi = jnp.where(down, cand, hi_k), jnp.where(down, cnt, c_hi)
            still = jnp.where((c_lo <= k_f) | (lo_k + 1 >= hi_k), 0.0, 1.0)
            return it + 1, jnp.max(still).astype(i32), lo_k, hi_k, c_lo, c_hi

        init = (i32(0), i32(1), jnp.full((1, tq), INT_MIN + 1, i32), jnp.full((1, tq), jnp.iinfo(i32).max, i32),
                n_t, jnp.zeros((1, tq), jnp.float32))
        _, _, thr, _, n_ge, _ = lax.while_loop(search_cond, search_step, init)
        thr_scr[0:1, :] = thr

        @pl.when(jnp.max(n_ge) > top_k)
        def _ties():
            need = top_k - count(lambda kk, r0: kk > thr)

            def idx_step(b, cut):
                cand = cut + lax.shift_left(i32(1), i32(idx_bits - 1) - b)
                cnt = count(lambda kk, r0: jnp.where(kk == thr, r0 + s_off, cand) < cand)
                return jnp.where(cnt < need, cand, cut)

            cut = lax.fori_loop(0, idx_bits, idx_step, jnp.zeros((1, tq), i32))
            cut = jnp.where(n_ge > top_k, cut, jnp.iinfo(i32).max)

            def drop(c, carry):
                r0 = pl.multiple_of(c * ck, ck)
                kk = keys_scr[pl.ds(r0, ck), :]
                excess = jnp.where(kk == thr, r0 + s_off, INT_MIN) > cut
                keys_scr[pl.ds(r0, ck), :] = jnp.where(excess, INT_MIN, kk)
                return carry

            lax.fori_loop(0, n_chunks, drop, 0)

        m_scr[...] = jnp.full_like(m_scr, NEG_BIG)
        l_scr[...] = jnp.zeros_like(l_scr)
        acc_scr[...] = jnp.zeros_like(acc_scr)

    sel = keys_scr[pl.ds(pl.multiple_of(kj * tk, tk), tk), :] >= thr_scr[0:1, :]
    bias = jnp.where(sel, 0.0, -jnp.inf)
    col_max = []
    for hd in range(ATT_HEADS):
        hs = slice(hd * ATT_HEAD_DIM, (hd + 1) * ATT_HEAD_DIM)
        logit = _dot(k_ref[:, hs], qt_ref[hs, :]) + bias
        logit_scr[hd] = logit
        col_max.append(jnp.max(logit, axis=0, keepdims=True))
    m_old = m_scr[...]
    m_new = jnp.maximum(m_old, jnp.concatenate(col_max, axis=0))
    alpha = jnp.exp2(m_old - m_new)
    m_scr[...] = m_new
    col_sum = []
    for hd in range(ATT_HEADS):
        hs = slice(hd * ATT_HEAD_DIM, (hd + 1) * ATT_HEAD_DIM)
        prob = jnp.exp2(logit_scr[hd] - m_new[hd:hd + 1])
        col_sum.append(jnp.sum(prob, axis=0, keepdims=True))
        acc_scr[hd] = alpha[hd:hd + 1] * acc_scr[hd] + _dot(vt_ref[hs, :], prob.astype(MXU_DTYPE))
    l_scr[...] = alpha * l_scr[...] + jnp.concatenate(col_sum, axis=0)

    @pl.when(kj == ((qi + 1) * tq - 1) // tk)
    def _finish():
        for hd in range(ATT_HEADS):
            hs = slice(hd * ATT_HEAD_DIM, (hd + 1) * ATT_HEAD_DIM)
            o_ref[:, hs] = (acc_scr[hd] / l_scr[hd:hd + 1, :]).T.astype(o_ref.dtype)


def _dsa(qit, lohi, ki, qt, k, vt):
    seq = k.shape[0]
    tq, tk = DSA_Q_TILE, DSA_K_TILE
    assert seq % tk == 0 and tk % tq == 0 and tq % DSA_CHUNK == 0
    assert qt.shape == (seq // tq, ATT_INNER, tq) and vt.shape == (seq // tk, ATT_INNER, tk)
    nq = seq // tq
    pairs = [(i, j) for i in range(nq) for j in range(((i + 1) * tq - 1) // tk + 1)]
    qi_idx = jnp.asarray([pq for pq, _ in pairs], jnp.int32)
    kj_idx = jnp.asarray([pk for _, pk in pairs], jnp.int32)
    top_k = min(TOPK_MAX, seq // 4)
    idx_bits = max(1, (seq - 1).bit_length())
    sample_rows = -(-(seq // SAMPLE_STRIDE) // SAMPLE_ROWS) * SAMPLE_ROWS
    grid_spec = pltpu.PrefetchScalarGridSpec(
        num_scalar_prefetch=2,
        grid=(len(pairs),),
        in_specs=[
            pl.BlockSpec((None, IDX_INNER, tq), lambda p, qi, kj: (qi[p], 0, 0)),
            pl.BlockSpec((None, 16, tq), lambda p, qi, kj: (qi[p], 0, 0)),
            pl.BlockSpec((seq, IDX_HEAD_DIM), lambda p, qi, kj: (0, 0), pipeline_mode=pl.Buffered(1)),
            pl.BlockSpec((None, ATT_INNER, tq), lambda p, qi, kj: (qi[p], 0, 0)),
            pl.BlockSpec((tk, ATT_INNER), lambda p, qi, kj: (kj[p], 0)),
            pl.BlockSpec((None, ATT_INNER, tk), lambda p, qi, kj: (kj[p], 0, 0)),
        ],
        out_specs=pl.BlockSpec((tq, ATT_INNER), lambda p, qi, kj: (qi[p], 0)),
        scratch_shapes=[pltpu.VMEM((seq, tq), jnp.int32),
                        pltpu.VMEM((sample_rows, tq), jnp.int32),
                        pltpu.VMEM((8, tq), jnp.int32),
                        pltpu.VMEM((ATT_HEADS, tq), jnp.float32),
                        pltpu.VMEM((ATT_HEADS, tq), jnp.float32),
                        pltpu.VMEM((ATT_HEADS, ATT_HEAD_DIM, tq), jnp.float32),
                        pltpu.VMEM((ATT_HEADS, tk, tq), jnp.float32)],
    )
    return pl.pallas_call(
        functools.partial(_dsa_body, top_k=top_k, idx_bits=idx_bits),
        out_shape=jax.ShapeDtypeStruct((seq, ATT_INNER), MXU_DTYPE),
        grid_spec=grid_spec,
        compiler_params=_params(("arbitrary",)),
        name="dsa",
    )(qi_idx, kj_idx, qit, lohi, ki, qt, k, vt)


def _even_out_body(h_ref, y_ref, o_ref, wy_ref, wo_ref, out_ref):
    out_ref[...] = h_ref[...] + _dot(y_ref[...], wy_ref[...]) + _dot(o_ref[...], wo_ref[...])


def _even_out(h, y, o, wy, wo):
    seq = h.shape[0]
    tm = min(ROW_TILE, seq)
    row = pl.BlockSpec((tm, D_MODEL), lambda i: (i, 0))
    return pl.pallas_call(
        _even_out_body,
        out_shape=jax.ShapeDtypeStruct((seq, D_MODEL), jnp.float32),
        grid=(seq // tm,),
        in_specs=[row, row, row, _const_spec((SSD_INNER, D_MODEL)), _const_spec((ATT_INNER, D_MODEL))],
        out_specs=row,
        compiler_params=_params(("parallel",)),
        name="even_out",
    )(h, y, o, wy, wo)


def _odd_body(h_ref, g_ref, win_ref, lng_ref, lnb_ref, ws_ref, bs_ref, wout_ref, o_ref, gated_scr):
    h = h_ref[...]
    tm = h.shape[0]
    xn = _rms(h, g_ref[...]).astype(MXU_DTYPE)
    u = jax.nn.gelu(_dot(xn, win_ref[:, :SG_INNER]))
    v = jax.nn.gelu(_dot(xn, win_ref[:, SG_INNER:]))
    mu = jnp.mean(v, axis=-1, keepdims=True)
    vc = v - mu
    var = jnp.mean(vc * vc, axis=-1, keepdims=True)
    v = (vc * lax.rsqrt(var + EPS) * lng_ref[...] + lnb_ref[...]).astype(MXU_DTYPE)
    ri = lax.broadcasted_iota(jnp.int32, (SG_CHUNK, SG_CHUNK), 0)
    ci = lax.broadcasted_iota(jnp.int32, (SG_CHUNK, SG_CHUNK), 1)
    gw = SG_INNER // SG_GROUPS
    for g in range(SG_GROUPS):
        w = jnp.where(ri >= ci, ws_ref[g], 0.0).astype(MXU_DTYPE)
        gs = slice(g * gw, (g + 1) * gw)
        for c in range(tm // SG_CHUNK):
            rs = slice(c * SG_CHUNK, (c + 1) * SG_CHUNK)
            mixed = _dot(w, v[rs, gs]) + bs_ref[:, gs]
            gated_scr[rs, gs] = (u[rs, gs] * mixed).astype(gated_scr.dtype)
    o_ref[...] = h + _dot(gated_scr[...], wout_ref[...])


def _odd(h, g, win, ln_g, ln_b, w_s, bs_full, wout):
    seq = h.shape[0]
    tm = min(ROW_TILE, seq)
    row = pl.BlockSpec((tm, D_MODEL), lambda i: (i, 0))
    return pl.pallas_call(
        _odd_body,
        out_shape=jax.ShapeDtypeStruct((seq, D_MODEL), jnp.float32),
        grid=(seq // tm,),
        in_specs=[row, _const_spec((1, D_MODEL)), _const_spec((D_MODEL, 2 * SG_INNER)),
                  _const_spec((1, SG_INNER)), _const_spec((1, SG_INNER)),
                  _const_spec((SG_GROUPS, SG_CHUNK, SG_CHUNK)), _const_spec((SG_CHUNK, SG_INNER)),
                  _const_spec((SG_INNER, D_MODEL))],
        out_specs=row,
        scratch_shapes=[pltpu.VMEM((tm, SG_INNER), MXU_DTYPE)],
        compiler_params=_params(("parallel",)),
        name="odd",
    )(h, g, win, ln_g, ln_b, w_s, bs_full, wout)


def _rope_inputs(seq):
    def tables(rot_dim):
        inv = 1.0 / (ROPE_THETA ** (jnp.arange(0, rot_dim, 2, dtype=jnp.float32) / rot_dim))
        ang = jnp.arange(seq, dtype=jnp.float32)[:, None] * inv[None, :]
        return jnp.cos(ang), jnp.sin(ang)

    def lane_tables(cos, sin):
        half = cos.shape[1]
        pad = LANES - 2 * half
        c = jnp.concatenate([cos, cos, jnp.ones((seq, pad), jnp.float32)], axis=1)
        s_lo = jnp.concatenate([-sin, jnp.zeros((seq, LANES - half), jnp.float32)], axis=1)
        s_hi = jnp.concatenate([jnp.zeros((seq, half), jnp.float32), sin,
                                jnp.zeros((seq, pad), jnp.float32)], axis=1)
        return c, s_lo, s_hi

    cos_a, sin_a = tables(ATT_HEAD_DIM // ROPE_FRACTION)
    cos_i, sin_i = tables(IDX_HEAD_DIM // ROPE_FRACTION)
    return (*lane_tables(cos_a, sin_a), *lane_tables(cos_i, sin_i), cos_a.T, sin_a.T, cos_i.T, sin_i.T)


def _even_weights(w_in):
    offs = np.cumsum((SSD_INNER, SSD_CONV_DIM, SSD_HEADS, ATT_INNER, ATT_INNER, ATT_INNER,
                      IDX_INNER, IDX_HEAD_DIM, IDX_HEADS))[:-1].tolist()
    z, xbc, dt, q, k, v, qi, ki, wi = jnp.split(w_in, offs, axis=-1)
    pad = jnp.zeros((D_MODEL, LANES - IDX_HEAD_DIM - SSD_HEADS), w_in.dtype)
    wn = jnp.concatenate([z, xbc, k, ki, dt, pad], axis=1).astype(MXU_DTYPE)
    wi_pad = jnp.zeros((D_MODEL, _T_DT - _T_WI - IDX_HEADS), w_in.dtype)
    wt = jnp.concatenate([q, v, qi, wi, wi_pad, dt], axis=1).T.astype(MXU_DTYPE)
    assert wn.shape == (D_MODEL, _N_END) and wt.shape == (_T_END, D_MODEL)
    return wn, wt


def _even_mixer(h, g, w_in, conv_w, conv_b, dt_bias, a_log, d_skip, gate_norm, w_out, rope):
    wn, wt = _even_weights(w_in)
    z, xbc, small, k, ki, qt, vt, qit, lohi, dtt = _even_in(h, g, wn, wt, rope)
    y = _ssd(xbc, small, dtt, z, conv_w, conv_b, dt_bias, a_log, d_skip, gate_norm)
    o = _dsa(qit, lohi, ki, qt, k, vt)
    w_out = w_out.astype(MXU_DTYPE)
    return _even_out(h, y, o, w_out[:SSD_INNER], w_out[SSD_INNER:])


def _forward(x, norm_g, final_g, ffn_w_gu, ffn_w_down, ev_w_in, ev_conv_w, ev_conv_b, ev_dt_bias,
             ev_a_log, ev_d, ev_gate_norm, ev_w_out, od_w_in, od_ln_g, od_ln_b, od_w_s, od_b_s, od_w_out):
    bsz, seq, _ = x.shape
    depth = norm_g.shape[0]
    rope = _rope_inputs(seq)
    wg = ffn_w_gu[..., :FFN_HIDDEN].astype(MXU_DTYPE)
    wu = ffn_w_gu[..., FFN_HIDDEN:].astype(MXU_DTYPE)
    wd = ffn_w_down.astype(MXU_DTYPE)
    fg = final_g[None, :]
    outs = []
    for b in range(bsz):
        h = x[b]
        for layer in range(depth):
            j = layer // 2
            h = _ffn(h, norm_g[layer, 0][None, :], wg[layer, 0], wu[layer, 0], wd[layer, 0], fg, False)
            g1 = norm_g[layer, 1][None, :]
            if layer % 2 == 0:
                h = _even_mixer(h, g1, ev_w_in[j], ev_conv_w[j], ev_conv_b[j], ev_dt_bias[j], ev_a_log[j],
                                ev_d[j], ev_gate_norm[j], ev_w_out[j], rope)
            else:
                bs_full = jnp.repeat(od_b_s[j].T, SG_INNER // SG_GROUPS, axis=1)
                h = _odd(h, g1, od_w_in[j].astype(MXU_DTYPE), od_ln_g[j][None, :], od_ln_b[j][None, :],
                         od_w_s[j], bs_full, od_w_out[j].astype(MXU_DTYPE))
            h = _ffn(h, norm_g[layer, 2][None, :], wg[layer, 1], wu[layer, 1], wd[layer, 1], fg,
                     layer == depth - 1)
        outs.append(h)
    return jnp.stack(outs, axis=0)


def kernel(x, norm_g, final_g, ffn_w_gu, ffn_w_down, ev_w_in, ev_conv_w, ev_conv_b, ev_dt_bias, ev_a_log,
           ev_d, ev_gate_norm, ev_w_out, od_w_in, od_ln_g, od_ln_b, od_w_s, od_b_s, od_w_out):
    return _forward(x, norm_g, final_g, ffn_w_gu, ffn_w_down, ev_w_in, ev_conv_w, ev_conv_b, ev_dt_bias,
                    ev_a_log, ev_d, ev_gate_norm, ev_w_out, od_w_in, od_ln_g, od_ln_b, od_w_s, od_b_s,
                    od_w_out)
```

```python
import functools
import math

import numpy as np
import jax
import jax.numpy as jnp
from jax import lax
from jax.experimental import pallas as pl
from jax.experimental.pallas import tpu as pltpu

D_MODEL = 1024
SSD_HEADS = 16
SSD_HEAD_DIM = 64
SSD_INNER = SSD_HEADS * SSD_HEAD_DIM
SSD_GROUPS = 2
SSD_STATE = 128
SSD_CONV = 4
SSD_CHUNK = 256
SSD_CONV_DIM = SSD_INNER + 2 * SSD_GROUPS * SSD_STATE
ATT_HEADS = 8
ATT_HEAD_DIM = 128
ATT_INNER = ATT_HEADS * ATT_HEAD_DIM
IDX_HEADS = 8
IDX_HEAD_DIM = 64
IDX_INNER = IDX_HEADS * IDX_HEAD_DIM
TOPK_MAX = 256
SG_CHUNK = 128
SG_GROUPS = 8
SG_INNER = 2 * D_MODEL
FFN_HIDDEN = 2816
ROPE_THETA = 500000.0
ROPE_FRACTION = 4
EPS = 1e-6

ATT_ROT_HALF = ATT_HEAD_DIM // ROPE_FRACTION // 2
IDX_ROT_HALF = IDX_HEAD_DIM // ROPE_FRACTION // 2

MXU_DTYPE = jnp.bfloat16
LANES = 128
VMEM_LIMIT_BYTES = 56 * 1024 * 1024

ROW_TILE = 512
FFN_HIDDEN_CHUNK = FFN_HIDDEN // 2
DSA_Q_TILE = 256
DSA_K_TILE = 512
DSA_CHUNK = 256
LEAD_MASK = -(2 ** 16)
QUIET_NAN_BITS = 0x7FC00000
FINITE_KEY = 0x7F7FFFFF
MIN_NORMAL_EXP = 0x00800000
MIN_NORMAL_CODE = MIN_NORMAL_EXP >> 16
INT_MIN = -(2 ** 31)
NEG_BIG = -0.7 * float(np.finfo(np.float32).max)

_N_Z = 0
_N_XBC = _N_Z + SSD_INNER
_N_K = _N_XBC + SSD_CONV_DIM
_N_SMALL = _N_K + ATT_INNER
_N_END = _N_SMALL + LANES
_SM_KI = 0
_SM_DT = IDX_HEAD_DIM
_T_Q = 0
_T_V = _T_Q + ATT_INNER
_T_QI = _T_V + ATT_INNER
_T_WI = _T_QI + IDX_INNER
_T_DT = _T_WI + 16
_T_END = _T_DT + SSD_HEADS


def _dot(a, b):
    return jnp.dot(a, b, preferred_element_type=jnp.float32)


def _rms(x, g):
    return x * lax.rsqrt(jnp.mean(x * x, axis=-1, keepdims=True) + EPS) * g


def _softplus(x):
    return jnp.maximum(x, 0.0) + jnp.log1p(jnp.exp(-jnp.abs(x)))


def _split3(x):
    hi = x.astype(MXU_DTYPE)
    r = x - hi.astype(jnp.float32)
    mid = r.astype(MXU_DTYPE)
    lo = (r - mid.astype(jnp.float32)).astype(MXU_DTYPE)
    return hi, mid, lo


def _const_spec(shape):
    zeros = (0,) * len(shape)
    return pl.BlockSpec(shape, lambda *_: zeros, pipeline_mode=pl.Buffered(1))


def _params(sem):
    return pltpu.CompilerParams(dimension_semantics=sem, vmem_limit_bytes=VMEM_LIMIT_BYTES)


def _ffn_body(h_ref, g_ref, wg_ref, wu_ref, wd_ref, fg_ref, o_ref, *, final):
    h = h_ref[...]
    xn = _rms(h, g_ref[...]).astype(MXU_DTYPE)
    acc = None
    for c in range(FFN_HIDDEN // FFN_HIDDEN_CHUNK):
        sl = slice(c * FFN_HIDDEN_CHUNK, (c + 1) * FFN_HIDDEN_CHUNK)
        gate = _dot(xn, wg_ref[:, sl])
        up = _dot(xn, wu_ref[:, sl])
        act = (gate * jax.nn.sigmoid(gate) * up).astype(MXU_DTYPE)
        part = _dot(act, wd_ref[sl, :])
        acc = part if acc is None else acc + part
    out = h + 0.5 * acc
    if final:
        out = _rms(out, fg_ref[...])
    o_ref[...] = out


def _ffn(h, g, wg, wu, wd, final_g, final):
    seq = h.shape[0]
    tm = min(ROW_TILE, seq)
    row = pl.BlockSpec((tm, D_MODEL), lambda i: (i, 0))
    return pl.pallas_call(
        functools.partial(_ffn_body, final=final),
        out_shape=jax.ShapeDtypeStruct((seq, D_MODEL), jnp.float32),
        grid=(seq // tm,),
        in_specs=[row, _const_spec((1, D_MODEL)), _const_spec((D_MODEL, FFN_HIDDEN)),
                  _const_spec((D_MODEL, FFN_HIDDEN)), _const_spec((FFN_HIDDEN, D_MODEL)),
                  _const_spec((1, D_MODEL))],
        out_specs=row,
        compiler_params=_params(("parallel",)),
        name="ffn",
    )(h, g, wg, wu, wd, final_g)


def _rope_lanes(x, c, s_lo, s_hi, half):
    return x * c + pltpu.roll(x, LANES - half, 1) * s_lo + pltpu.roll(x, half, 1) * s_hi


def _rope_rows(x, c, s, half):
    x1, x2 = x[:half], x[half:2 * half]
    return jnp.concatenate([x1 * c - x2 * s, x1 * s + x2 * c, x[2 * half:]], axis=0)


def _even_in_body(h_ref, g_ref, wn_ref, wt_ref, ca_ref, sal_ref, sah_ref, ci_ref, sil_ref, sih_ref,
                  cat_ref, sat_ref, cit_ref, sit_ref,
                  z_ref, xbc_ref, small_ref, k_ref, ki_ref, qt_ref, vt_ref, qit_ref, lohi_ref, dtt_ref):
    xn = _rms(h_ref[...], g_ref[...])
    xnb = xn.astype(MXU_DTYPE)
    xnt = xn.T.astype(MXU_DTYPE)

    z_ref[...] = _dot(xnb, wn_ref[:, _N_Z:_N_XBC])
    xbc_ref[...] = _dot(xnb, wn_ref[:, _N_XBC:_N_K])
    kk = _dot(xnb, wn_ref[:, _N_K:_N_SMALL])
    ca, sal, sah = ca_ref[...], sal_ref[...], sah_ref[...]
    for hd in range(ATT_HEADS):
        sl = slice(hd * ATT_HEAD_DIM, (hd + 1) * ATT_HEAD_DIM)
        k_ref[:, sl] = _rope_lanes(kk[:, sl], ca, sal, sah, ATT_ROT_HALF).astype(k_ref.dtype)
    sm = _dot(xnb, wn_ref[:, _N_SMALL:_N_END])
    small_ref[...] = sm
    smr = _rope_lanes(sm, ci_ref[...], sil_ref[...], sih_ref[...], IDX_ROT_HALF)
    ki_ref[...] = smr[:, _SM_KI:_SM_KI + IDX_HEAD_DIM].astype(ki_ref.dtype)

    att_scale = ATT_HEAD_DIM ** -0.5 * math.log2(math.e)
    qt = _dot(wt_ref[_T_Q:_T_V, :], xnt)
    cat, sat = cat_ref[...], sat_ref[...]
    def put(ref, rows, val):
        tb = ref.shape[-1]
        for b in range(ref.shape[0]):
            ref[b, rows, :] = val[:, b * tb:(b + 1) * tb].astype(ref.dtype)

    for hd in range(ATT_HEADS):
        sl = slice(hd * ATT_HEAD_DIM, (hd + 1) * ATT_HEAD_DIM)
        put(qt_ref, sl, _rope_rows(qt[sl], cat, sat, ATT_ROT_HALF) * att_scale)
    put(vt_ref, slice(None), _dot(wt_ref[_T_V:_T_QI, :], xnt))
    tail = _dot(wt_ref[_T_WI:_T_END, :], xnt)
    wit = tail[0:IDX_HEADS] * (IDX_HEADS ** -0.5 * IDX_HEAD_DIM ** -0.5)
    dtt_ref[...] = tail[_T_DT - _T_WI:_T_END - _T_WI]
    pos = wit > 0.0
    put(lohi_ref, slice(0, 8), jnp.where(pos, 0.0, -jnp.inf))
    put(lohi_ref, slice(8, 16), jnp.where(pos, jnp.inf, 0.0))
    qit = _dot(wt_ref[_T_QI:_T_WI, :], xnt)
    cit, sit = cit_ref[...], sit_ref[...]
    for hd in range(IDX_HEADS):
        sl = slice(hd * IDX_HEAD_DIM, (hd + 1) * IDX_HEAD_DIM)
        put(qit_ref, sl, _rope_rows(qit[sl], cit, sit, IDX_ROT_HALF) * wit[hd:hd + 1])


def _even_in(h, g, wn, wt, rope):
    seq = h.shape[0]
    tm = min(ROW_TILE, seq)
    f32 = jnp.float32

    def rows(width):
        return pl.BlockSpec((tm, width), lambda i: (i, 0))

    def cols(height):
        return pl.BlockSpec((height, tm), lambda i: (0, i))

    def blocks(height, tb):
        return pl.BlockSpec((tm // tb, height, tb), lambda i: (i, 0, 0))

    assert tm % DSA_K_TILE == 0 and tm % DSA_Q_TILE == 0
    nq, nk = seq // DSA_Q_TILE, seq // DSA_K_TILE
    out_shape = (
        jax.ShapeDtypeStruct((seq, SSD_INNER), f32),
        jax.ShapeDtypeStruct((seq, SSD_CONV_DIM), f32),
        jax.ShapeDtypeStruct((seq, LANES), f32),
        jax.ShapeDtypeStruct((seq, ATT_INNER), MXU_DTYPE),
        jax.ShapeDtypeStruct((seq, IDX_HEAD_DIM), MXU_DTYPE),
        jax.ShapeDtypeStruct((nq, ATT_INNER, DSA_Q_TILE), MXU_DTYPE),
        jax.ShapeDtypeStruct((nk, ATT_INNER, DSA_K_TILE), MXU_DTYPE),
        jax.ShapeDtypeStruct((nq, IDX_INNER, DSA_Q_TILE), MXU_DTYPE),
        jax.ShapeDtypeStruct((nq, 16, DSA_Q_TILE), f32),
        jax.ShapeDtypeStruct((SSD_HEADS, seq), f32),
    )
    out_specs = (rows(SSD_INNER), rows(SSD_CONV_DIM), rows(LANES), rows(ATT_INNER), rows(IDX_HEAD_DIM),
                 blocks(ATT_INNER, DSA_Q_TILE), blocks(ATT_INNER, DSA_K_TILE), blocks(IDX_INNER, DSA_Q_TILE),
                 blocks(16, DSA_Q_TILE), cols(SSD_HEADS))
    in_specs = [rows(D_MODEL), _const_spec((1, D_MODEL)), _const_spec(wn.shape), _const_spec(wt.shape)]
    in_specs += [rows(LANES)] * 6
    in_specs += [cols(ATT_ROT_HALF)] * 2 + [cols(IDX_ROT_HALF)] * 2
    return pl.pallas_call(
        _even_in_body,
        out_shape=out_shape,
        grid=(seq // tm,),
        in_specs=in_specs,
        out_specs=out_specs,
        compiler_params=_params(("parallel",)),
        name="even_in",
    )(h, g, wn, wt, *rope)


def _ssd_body(xbc_ref, small_ref, dtt_ref, z_ref, cw_ref, cb_ref, dtb_ref, dtbt_ref, a_ref, at_ref,
              drep_ref, gn_ref, o_ref, tail_scr, st_scr, y_scr, xdec_scr, decay_scr):
    q = SSD_CHUNK
    f32 = jnp.float32

    @pl.when(pl.program_id(0) == 0)
    def _():
        tail_scr[...] = jnp.zeros_like(tail_scr)
        st_scr[...] = jnp.zeros_like(st_scr)

    x = xbc_ref[...]
    tail = tail_scr[...]
    row8 = lax.broadcasted_iota(jnp.int32, (8, SSD_CONV_DIM), 0)
    conv = x * cw_ref[SSD_CONV - 1:SSD_CONV, :] + cb_ref[...]
    for shift in range(1, SSD_CONV):
        rolled = pltpu.roll(x, shift, 0)
        head = jnp.where(row8 < shift, pltpu.roll(tail, shift, 0), rolled[0:8])
        shifted = jnp.concatenate([head, rolled[8:]], axis=0)
        conv = conv + shifted * cw_ref[SSD_CONV - 1 - shift:SSD_CONV - shift, :]
    tail_scr[...] = x[q - 8:q]
    xbc = conv * jax.nn.sigmoid(conv)
    xs = xbc[:, :SSD_INNER]
    bm = xbc[:, SSD_INNER:SSD_INNER + SSD_GROUPS * SSD_STATE]
    cm = xbc[:, SSD_INNER + SSD_GROUPS * SSD_STATE:]

    dt_col = _softplus(small_ref[:, _SM_DT:_SM_DT + SSD_HEADS] + dtb_ref[...])
    dt_row = _softplus(dtt_ref[...] + dtbt_ref[...])
    adt_col = dt_col * (-jnp.exp(a_ref[...]))
    adt_row = dt_row * (-jnp.exp(at_ref[...]))
    ri = lax.broadcasted_iota(jnp.int32, (q, q), 0)
    ci = lax.broadcasted_iota(jnp.int32, (q, q), 1)
    causal = ri >= ci
    tril = jnp.where(causal, 1.0, 0.0).astype(MXU_DTYPE)
    triu = jnp.where(ri <= ci, 1.0, 0.0).astype(MXU_DTYPE)
    cs_col = sum(_dot(tril, p) for p in _split3(adt_col))
    cs_row = sum(_dot(p, triu) for p in _split3(adt_row))

    cmb = cm.astype(MXU_DTYPE)
    bmb = bm.astype(MXU_DTYPE)
    for g in range(SSD_GROUPS):
        gs = slice(g * SSD_STATE, (g + 1) * SSD_STATE)
        cb = lax.dot_general(cmb[:, gs], bmb[:, gs], (((1,), (1,)), ((), ())),
                             preferred_element_type=f32)
        kpg = SSD_HEADS // SSD_GROUPS
        for hd in range(g * kpg, (g + 1) * kpg):
            hs = slice(hd * SSD_HEAD_DIM, (hd + 1) * SSD_HEAD_DIM)
            col = cs_col[:, hd:hd + 1]
            row = cs_row[hd:hd + 1, :]
            last = cs_row[hd:hd + 1, q - 1:q]
            decay = jnp.exp(jnp.where(causal, col - row, -jnp.inf))
            xdt = xs[:, hs] * dt_col[:, hd:hd + 1]
            y = _dot((cb * decay).astype(MXU_DTYPE), xdt.astype(MXU_DTYPE))
            y_off = _dot(cmb[:, gs], st_scr[:, hs].astype(MXU_DTYPE))
            y_scr[:, hs] = y + y_off * jnp.exp(col)
            xdec_scr[:, hs] = (xdt * jnp.exp(last - col)).astype(xdec_scr.dtype)
            decay_scr[:, hs] = jnp.broadcast_to(jnp.exp(last), (1, SSD_HEAD_DIM))
        ws = slice(g * kpg * SSD_HEAD_DIM, (g + 1) * kpg * SSD_HEAD_DIM)
        bmt = bm[:, gs].T.astype(MXU_DTYPE)
        st_scr[:, ws] = st_scr[:, ws] * decay_scr[:, ws] + _dot(bmt, xdec_scr[:, ws])

    z = z_ref[...]
    y = (y_scr[...] + drep_ref[...] * xs) * (z * jax.nn.sigmoid(z))
    gw = SSD_INNER // SSD_GROUPS
    for g in range(SSD_GROUPS):
        sl = slice(g * gw, (g + 1) * gw)
        seg = y[:, sl]
        seg = seg * lax.rsqrt(jnp.mean(seg * seg, axis=-1, keepdims=True) + EPS)
        o_ref[:, sl] = (seg * gn_ref[:, sl]).astype(o_ref.dtype)


def _ssd(xbc, small, dtt, z, conv_w, conv_b, dt_bias, a_log, d_skip, gate_norm):
    seq = xbc.shape[0]
    q = SSD_CHUNK
    f32 = jnp.float32

    def rows(width):
        return pl.BlockSpec((q, width), lambda i: (i, 0))

    drep = jnp.repeat(d_skip, SSD_HEAD_DIM)[None, :]
    return pl.pallas_call(
        _ssd_body,
        out_shape=jax.ShapeDtypeStruct((seq, SSD_INNER), MXU_DTYPE),
        grid=(seq // q,),
        in_specs=[rows(SSD_CONV_DIM), rows(LANES), pl.BlockSpec((SSD_HEADS, q), lambda i: (0, i)),
                  rows(SSD_INNER), _const_spec((SSD_CONV, SSD_CONV_DIM)), _const_spec((1, SSD_CONV_DIM)),
                  _const_spec((1, SSD_HEADS)), _const_spec((SSD_HEADS, 1)),
                  _const_spec((1, SSD_HEADS)), _const_spec((SSD_HEADS, 1)),
                  _const_spec((1, SSD_INNER)), _const_spec((1, SSD_INNER))],
        out_specs=rows(SSD_INNER),
        scratch_shapes=[pltpu.VMEM((8, SSD_CONV_DIM), f32),
                        pltpu.VMEM((SSD_STATE, SSD_INNER), f32),
                        pltpu.VMEM((q, SSD_INNER), f32),
                        pltpu.VMEM((q, SSD_INNER), MXU_DTYPE),
                        pltpu.VMEM((1, SSD_INNER), f32)],
        compiler_params=_params(("arbitrary",)),
        name="ssd",
    )(xbc, small, dtt, z, conv_w, conv_b[None, :], dt_bias[None, :], dt_bias[:, None],
      a_log[None, :], a_log[:, None], drep, gate_norm[None, :])


def _dsa_body(qi_ref, kj_ref, qit_ref, lohi_ref, ki_ref, qt_ref, k_ref, vt_ref, o_ref,
              keys_scr, lead_scr, thr_scr, m_scr, l_scr, acc_scr, logit_scr, *, top_k, idx_bits):
    tq, tk, ck = DSA_Q_TILE, DSA_K_TILE, DSA_CHUNK
    i32 = jnp.int32
    p = pl.program_id(0)
    qi = qi_ref[p]
    kj = kj_ref[p]

    @pl.when(kj == 0)
    def _select():
        n_chunks = (qi + 1) * (tq // ck)
        lo = lohi_ref[0:8, :]
        hi = lohi_ref[8:16, :]
        t_pos = qi * tq + lax.broadcasted_iota(i32, (ck, tq), 1)
        s_off = lax.broadcasted_iota(i32, (ck, tq), 0)

        @pl.when(((qi + 1) * tq) % tk != 0)
        def _():
            keys_scr[pl.ds(pl.multiple_of((qi + 1) * tq, tq), tk - tq), :] = jnp.full((tk - tq, tq), INT_MIN, i32)

        def score_chunk(c, carry):
            r0 = pl.multiple_of(c * ck, ck)
            kic = ki_ref[pl.ds(r0, ck), :]
            sc = jnp.zeros((ck, tq), jnp.float32)
            for hd in range(IDX_HEADS):
                y = _dot(kic, qit_ref[hd * IDX_HEAD_DIM:(hd + 1) * IDX_HEAD_DIM, :])
                sc = sc + jnp.minimum(jnp.maximum(y, lo[hd:hd + 1]), hi[hd:hd + 1])
            bits = pltpu.bitcast(sc, i32)
            bits = jnp.where((bits & 0x7FFFFFFF) < MIN_NORMAL_EXP, 0, bits)
            causal = r0 + s_off <= t_pos
            key = bits ^ ((bits >> 31) & 0x7FFFFFFF)
            keys_scr[pl.ds(r0, ck), :] = jnp.where(causal, key, INT_MIN)
            lead = jnp.where(causal, bits & LEAD_MASK, QUIET_NAN_BITS)
            lead_scr[pl.ds(r0, ck), :] = pltpu.bitcast(lead, jnp.float32).astype(jnp.bfloat16)
            return carry

        @pl.when(((qi + 1) * tq) % tk != 0)
        def _():
            lead_scr[pl.ds(pl.multiple_of((qi + 1) * tq, tq), tk - tq), :] = jnp.full(
                (tk - tq, tq), jnp.nan, jnp.bfloat16)

        lax.fori_loop(0, n_chunks, score_chunk, 0)

        n_tiles = ((qi + 1) * tq + tk - 1) // tk
        s_tile = lax.broadcasted_iota(i32, (tk, tq), 0)
        partial_rows = 4 * 8

        def fold(x, op):
            return op(x.reshape(tk // partial_rows, partial_rows, tq), axis=0)

        def count(pred):
            def body(c, acc):
                r0 = pl.multiple_of(c * tk, tk)
                return acc + fold(jnp.where(pred(keys_scr[pl.ds(r0, tk), :], r0), 1.0, 0.0), jnp.sum)
            acc = lax.fori_loop(0, n_tiles, body, jnp.zeros((partial_rows, tq), jnp.float32))
            return acc.sum(axis=0, keepdims=True)

        def count_lead(cand_code):
            cand_code = jnp.where((cand_code > 0) & (cand_code < MIN_NORMAL_CODE), MIN_NORMAL_CODE,
                                  jnp.where((cand_code < 0) & (cand_code >= -MIN_NORMAL_CODE), 0, cand_code))
            cand_bits = (cand_code ^ ((cand_code >> 31) & 0x7FFF)) << 16
            cand = jnp.broadcast_to(pltpu.bitcast(cand_bits, jnp.float32), (16, tq)).astype(jnp.bfloat16)
            one, zero = jnp.ones((), jnp.bfloat16), jnp.zeros((), jnp.bfloat16)

            def body(c, acc):
                r0 = pl.multiple_of(c * tk, tk)
                lead = lead_scr[pl.ds(r0, tk), :].reshape(tk // 16, 16, tq)
                hit = jnp.where(lead >= cand[None], one, zero)
                parts = [hit[g] for g in range(tk // 16)]
                while len(parts) > 1:
                    parts = [a + b for a, b in zip(parts[0::2], parts[1::2])]
                return acc + parts[0].astype(jnp.float32)
            acc = lax.fori_loop(0, n_tiles, body, jnp.zeros((16, tq), jnp.float32))
            return acc.sum(axis=0, keepdims=True)

        k_f = jnp.float32(top_k)
        n_t = (qi * tq + 1 + lax.broadcasted_iota(i32, (1, tq), 1)).astype(jnp.float32)

        def lead_step(b, carry):
            code, c_lo, c_hi = carry
            cand = code + lax.shift_left(i32(1), i32(15) - b)
            cnt = count_lead(cand)
            ok = cnt >= k_f
            return jnp.where(ok, cand, code), jnp.where(ok, cnt, c_lo), jnp.where(ok, c_hi, cnt)

        code, c_lo, c_hi = lax.fori_loop(
            0, 16, lead_step, (jnp.full((1, tq), -(2 ** 15), i32), n_t, jnp.zeros((1, tq), jnp.float32)))

        lo_k = jnp.maximum(code << 16, INT_MIN + 1)
        hi_k = jnp.where(code == 2 ** 15 - 1, jnp.iinfo(i32).max, (code + 1) << 16)

        def finished(lo_k, hi_k, c_lo):
            return (c_lo <= k_f) | (lo_k + 1 >= hi_k)

        def key_value(kk):
            kk = jnp.clip(kk, -FINITE_KEY, FINITE_KEY)
            return pltpu.bitcast(kk ^ ((kk >> 31) & 0x7FFFFFFF), jnp.float32)

        def tighten(lo_k, hi_k, c_lo, c_hi, done, it):
            def body(c, carry):
                mn, mx = carry
                kk = keys_scr[pl.ds(pl.multiple_of(c * tk, tk), tk), :]
                mn = jnp.minimum(mn, fold(jnp.where(kk >= lo_k, kk, jnp.iinfo(i32).max), jnp.min))
                mx = jnp.maximum(mx, fold(jnp.where(kk < hi_k, kk, INT_MIN), jnp.max))
                return mn, mx
            mn, mx = lax.fori_loop(0, n_tiles, body, (jnp.full((partial_rows, tq), jnp.iinfo(i32).max, i32),
                                                      jnp.full((partial_rows, tq), INT_MIN, i32)))
            mn, mx = mn.min(axis=0, keepdims=True), mx.max(axis=0, keepdims=True)
            return jnp.where(done, lo_k, mn), jnp.where(done, hi_k, mx + 1), c_lo, c_hi

        def probe(lo_k, hi_k, c_lo, c_hi, done, it):
            below, span = c_lo - k_f, c_lo - c_hi
            end = jnp.where(below + below >= span, hi_k - 1, lo_k + 1)
            lo_v, hi_v = key_value(lo_k), key_value(hi_k)
            guess_v = lo_v + (below + 0.5) / jnp.maximum(span, 1.0) * (hi_v - lo_v)
            guess_b = pltpu.bitcast(guess_v, i32)
            guess = guess_b ^ ((guess_b >> 31) & 0x7FFFFFFF)
            mid = (lo_k >> 1) + (hi_k >> 1) + (lo_k & hi_k & 1)
            phase = it % 4
            cand = jnp.where(phase == 1, end, jnp.where(phase == 3, mid, guess))
            cand = jnp.where(done, lo_k, jnp.maximum(lo_k + 1, jnp.minimum(hi_k - 1, cand)))
            cnt = count(lambda kk, r0: kk >= cand)
            up = jnp.logical_and(~done, cnt >= k_f)
            down = jnp.logical_and(~done, cnt < k_f)
            return (jnp.where(up, cand, lo_k), jnp.where(down, cand, hi_k),
                    jnp.where(up, cnt, c_lo), jnp.where(down, cnt, c_hi))

        def search_cond(carry):
            return jnp.logical_and(carry[0] < 4 * 32, carry[1] > 0)

        def search_step(carry):
            it, _, lo_k, hi_k, c_lo, c_hi = carry
            done = finished(lo_k, hi_k, c_lo)
            lo_k, hi_k, c_lo, c_hi = lax.cond(it % 4 == 0, tighten, probe, lo_k, hi_k, c_lo, c_hi, done, it)
            still = jnp.where(finished(lo_k, hi_k, c_lo), 0.0, 1.0)
            return it + 1, jnp.max(still).astype(i32), lo_k, hi_k, c_lo, c_hi

        active = jnp.max(jnp.where(finished(lo_k, hi_k, c_lo), 0.0, 1.0)).astype(i32)
        _, _, thr, _, n_ge, _ = lax.while_loop(search_cond, search_step, (i32(0), active, lo_k, hi_k, c_lo, c_hi))
        thr_scr[0:1, :] = thr

        @pl.when(jnp.max(n_ge) > top_k)
        def _ties():
            need = top_k - count(lambda kk, r0: kk > thr)

            def idx_step(b, cut):
                cand = cut + lax.shift_left(i32(1), i32(idx_bits - 1) - b)
                cnt = count(lambda kk, r0: jnp.where(kk == thr, r0 + s_tile, cand) < cand)
                return jnp.where(cnt < need, cand, cut)

            cut = lax.fori_loop(0, idx_bits, idx_step, jnp.zeros((1, tq), i32))
            cut = jnp.where(n_ge > top_k, cut, jnp.iinfo(i32).max)

            def drop(c, carry):
                r0 = pl.multiple_of(c * ck, ck)
                kk = keys_scr[pl.ds(r0, ck), :]
                excess = jnp.where(kk == thr, r0 + s_off, INT_MIN) > cut
                keys_scr[pl.ds(r0, ck), :] = jnp.where(excess, INT_MIN, kk)
                return carry

            lax.fori_loop(0, n_chunks, drop, 0)

        m_scr[...] = jnp.full_like(m_scr, NEG_BIG)
        l_scr[...] = jnp.zeros_like(l_scr)
        acc_scr[...] = jnp.zeros_like(acc_scr)

    sel = keys_scr[pl.ds(pl.multiple_of(kj * tk, tk), tk), :] >= thr_scr[0:1, :]
    bias = jnp.where(sel, 0.0, -jnp.inf)
    col_max = []
    for hd in range(ATT_HEADS):
        hs = slice(hd * ATT_HEAD_DIM, (hd + 1) * ATT_HEAD_DIM)
        logit = _dot(k_ref[:, hs], qt_ref[hs, :]) + bias
        logit_scr[hd] = logit
        col_max.append(jnp.max(logit, axis=0, keepdims=True))
    m_old = m_scr[...]
    m_new = jnp.maximum(m_old, jnp.concatenate(col_max, axis=0))
    alpha = jnp.exp2(m_old - m_new)
    m_scr[...] = m_new
    col_sum = []
    for hd in range(ATT_HEADS):
        hs = slice(hd * ATT_HEAD_DIM, (hd + 1) * ATT_HEAD_DIM)
        prob = jnp.exp2(logit_scr[hd] - m_new[hd:hd + 1])
        col_sum.append(jnp.sum(prob, axis=0, keepdims=True))
        acc_scr[hd] = alpha[hd:hd + 1] * acc_scr[hd] + _dot(vt_ref[hs, :], prob.astype(MXU_DTYPE))
    l_scr[...] = alpha * l_scr[...] + jnp.concatenate(col_sum, axis=0)

    @pl.when(kj == ((qi + 1) * tq - 1) // tk)
    def _finish():
        for hd in range(ATT_HEADS):
            hs = slice(hd * ATT_HEAD_DIM, (hd + 1) * ATT_HEAD_DIM)
            o_ref[:, hs] = (acc_scr[hd] / l_scr[hd:hd + 1, :]).T.astype(o_ref.dtype)


def _dsa(qit, lohi, ki, qt, k, vt):
    seq = k.shape[0]
    tq, tk = DSA_Q_TILE, DSA_K_TILE
    assert seq % tk == 0 and tk % tq == 0 and tq % DSA_CHUNK == 0
    assert qt.shape == (seq // tq, ATT_INNER, tq) and vt.shape == (seq // tk, ATT_INNER, tk)
    nq = seq // tq
    pairs = [(i, j) for i in range(nq) for j in range(((i + 1) * tq - 1) // tk + 1)]
    qi_idx = jnp.asarray([pq for pq, _ in pairs], jnp.int32)
    kj_idx = jnp.asarray([pk for _, pk in pairs], jnp.int32)
    top_k = min(TOPK_MAX, seq // 4)
    idx_bits = max(1, (seq - 1).bit_length())
    grid_spec = pltpu.PrefetchScalarGridSpec(
        num_scalar_prefetch=2,
        grid=(len(pairs),),
        in_specs=[
            pl.BlockSpec((None, IDX_INNER, tq), lambda p, qi, kj: (qi[p], 0, 0)),
            pl.BlockSpec((None, 16, tq), lambda p, qi, kj: (qi[p], 0, 0)),
            pl.BlockSpec((seq, IDX_HEAD_DIM), lambda p, qi, kj: (0, 0), pipeline_mode=pl.Buffered(1)),
            pl.BlockSpec((None, ATT_INNER, tq), lambda p, qi, kj: (qi[p], 0, 0)),
            pl.BlockSpec((tk, ATT_INNER), lambda p, qi, kj: (kj[p], 0)),
            pl.BlockSpec((None, ATT_INNER, tk), lambda p, qi, kj: (kj[p], 0, 0)),
        ],
        out_specs=pl.BlockSpec((tq, ATT_INNER), lambda p, qi, kj: (qi[p], 0)),
        scratch_shapes=[pltpu.VMEM((seq, tq), jnp.int32),
                        pltpu.VMEM((seq, tq), jnp.bfloat16),
                        pltpu.VMEM((8, tq), jnp.int32),
                        pltpu.VMEM((ATT_HEADS, tq), jnp.float32),
                        pltpu.VMEM((ATT_HEADS, tq), jnp.float32),
                        pltpu.VMEM((ATT_HEADS, ATT_HEAD_DIM, tq), jnp.float32),
                        pltpu.VMEM((ATT_HEADS, tk, tq), jnp.float32)],
    )
    return pl.pallas_call(
        functools.partial(_dsa_body, top_k=top_k, idx_bits=idx_bits),
        out_shape=jax.ShapeDtypeStruct((seq, ATT_INNER), MXU_DTYPE),
        grid_spec=grid_spec,
        compiler_params=_params(("arbitrary",)),
        name="dsa",
    )(qi_idx, kj_idx, qit, lohi, ki, qt, k, vt)


def _even_out_body(h_ref, y_ref, o_ref, wy_ref, wo_ref, out_ref):
    out_ref[...] = h_ref[...] + _dot(y_ref[...], wy_ref[...]) + _dot(o_ref[...], wo_ref[...])


def _even_out(h, y, o, wy, wo):
    seq = h.shape[0]
    tm = min(ROW_TILE, seq)
    row = pl.BlockSpec((tm, D_MODEL), lambda i: (i, 0))
    return pl.pallas_call(
        _even_out_body,
        out_shape=jax.ShapeDtypeStruct((seq, D_MODEL), jnp.float32),
        grid=(seq // tm,),
        in_specs=[row, row, row, _const_spec((SSD_INNER, D_MODEL)), _const_spec((ATT_INNER, D_MODEL))],
        out_specs=row,
        compiler_params=_params(("parallel",)),
        name="even_out",
    )(h, y, o, wy, wo)


def _odd_body(h_ref, g_ref, win_ref, lng_ref, lnb_ref, ws_ref, bs_ref, wout_ref, o_ref, gated_scr):
    h = h_ref[...]
    tm = h.shape[0]
    xn = _rms(h, g_ref[...]).astype(MXU_DTYPE)
    u = jax.nn.gelu(_dot(xn, win_ref[:, :SG_INNER]))
    v = jax.nn.gelu(_dot(xn, win_ref[:, SG_INNER:]))
    mu = jnp.mean(v, axis=-1, keepdims=True)
    vc = v - mu
    var = jnp.mean(vc * vc, axis=-1, keepdims=True)
    v = (vc * lax.rsqrt(var + EPS) * lng_ref[...] + lnb_ref[...]).astype(MXU_DTYPE)
    ri = lax.broadcasted_iota(jnp.int32, (SG_CHUNK, SG_CHUNK), 0)
    ci = lax.broadcasted_iota(jnp.int32, (SG_CHUNK, SG_CHUNK), 1)
    gw = SG_INNER // SG_GROUPS
    for g in range(SG_GROUPS):
        w = jnp.where(ri >= ci, ws_ref[g], 0.0).astype(MXU_DTYPE)
        gs = slice(g * gw, (g + 1) * gw)
        for c in range(tm // SG_CHUNK):
            rs = slice(c * SG_CHUNK, (c + 1) * SG_CHUNK)
            mixed = _dot(w, v[rs, gs]) + bs_ref[:, gs]
            gated_scr[rs, gs] = (u[rs, gs] * mixed).astype(gated_scr.dtype)
    o_ref[...] = h + _dot(gated_scr[...], wout_ref[...])


def _odd(h, g, win, ln_g, ln_b, w_s, bs_full, wout):
    seq = h.shape[0]
    tm = min(ROW_TILE, seq)
    row = pl.BlockSpec((tm, D_MODEL), lambda i: (i, 0))
    return pl.pallas_call(
        _odd_body,
        out_shape=jax.ShapeDtypeStruct((seq, D_MODEL), jnp.float32),
        grid=(seq // tm,),
        in_specs=[row, _const_spec((1, D_MODEL)), _const_spec((D_MODEL, 2 * SG_INNER)),
                  _const_spec((1, SG_INNER)), _const_spec((1, SG_INNER)),
                  _const_spec((SG_GROUPS, SG_CHUNK, SG_CHUNK)), _const_spec((SG_CHUNK, SG_INNER)),
                  _const_spec((SG_INNER, D_MODEL))],
        out_specs=row,
        scratch_shapes=[pltpu.VMEM((tm, SG_INNER), MXU_DTYPE)],
        compiler_params=_params(("parallel",)),
        name="odd",
    )(h, g, win, ln_g, ln_b, w_s, bs_full, wout)


def _rope_inputs(seq):
    def tables(rot_dim):
        inv = 1.0 / (ROPE_THETA ** (jnp.arange(0, rot_dim, 2, dtype=jnp.float32) / rot_dim))
        ang = jnp.arange(seq, dtype=jnp.float32)[:, None] * inv[None, :]
        return jnp.cos(ang), jnp.sin(ang)

    def lane_tables(cos, sin):
        half = cos.shape[1]
        pad = LANES - 2 * half
        c = jnp.concatenate([cos, cos, jnp.ones((seq, pad), jnp.float32)], axis=1)
        s_lo = jnp.concatenate([-sin, jnp.zeros((seq, LANES - half), jnp.float32)], axis=1)
        s_hi = jnp.concatenate([jnp.zeros((seq, half), jnp.float32), sin,
                                jnp.zeros((seq, pad), jnp.float32)], axis=1)
        return c, s_lo, s_hi

    cos_a, sin_a = tables(ATT_HEAD_DIM // ROPE_FRACTION)
    cos_i, sin_i = tables(IDX_HEAD_DIM // ROPE_FRACTION)
    return (*lane_tables(cos_a, sin_a), *lane_tables(cos_i, sin_i), cos_a.T, sin_a.T, cos_i.T, sin_i.T)


def _even_weights(w_in):
    offs = np.cumsum((SSD_INNER, SSD_CONV_DIM, SSD_HEADS, ATT_INNER, ATT_INNER, ATT_INNER,
                      IDX_INNER, IDX_HEAD_DIM, IDX_HEADS))[:-1].tolist()
    z, xbc, dt, q, k, v, qi, ki, wi = jnp.split(w_in, offs, axis=-1)
    pad = jnp.zeros((D_MODEL, LANES - IDX_HEAD_DIM - SSD_HEADS), w_in.dtype)
    wn = jnp.concatenate([z, xbc, k, ki, dt, pad], axis=1).astype(MXU_DTYPE)
    wi_pad = jnp.zeros((D_MODEL, _T_DT - _T_WI - IDX_HEADS), w_in.dtype)
    wt = jnp.concatenate([q, v, qi, wi, wi_pad, dt], axis=1).T.astype(MXU_DTYPE)
    assert wn.shape == (D_MODEL, _N_END) and wt.shape == (_T_END, D_MODEL)
    return wn, wt


def _even_mixer(h, g, w_in, conv_w, conv_b, dt_bias, a_log, d_skip, gate_norm, w_out, rope):
    wn, wt = _even_weights(w_in)
    z, xbc, small, k, ki, qt, vt, qit, lohi, dtt = _even_in(h, g, wn, wt, rope)
    y = _ssd(xbc, small, dtt, z, conv_w, conv_b, dt_bias, a_log, d_skip, gate_norm)
    o = _dsa(qit, lohi, ki, qt, k, vt)
    w_out = w_out.astype(MXU_DTYPE)
    return _even_out(h, y, o, w_out[:SSD_INNER], w_out[SSD_INNER:])


def _forward(x, norm_g, final_g, ffn_w_gu, ffn_w_down, ev_w_in, ev_conv_w, ev_conv_b, ev_dt_bias,
             ev_a_log, ev_d, ev_gate_norm, ev_w_out, od_w_in, od_ln_g, od_ln_b, od_w_s, od_b_s, od_w_out):
    bsz, seq, _ = x.shape
    depth = norm_g.shape[0]
    rope = _rope_inputs(seq)
    wg = ffn_w_gu[..., :FFN_HIDDEN].astype(MXU_DTYPE)
    wu = ffn_w_gu[..., FFN_HIDDEN:].astype(MXU_DTYPE)
    wd = ffn_w_down.astype(MXU_DTYPE)
    fg = final_g[None, :]
    outs = []
    for b in range(bsz):
        h = x[b]
        for layer in range(depth):
            j = layer // 2
            h = _ffn(h, norm_g[layer, 0][None, :], wg[layer, 0], wu[layer, 0], wd[layer, 0], fg, False)
            g1 = norm_g[layer, 1][None, :]
            if layer % 2 == 0:
                h = _even_mixer(h, g1, ev_w_in[j], ev_conv_w[j], ev_conv_b[j], ev_dt_bias[j], ev_a_log[j],
                                ev_d[j], ev_gate_norm[j], ev_w_out[j], rope)
            else:
                bs_full = jnp.repeat(od_b_s[j].T, SG_INNER // SG_GROUPS, axis=1)
                h = _odd(h, g1, od_w_in[j].astype(MXU_DTYPE), od_ln_g[j][None, :], od_ln_b[j][None, :],
                         od_w_s[j], bs_full, od_w_out[j].astype(MXU_DTYPE))
            h = _ffn(h, norm_g[layer, 2][None, :], wg[layer, 1], wu[layer, 1], wd[layer, 1], fg,
                     layer == depth - 1)
        outs.append(h)
    return jnp.stack(outs, axis=0)


def kernel(x, norm_g, final_g, ffn_w_gu, ffn_w_down, ev_w_in, ev_conv_w, ev_conv_b, ev_dt_bias, ev_a_log,
           ev_d, ev_gate_norm, ev_w_out, od_w_in, od_ln_g, od_ln_b, od_w_s, od_b_s, od_w_out):
    return _forward(x, norm_g, final_g, ffn_w_gu, ffn_w_down, ev_w_in, ev_conv_w, ev_conv_b, ev_dt_bias,
                    ev_a_log, ev_d, ev_gate_norm, ev_w_out, od_w_in, od_ln_g, od_ln_b, od_w_s, od_b_s,
                    od_w_out)
```

```python
import functools
import math

import numpy as np
import jax
import jax.numpy as jnp
from jax import lax
from jax.experimental import pallas as pl
from jax.experimental.pallas import tpu as pltpu

D_MODEL = 1024
SSD_HEADS = 16
SSD_HEAD_DIM = 64
SSD_INNER = SSD_HEADS * SSD_HEAD_DIM
SSD_GROUPS = 2
SSD_STATE = 128
SSD_CONV = 4
SSD_CHUNK = 256
SSD_CONV_DIM = SSD_INNER + 2 * SSD_GROUPS * SSD_STATE
ATT_HEADS = 8
ATT_HEAD_DIM = 128
ATT_INNER = ATT_HEADS * ATT_HEAD_DIM
IDX_HEADS = 8
IDX_HEAD_DIM = 64
IDX_INNER = IDX_HEADS * IDX_HEAD_DIM
TOPK_MAX = 256
SG_CHUNK = 128
SG_GROUPS = 8
SG_INNER = 2 * D_MODEL
FFN_HIDDEN = 2816
ROPE_THETA = 500000.0
ROPE_FRACTION = 4
EPS = 1e-6

ATT_ROT_HALF = ATT_HEAD_DIM // ROPE_FRACTION // 2
IDX_ROT_HALF = IDX_HEAD_DIM // ROPE_FRACTION // 2

MXU_DTYPE = jnp.bfloat16
LANES = 128
VMEM_LIMIT_BYTES = 56 * 1024 * 1024

ROW_TILE = 512
FFN_HIDDEN_CHUNK = FFN_HIDDEN // 2
DSA_Q_TILE = 256
DSA_K_TILE = 512
DSA_CHUNK = 256
LEAD_MASK = -(2 ** 16)
QUIET_NAN_BITS = 0x7FC00000
FINITE_KEY = 0x7F7FFFFF
MIN_NORMAL_EXP = 0x00800000
MIN_NORMAL_CODE = MIN_NORMAL_EXP >> 16
INT_MIN = -(2 ** 31)
NEG_BIG = -0.7 * float(np.finfo(np.float32).max)

_N_Z = 0
_N_XBC = _N_Z + SSD_INNER
_N_K = _N_XBC + SSD_CONV_DIM
_N_SMALL = _N_K + ATT_INNER
_N_END = _N_SMALL + LANES
_SM_KI = 0
_SM_DT = IDX_HEAD_DIM
_T_Q = 0
_T_V = _T_Q + ATT_INNER
_T_QI = _T_V + ATT_INNER
_T_WI = _T_QI + IDX_INNER
_T_DT = _T_WI + 16
_T_END = _T_DT + SSD_HEADS


def _dot(a, b):
    return jnp.dot(a, b, preferred_element_type=jnp.float32)


def _rms(x, g):
    return x * lax.rsqrt(jnp.mean(x * x, axis=-1, keepdims=True) + EPS) * g


def _softplus(x):
    return jnp.maximum(x, 0.0) + jnp.log1p(jnp.exp(-jnp.abs(x)))


def _split3(x):
    hi = x.astype(MXU_DTYPE)
    r = x - hi.astype(jnp.float32)
    mid = r.astype(MXU_DTYPE)
    lo = (r - mid.astype(jnp.float32)).astype(MXU_DTYPE)
    return hi, mid, lo


def _const_spec(shape):
    zeros = (0,) * len(shape)
    return pl.BlockSpec(shape, lambda *_: zeros, pipeline_mode=pl.Buffered(1))


def _params(sem):
    return pltpu.CompilerParams(dimension_semantics=sem, vmem_limit_bytes=VMEM_LIMIT_BYTES)


def _ffn_body(h_ref, g_ref, wg_ref, wu_ref, wd_ref, fg_ref, o_ref, *, final):
    h = h_ref[...]
    xn = _rms(h, g_ref[...]).astype(MXU_DTYPE)
    acc = None
    for c in range(FFN_HIDDEN // FFN_HIDDEN_CHUNK):
        sl = slice(c * FFN_HIDDEN_CHUNK, (c + 1) * FFN_HIDDEN_CHUNK)
        gate = _dot(xn, wg_ref[:, sl])
        up = _dot(xn, wu_ref[:, sl])
        act = (gate * jax.nn.sigmoid(gate) * up).astype(MXU_DTYPE)
        part = _dot(act, wd_ref[sl, :])
        acc = part if acc is None else acc + part
    out = h + 0.5 * acc
    if final:
        out = _rms(out, fg_ref[...])
    o_ref[...] = out


def _ffn(h, g, wg, wu, wd, final_g, final):
    seq = h.shape[0]
    tm = min(ROW_TILE, seq)
    row = pl.BlockSpec((tm, D_MODEL), lambda i: (i, 0))
    return pl.pallas_call(
        functools.partial(_ffn_body, final=final),
        out_shape=jax.ShapeDtypeStruct((seq, D_MODEL), jnp.float32),
        grid=(seq // tm,),
        in_specs=[row, _const_spec((1, D_MODEL)), _const_spec((D_MODEL, FFN_HIDDEN)),
                  _const_spec((D_MODEL, FFN_HIDDEN)), _const_spec((FFN_HIDDEN, D_MODEL)),
                  _const_spec((1, D_MODEL))],
        out_specs=row,
        compiler_params=_params(("parallel",)),
        name="ffn",
    )(h, g, wg, wu, wd, final_g)


def _rope_lanes(x, c, s_lo, s_hi, half):
    return x * c + pltpu.roll(x, LANES - half, 1) * s_lo + pltpu.roll(x, half, 1) * s_hi


def _rope_rows(x, c, s, half):
    x1, x2 = x[:half], x[half:2 * half]
    return jnp.concatenate([x1 * c - x2 * s, x1 * s + x2 * c, x[2 * half:]], axis=0)


def _even_in_body(h_ref, g_ref, wn_ref, wt_ref, ca_ref, sal_ref, sah_ref, ci_ref, sil_ref, sih_ref,
                  cat_ref, sat_ref, cit_ref, sit_ref,
                  z_ref, xbc_ref, small_ref, k_ref, ki_ref, qt_ref, vt_ref, qit_ref, lohi_ref, dtt_ref):
    xn = _rms(h_ref[...], g_ref[...])
    xnb = xn.astype(MXU_DTYPE)
    xnt = xn.T.astype(MXU_DTYPE)

    z_ref[...] = _dot(xnb, wn_ref[:, _N_Z:_N_XBC])
    xbc_ref[...] = _dot(xnb, wn_ref[:, _N_XBC:_N_K])
    kk = _dot(xnb, wn_ref[:, _N_K:_N_SMALL])
    ca, sal, sah = ca_ref[...], sal_ref[...], sah_ref[...]
    for hd in range(ATT_HEADS):
        sl = slice(hd * ATT_HEAD_DIM, (hd + 1) * ATT_HEAD_DIM)
        k_ref[hd] = _rope_lanes(kk[:, sl], ca, sal, sah, ATT_ROT_HALF).astype(k_ref.dtype)
    sm = _dot(xnb, wn_ref[:, _N_SMALL:_N_END])
    small_ref[...] = sm
    smr = _rope_lanes(sm, ci_ref[...], sil_ref[...], sih_ref[...], IDX_ROT_HALF)
    ki_ref[...] = smr[:, _SM_KI:_SM_KI + IDX_HEAD_DIM].astype(ki_ref.dtype)

    att_scale = ATT_HEAD_DIM ** -0.5 * math.log2(math.e)
    qt = _dot(wt_ref[_T_Q:_T_V, :], xnt)
    cat, sat = cat_ref[...], sat_ref[...]
    def put(ref, where, val):
        tb = ref.shape[-1]
        for b in range(ref.shape[0]):
            ref[(b, *where, slice(None))] = val[:, b * tb:(b + 1) * tb].astype(ref.dtype)

    vt = _dot(wt_ref[_T_V:_T_QI, :], xnt)
    for hd in range(ATT_HEADS):
        sl = slice(hd * ATT_HEAD_DIM, (hd + 1) * ATT_HEAD_DIM)
        put(qt_ref, (hd, slice(None)), _rope_rows(qt[sl], cat, sat, ATT_ROT_HALF) * att_scale)
        put(vt_ref, (hd, slice(None)), vt[sl])
    tail = _dot(wt_ref[_T_WI:_T_END, :], xnt)
    wit = tail[0:IDX_HEADS] * (IDX_HEADS ** -0.5 * IDX_HEAD_DIM ** -0.5)
    dtt_ref[...] = tail[_T_DT - _T_WI:_T_END - _T_WI]
    pos = wit > 0.0
    put(lohi_ref, (slice(0, 8),), jnp.where(pos, 0.0, -jnp.inf))
    put(lohi_ref, (slice(8, 16),), jnp.where(pos, jnp.inf, 0.0))
    qit = _dot(wt_ref[_T_QI:_T_WI, :], xnt)
    cit, sit = cit_ref[...], sit_ref[...]
    for hd in range(IDX_HEADS):
        sl = slice(hd * IDX_HEAD_DIM, (hd + 1) * IDX_HEAD_DIM)
        put(qit_ref, (sl,), _rope_rows(qit[sl], cit, sit, IDX_ROT_HALF) * wit[hd:hd + 1])


def _even_in(h, g, wn, wt, rope):
    seq = h.shape[0]
    tm = min(ROW_TILE, seq)
    f32 = jnp.float32

    def rows(width):
        return pl.BlockSpec((tm, width), lambda i: (i, 0))

    def cols(height):
        return pl.BlockSpec((height, tm), lambda i: (0, i))

    def blocks(height, tb):
        return pl.BlockSpec((tm // tb, height, tb), lambda i: (i, 0, 0))

    def head_blocks(tb):
        return pl.BlockSpec((tm // tb, ATT_HEADS, ATT_HEAD_DIM, tb), lambda i: (i, 0, 0, 0))

    assert tm % DSA_K_TILE == 0 and tm % DSA_Q_TILE == 0
    nq, nk = seq // DSA_Q_TILE, seq // DSA_K_TILE
    out_shape = (
        jax.ShapeDtypeStruct((seq, SSD_INNER), f32),
        jax.ShapeDtypeStruct((seq, SSD_CONV_DIM), f32),
        jax.ShapeDtypeStruct((seq, LANES), f32),
        jax.ShapeDtypeStruct((ATT_HEADS, seq, ATT_HEAD_DIM), MXU_DTYPE),
        jax.ShapeDtypeStruct((seq, IDX_HEAD_DIM), MXU_DTYPE),
        jax.ShapeDtypeStruct((nq, ATT_HEADS, ATT_HEAD_DIM, DSA_Q_TILE), MXU_DTYPE),
        jax.ShapeDtypeStruct((nk, ATT_HEADS, ATT_HEAD_DIM, DSA_K_TILE), MXU_DTYPE),
        jax.ShapeDtypeStruct((nq, IDX_INNER, DSA_Q_TILE), MXU_DTYPE),
        jax.ShapeDtypeStruct((nq, 16, DSA_Q_TILE), f32),
        jax.ShapeDtypeStruct((SSD_HEADS, seq), f32),
    )
    out_specs = (rows(SSD_INNER), rows(SSD_CONV_DIM), rows(LANES),
                 pl.BlockSpec((ATT_HEADS, tm, ATT_HEAD_DIM), lambda i: (0, i, 0)), rows(IDX_HEAD_DIM),
                 head_blocks(DSA_Q_TILE), head_blocks(DSA_K_TILE), blocks(IDX_INNER, DSA_Q_TILE),
                 blocks(16, DSA_Q_TILE), cols(SSD_HEADS))
    in_specs = [rows(D_MODEL), _const_spec((1, D_MODEL)), _const_spec(wn.shape), _const_spec(wt.shape)]
    in_specs += [rows(LANES)] * 6
    in_specs += [cols(ATT_ROT_HALF)] * 2 + [cols(IDX_ROT_HALF)] * 2
    return pl.pallas_call(
        _even_in_body,
        out_shape=out_shape,
        grid=(seq // tm,),
        in_specs=in_specs,
        out_specs=out_specs,
        compiler_params=_params(("parallel",)),
        name="even_in",
    )(h, g, wn, wt, *rope)


def _ssd_body(xbc_ref, small_ref, dtt_ref, z_ref, cw_ref, cb_ref, dtb_ref, dtbt_ref, a_ref, at_ref,
              drep_ref, gn_ref, o_ref, tail_scr, st_scr, y_scr, xdec_scr, decay_scr):
    q = SSD_CHUNK
    f32 = jnp.float32

    @pl.when(pl.program_id(0) == 0)
    def _():
        tail_scr[...] = jnp.zeros_like(tail_scr)
        st_scr[...] = jnp.zeros_like(st_scr)

    x = xbc_ref[...]
    tail = tail_scr[...]
    row8 = lax.broadcasted_iota(jnp.int32, (8, SSD_CONV_DIM), 0)
    conv = x * cw_ref[SSD_CONV - 1:SSD_CONV, :] + cb_ref[...]
    for shift in range(1, SSD_CONV):
        rolled = pltpu.roll(x, shift, 0)
        head = jnp.where(row8 < shift, pltpu.roll(tail, shift, 0), rolled[0:8])
        shifted = jnp.concatenate([head, rolled[8:]], axis=0)
        conv = conv + shifted * cw_ref[SSD_CONV - 1 - shift:SSD_CONV - shift, :]
    tail_scr[...] = x[q - 8:q]
    xbc = conv * jax.nn.sigmoid(conv)
    xs = xbc[:, :SSD_INNER]
    bm = xbc[:, SSD_INNER:SSD_INNER + SSD_GROUPS * SSD_STATE]
    cm = xbc[:, SSD_INNER + SSD_GROUPS * SSD_STATE:]

    dt_col = _softplus(small_ref[:, _SM_DT:_SM_DT + SSD_HEADS] + dtb_ref[...])
    dt_row = _softplus(dtt_ref[...] + dtbt_ref[...])
    adt_col = dt_col * (-jnp.exp(a_ref[...]))
    adt_row = dt_row * (-jnp.exp(at_ref[...]))
    ri = lax.broadcasted_iota(jnp.int32, (q, q), 0)
    ci = lax.broadcasted_iota(jnp.int32, (q, q), 1)
    causal = ri >= ci
    tril = jnp.where(causal, 1.0, 0.0).astype(MXU_DTYPE)
    triu = jnp.where(ri <= ci, 1.0, 0.0).astype(MXU_DTYPE)
    cs_col = sum(_dot(tril, p) for p in _split3(adt_col))
    cs_row = sum(_dot(p, triu) for p in _split3(adt_row))

    cmb = cm.astype(MXU_DTYPE)
    bmb = bm.astype(MXU_DTYPE)
    for g in range(SSD_GROUPS):
        gs = slice(g * SSD_STATE, (g + 1) * SSD_STATE)
        cb = lax.dot_general(cmb[:, gs], bmb[:, gs], (((1,), (1,)), ((), ())),
                             preferred_element_type=f32)
        kpg = SSD_HEADS // SSD_GROUPS
        for hd in range(g * kpg, (g + 1) * kpg):
            hs = slice(hd * SSD_HEAD_DIM, (hd + 1) * SSD_HEAD_DIM)
            col = cs_col[:, hd:hd + 1]
            row = cs_row[hd:hd + 1, :]
            last = cs_row[hd:hd + 1, q - 1:q]
            decay = jnp.exp(jnp.where(causal, col - row, -jnp.inf))
            xdt = xs[:, hs] * dt_col[:, hd:hd + 1]
            y = _dot((cb * decay).astype(MXU_DTYPE), xdt.astype(MXU_DTYPE))
            y_off = _dot(cmb[:, gs], st_scr[:, hs].astype(MXU_DTYPE))
            y_scr[:, hs] = y + y_off * jnp.exp(col)
            xdec_scr[:, hs] = (xdt * jnp.exp(last - col)).astype(xdec_scr.dtype)
            decay_scr[:, hs] = jnp.broadcast_to(jnp.exp(last), (1, SSD_HEAD_DIM))
        ws = slice(g * kpg * SSD_HEAD_DIM, (g + 1) * kpg * SSD_HEAD_DIM)
        bmt = bm[:, gs].T.astype(MXU_DTYPE)
        st_scr[:, ws] = st_scr[:, ws] * decay_scr[:, ws] + _dot(bmt, xdec_scr[:, ws])

    z = z_ref[...]
    y = (y_scr[...] + drep_ref[...] * xs) * (z * jax.nn.sigmoid(z))
    gw = SSD_INNER // SSD_GROUPS
    for g in range(SSD_GROUPS):
        sl = slice(g * gw, (g + 1) * gw)
        seg = y[:, sl]
        seg = seg * lax.rsqrt(jnp.mean(seg * seg, axis=-1, keepdims=True) + EPS)
        o_ref[:, sl] = (seg * gn_ref[:, sl]).astype(o_ref.dtype)


def _ssd(xbc, small, dtt, z, conv_w, conv_b, dt_bias, a_log, d_skip, gate_norm):
    seq = xbc.shape[0]
    q = SSD_CHUNK
    f32 = jnp.float32

    def rows(width):
        return pl.BlockSpec((q, width), lambda i: (i, 0))

    drep = jnp.repeat(d_skip, SSD_HEAD_DIM)[None, :]
    return pl.pallas_call(
        _ssd_body,
        out_shape=jax.ShapeDtypeStruct((seq, SSD_INNER), MXU_DTYPE),
        grid=(seq // q,),
        in_specs=[rows(SSD_CONV_DIM), rows(LANES), pl.BlockSpec((SSD_HEADS, q), lambda i: (0, i)),
                  rows(SSD_INNER), _const_spec((SSD_CONV, SSD_CONV_DIM)), _const_spec((1, SSD_CONV_DIM)),
                  _const_spec((1, SSD_HEADS)), _const_spec((SSD_HEADS, 1)),
                  _const_spec((1, SSD_HEADS)), _const_spec((SSD_HEADS, 1)),
                  _const_spec((1, SSD_INNER)), _const_spec((1, SSD_INNER))],
        out_specs=rows(SSD_INNER),
        scratch_shapes=[pltpu.VMEM((8, SSD_CONV_DIM), f32),
                        pltpu.VMEM((SSD_STATE, SSD_INNER), f32),
                        pltpu.VMEM((q, SSD_INNER), f32),
                        pltpu.VMEM((q, SSD_INNER), MXU_DTYPE),
                        pltpu.VMEM((1, SSD_INNER), f32)],
        compiler_params=_params(("arbitrary",)),
        name="ssd",
    )(xbc, small, dtt, z, conv_w, conv_b[None, :], dt_bias[None, :], dt_bias[:, None],
      a_log[None, :], a_log[:, None], drep, gate_norm[None, :])


def _dsa_body(qi_ref, kj_ref, qit_ref, lohi_ref, ki_ref, qt_ref, k_ref, vt_ref, o_ref,
              keys_scr, lead_scr, y_scr, thr_scr, m_scr, l_scr, acc_scr, logit_scr, *, top_k, idx_bits):
    tq, tk, ck = DSA_Q_TILE, DSA_K_TILE, DSA_CHUNK
    i32 = jnp.int32
    p = pl.program_id(0)
    qi = qi_ref[p]
    kj = kj_ref[p]

    @pl.when(kj == 0)
    def _select():
        chunks_per_tile = tk // ck
        n_tiles = ((qi + 1) * tq + tk - 1) // tk
        n_chunks = n_tiles * chunks_per_tile
        lo = lohi_ref[0:8, :]
        hi = lohi_ref[8:16, :]
        rows = 4 * 8
        t_pos = qi * tq + lax.broadcasted_iota(i32, (rows, tq), 1)
        s_off = lax.broadcasted_iota(i32, (ck, tq), 0)
        s_rows = lax.broadcasted_iota(i32, (rows, tq), 0)

        def project(c, slot):
            kic = ki_ref[pl.ds(pl.multiple_of(c * ck, ck), ck), :]
            for hd in range(IDX_HEADS):
                y_scr[slot, hd] = _dot(kic, qit_ref[hd * IDX_HEAD_DIM:(hd + 1) * IDX_HEAD_DIM, :])

        def emit_keys(c, slot, diagonal):
            for g in range(ck // rows):
                r0 = pl.multiple_of(c * ck + g * rows, rows)
                gs = slice(g * rows, (g + 1) * rows)
                sc = None
                for hd in range(IDX_HEADS):
                    term = jnp.minimum(jnp.maximum(y_scr[slot, hd, gs, :], lo[hd:hd + 1]), hi[hd:hd + 1])
                    sc = term if sc is None else sc + term
                bits = pltpu.bitcast(sc, i32)
                bits = jnp.where((bits & 0x7FFFFFFF) < MIN_NORMAL_EXP, 0, bits)
                key = bits ^ ((bits >> 31) & 0x7FFFFFFF)
                lead = bits & LEAD_MASK
                if diagonal:
                    causal = r0 + s_rows <= t_pos
                    key = jnp.where(causal, key, INT_MIN)
                    lead = jnp.where(causal, lead, QUIET_NAN_BITS)
                keys_scr[pl.ds(r0, rows), :] = key
                lead_scr[pl.ds(r0, rows), :] = pltpu.bitcast(lead, jnp.float32).astype(jnp.bfloat16)

        assert chunks_per_tile % 2 == 0

        def score_tile(t, diagonal):
            for j in range(chunks_per_tile):
                c = t * chunks_per_tile + j
                emit_keys(c, j % 2, diagonal)
                if not (diagonal and j == chunks_per_tile - 1):
                    project(c + 1, (j + 1) % 2)

        project(0, 0)
        lax.fori_loop(0, n_tiles - 1, lambda t, carry: (score_tile(t, False), carry)[1], 0)
        score_tile(n_tiles - 1, True)

        partial_rows = 4 * 8
        s_tile = lax.broadcasted_iota(i32, (partial_rows, tq), 0)

        def key_groups(c):
            for g in range(tk // partial_rows):
                r0 = pl.multiple_of(c * tk + g * partial_rows, partial_rows)
                yield keys_scr[pl.ds(r0, partial_rows), :], r0

        def count(pred):
            def body(c, acc):
                for kk, r0 in key_groups(c):
                    acc = acc + jnp.where(pred(kk, r0), 1.0, 0.0)
                return acc
            acc = lax.fori_loop(0, n_tiles, body, jnp.zeros((partial_rows, tq), jnp.float32))
            return acc.sum(axis=0, keepdims=True)

        def count_lead(cand_code):
            cand_code = jnp.where((cand_code > 0) & (cand_code < MIN_NORMAL_CODE), MIN_NORMAL_CODE,
                                  jnp.where((cand_code < 0) & (cand_code >= -MIN_NORMAL_CODE), 0, cand_code))
            cand_bits = (cand_code ^ ((cand_code >> 31) & 0x7FFF)) << 16
            cand = jnp.broadcast_to(pltpu.bitcast(cand_bits, jnp.float32), (16, tq)).astype(jnp.bfloat16)
            one, zero = jnp.ones((), jnp.bfloat16), jnp.zeros((), jnp.bfloat16)

            def body(c, acc):
                parts = []
                for g in range(tk // 128):
                    r0 = pl.multiple_of(c * tk + g * 128, 128)
                    lead = lead_scr[pl.ds(r0, 128), :].reshape(8, 16, tq)
                    parts += [jnp.where(lead[j] >= cand, one, zero) for j in range(8)]
                while len(parts) > 1:
                    parts = [a + b for a, b in zip(parts[0::2], parts[1::2])]
                return acc + parts[0].astype(jnp.float32)
            acc = lax.fori_loop(0, n_tiles, body, jnp.zeros((16, tq), jnp.float32))
            return acc.sum(axis=0, keepdims=True)

        k_f = jnp.float32(top_k)
        n_t = (qi * tq + 1 + lax.broadcasted_iota(i32, (1, tq), 1)).astype(jnp.float32)

        def lead_step(b, carry):
            code, c_lo, c_hi = carry
            cand = code + lax.shift_left(i32(1), i32(15) - b)
            cnt = count_lead(cand)
            ok = cnt >= k_f
            return jnp.where(ok, cand, code), jnp.where(ok, cnt, c_lo), jnp.where(ok, c_hi, cnt)

        code, c_lo, c_hi = lax.fori_loop(
            0, 16, lead_step, (jnp.full((1, tq), -(2 ** 15), i32), n_t, jnp.zeros((1, tq), jnp.float32)))

        lo_k = jnp.maximum(code << 16, INT_MIN + 1)
        hi_k = jnp.where(code == 2 ** 15 - 1, jnp.iinfo(i32).max, (code + 1) << 16)

        def finished(lo_k, hi_k, c_lo):
            return (c_lo <= k_f) | (lo_k + 1 >= hi_k)

        def key_value(kk):
            kk = jnp.clip(kk, -FINITE_KEY, FINITE_KEY)
            return pltpu.bitcast(kk ^ ((kk >> 31) & 0x7FFFFFFF), jnp.float32)

        def tighten(lo_k, hi_k, c_lo, c_hi, done, it):
            def body(c, carry):
                mn, mx = carry
                for kk, _ in key_groups(c):
                    mn = jnp.minimum(mn, jnp.where(kk >= lo_k, kk, jnp.iinfo(i32).max))
                    mx = jnp.maximum(mx, jnp.where(kk < hi_k, kk, INT_MIN))
                return mn, mx
            mn, mx = lax.fori_loop(0, n_tiles, body, (jnp.full((partial_rows, tq), jnp.iinfo(i32).max, i32),
                                                      jnp.full((partial_rows, tq), INT_MIN, i32)))
            mn, mx = mn.min(axis=0, keepdims=True), mx.max(axis=0, keepdims=True)
            return jnp.where(done, lo_k, mn), jnp.where(done, hi_k, mx + 1), c_lo, c_hi

        def probe(lo_k, hi_k, c_lo, c_hi, done, it):
            below, span = c_lo - k_f, c_lo - c_hi
            end = jnp.where(below + below >= span, hi_k - 1, lo_k + 1)
            lo_v, hi_v = key_value(lo_k), key_value(hi_k)
            guess_v = lo_v + (below + 0.5) / jnp.maximum(span, 1.0) * (hi_v - lo_v)
            guess_b = pltpu.bitcast(guess_v, i32)
            guess = guess_b ^ ((guess_b >> 31) & 0x7FFFFFFF)
            mid = (lo_k >> 1) + (hi_k >> 1) + (lo_k & hi_k & 1)
            phase = it % 4
            cand = jnp.where(phase == 1, end, jnp.where(phase == 3, mid, guess))
            cand = jnp.where(done, lo_k, jnp.maximum(lo_k + 1, jnp.minimum(hi_k - 1, cand)))
            cnt = count(lambda kk, r0: kk >= cand)
            up = jnp.logical_and(~done, cnt >= k_f)
            down = jnp.logical_and(~done, cnt < k_f)
            return (jnp.where(up, cand, lo_k), jnp.where(down, cand, hi_k),
                    jnp.where(up, cnt, c_lo), jnp.where(down, cnt, c_hi))

        def search_cond(carry):
            return jnp.logical_and(carry[0] < 4 * 32, carry[1] > 0)

        def search_step(carry):
            it, _, lo_k, hi_k, c_lo, c_hi = carry
            done = finished(lo_k, hi_k, c_lo)
            lo_k, hi_k, c_lo, c_hi = lax.cond(it % 4 == 0, tighten, probe, lo_k, hi_k, c_lo, c_hi, done, it)
            still = jnp.where(finished(lo_k, hi_k, c_lo), 0.0, 1.0)
            return it + 1, jnp.max(still).astype(i32), lo_k, hi_k, c_lo, c_hi

        active = jnp.max(jnp.where(finished(lo_k, hi_k, c_lo), 0.0, 1.0)).astype(i32)
        _, _, thr, _, n_ge, _ = lax.while_loop(search_cond, search_step, (i32(0), active, lo_k, hi_k, c_lo, c_hi))
        thr_scr[0:1, :] = thr

        @pl.when(jnp.max(n_ge) > top_k)
        def _ties():
            need = top_k - count(lambda kk, r0: kk > thr)

            def idx_step(b, cut):
                cand = cut + lax.shift_left(i32(1), i32(idx_bits - 1) - b)
                cnt = count(lambda kk, r0: jnp.where(kk == thr, r0 + s_tile, cand) < cand)
                return jnp.where(cnt < need, cand, cut)

            cut = lax.fori_loop(0, idx_bits, idx_step, jnp.zeros((1, tq), i32))
            cut = jnp.where(n_ge > top_k, cut, jnp.iinfo(i32).max)

            def drop(c, carry):
                r0 = pl.multiple_of(c * ck, ck)
                kk = keys_scr[pl.ds(r0, ck), :]
                excess = jnp.where(kk == thr, r0 + s_off, INT_MIN) > cut
                keys_scr[pl.ds(r0, ck), :] = jnp.where(excess, INT_MIN, kk)
                return carry

            lax.fori_loop(0, n_chunks, drop, 0)

        m_scr[...] = jnp.full_like(m_scr, NEG_BIG)
        l_scr[...] = jnp.zeros_like(l_scr)
        acc_scr[...] = jnp.zeros_like(acc_scr)

    sel = keys_scr[pl.ds(pl.multiple_of(kj * tk, tk), tk), :] >= thr_scr[0:1, :]
    bias = jnp.where(sel, 0.0, -jnp.inf)
    col_max = []
    for hd in range(ATT_HEADS):
        logit = _dot(k_ref[hd], qt_ref[hd]) + bias
        logit_scr[hd] = logit
        col_max.append(jnp.max(logit, axis=0, keepdims=True))
    m_old = m_scr[...]
    m_new = jnp.maximum(m_old, jnp.concatenate(col_max, axis=0))
    alpha = jnp.exp2(m_old - m_new)
    m_scr[...] = m_new
    col_sum = []
    for hd in range(ATT_HEADS):
        prob = jnp.exp2(logit_scr[hd] - m_new[hd:hd + 1])
        col_sum.append(jnp.sum(prob, axis=0, keepdims=True))
        acc_scr[hd] = alpha[hd:hd + 1] * acc_scr[hd] + _dot(vt_ref[hd], prob.astype(MXU_DTYPE))
    l_scr[...] = alpha * l_scr[...] + jnp.concatenate(col_sum, axis=0)

    @pl.when(kj == ((qi + 1) * tq - 1) // tk)
    def _finish():
        for hd in range(ATT_HEADS):
            hs = slice(hd * ATT_HEAD_DIM, (hd + 1) * ATT_HEAD_DIM)
            o_ref[:, hs] = (acc_scr[hd] / l_scr[hd:hd + 1, :]).T.astype(o_ref.dtype)


def _dsa(qit, lohi, ki, qt, k, vt):
    seq = k.shape[1]
    tq, tk = DSA_Q_TILE, DSA_K_TILE
    assert seq % tk == 0 and tk % tq == 0 and tq % DSA_CHUNK == 0
    assert k.shape == (ATT_HEADS, seq, ATT_HEAD_DIM)
    assert qt.shape == (seq // tq, ATT_HEADS, ATT_HEAD_DIM, tq) and vt.shape == (seq // tk, ATT_HEADS, ATT_HEAD_DIM, tk)
    nq = seq // tq
    pairs = [(i, j) for i in range(nq) for j in range(((i + 1) * tq - 1) // tk + 1)]
    qi_idx = jnp.asarray([pq for pq, _ in pairs], jnp.int32)
    kj_idx = jnp.asarray([pk for _, pk in pairs], jnp.int32)
    top_k = min(TOPK_MAX, seq // 4)
    idx_bits = max(1, (seq - 1).bit_length())
    grid_spec = pltpu.PrefetchScalarGridSpec(
        num_scalar_prefetch=2,
        grid=(len(pairs),),
        in_specs=[
            pl.BlockSpec((None, IDX_INNER, tq), lambda p, qi, kj: (qi[p], 0, 0)),
            pl.BlockSpec((None, 16, tq), lambda p, qi, kj: (qi[p], 0, 0)),
            pl.BlockSpec((seq, IDX_HEAD_DIM), lambda p, qi, kj: (0, 0), pipeline_mode=pl.Buffered(1)),
            pl.BlockSpec((None, ATT_HEADS, ATT_HEAD_DIM, tq), lambda p, qi, kj: (qi[p], 0, 0, 0)),
            pl.BlockSpec((ATT_HEADS, tk, ATT_HEAD_DIM), lambda p, qi, kj: (0, kj[p], 0)),
            pl.BlockSpec((None, ATT_HEADS, ATT_HEAD_DIM, tk), lambda p, qi, kj: (kj[p], 0, 0, 0)),
        ],
        out_specs=pl.BlockSpec((tq, ATT_INNER), lambda p, qi, kj: (qi[p], 0)),
        scratch_shapes=[pltpu.VMEM((seq, tq), jnp.int32),
                        pltpu.VMEM((seq, tq), jnp.bfloat16),
                        pltpu.VMEM((2, IDX_HEADS, DSA_CHUNK, tq), jnp.float32),
                        pltpu.VMEM((8, tq), jnp.int32),
                        pltpu.VMEM((ATT_HEADS, tq), jnp.float32),
                        pltpu.VMEM((ATT_HEADS, tq), jnp.float32),
                        pltpu.VMEM((ATT_HEADS, ATT_HEAD_DIM, tq), jnp.float32),
                        pltpu.VMEM((ATT_HEADS, tk, tq), jnp.float32)],
    )
    return pl.pallas_call(
        functools.partial(_dsa_body, top_k=top_k, idx_bits=idx_bits),
        out_shape=jax.ShapeDtypeStruct((seq, ATT_INNER), MXU_DTYPE),
        grid_spec=grid_spec,
        compiler_params=_params(("arbitrary",)),
        name="dsa",
    )(qi_idx, kj_idx, qit, lohi, ki, qt, k, vt)


def _even_out_body(h_ref, y_ref, o_ref, wy_ref, wo_ref, out_ref):
    out_ref[...] = h_ref[...] + _dot(y_ref[...], wy_ref[...]) + _dot(o_ref[...], wo_ref[...])


def _even_out(h, y, o, wy, wo):
    seq = h.shape[0]
    tm = min(ROW_TILE, seq)
    row = pl.BlockSpec((tm, D_MODEL), lambda i: (i, 0))
    return pl.pallas_call(
        _even_out_body,
        out_shape=jax.ShapeDtypeStruct((seq, D_MODEL), jnp.float32),
        grid=(seq // tm,),
        in_specs=[row, row, row, _const_spec((SSD_INNER, D_MODEL)), _const_spec((ATT_INNER, D_MODEL))],
        out_specs=row,
        compiler_params=_params(("parallel",)),
        name="even_out",
    )(h, y, o, wy, wo)


def _odd_body(h_ref, g_ref, win_ref, lng_ref, lnb_ref, ws_ref, bs_ref, wout_ref, o_ref, gated_scr):
    h = h_ref[...]
    tm = h.shape[0]
    xn = _rms(h, g_ref[...]).astype(MXU_DTYPE)
    u = jax.nn.gelu(_dot(xn, win_ref[:, :SG_INNER]))
    v = jax.nn.gelu(_dot(xn, win_ref[:, SG_INNER:]))
    mu = jnp.mean(v, axis=-1, keepdims=True)
    vc = v - mu
    var = jnp.mean(vc * vc, axis=-1, keepdims=True)
    v = (vc * lax.rsqrt(var + EPS) * lng_ref[...] + lnb_ref[...]).astype(MXU_DTYPE)
    ri = lax.broadcasted_iota(jnp.int32, (SG_CHUNK, SG_CHUNK), 0)
    ci = lax.broadcasted_iota(jnp.int32, (SG_CHUNK, SG_CHUNK), 1)
    gw = SG_INNER // SG_GROUPS
    for g in range(SG_GROUPS):
        w = jnp.where(ri >= ci, ws_ref[g], 0.0).astype(MXU_DTYPE)
        gs = slice(g * gw, (g + 1) * gw)
        for c in range(tm // SG_CHUNK):
            rs = slice(c * SG_CHUNK, (c + 1) * SG_CHUNK)
            mixed = _dot(w, v[rs, gs]) + bs_ref[:, gs]
            gated_scr[rs, gs] = (u[rs, gs] * mixed).astype(gated_scr.dtype)
    o_ref[...] = h + _dot(gated_scr[...], wout_ref[...])


def _odd(h, g, win, ln_g, ln_b, w_s, bs_full, wout):
    seq = h.shape[0]
    tm = min(ROW_TILE, seq)
    row = pl.BlockSpec((tm, D_MODEL), lambda i: (i, 0))
    return pl.pallas_call(
        _odd_body,
        out_shape=jax.ShapeDtypeStruct((seq, D_MODEL), jnp.float32),
        grid=(seq // tm,),
        in_specs=[row, _const_spec((1, D_MODEL)), _const_spec((D_MODEL, 2 * SG_INNER)),
                  _const_spec((1, SG_INNER)), _const_spec((1, SG_INNER)),
                  _const_spec((SG_GROUPS, SG_CHUNK, SG_CHUNK)), _const_spec((SG_CHUNK, SG_INNER)),
                  _const_spec((SG_INNER, D_MODEL))],
        out_specs=row,
        scratch_shapes=[pltpu.VMEM((tm, SG_INNER), MXU_DTYPE)],
        compiler_params=_params(("parallel",)),
        name="odd",
    )(h, g, win, ln_g, ln_b, w_s, bs_full, wout)


def _rope_inputs(seq):
    def tables(rot_dim):
        inv = 1.0 / (ROPE_THETA ** (jnp.arange(0, rot_dim, 2, dtype=jnp.float32) / rot_dim))
        ang = jnp.arange(seq, dtype=jnp.float32)[:, None] * inv[None, :]
        return jnp.cos(ang), jnp.sin(ang)

    def lane_tables(cos, sin):
        half = cos.shape[1]
        pad = LANES - 2 * half
        c = jnp.concatenate([cos, cos, jnp.ones((seq, pad), jnp.float32)], axis=1)
        s_lo = jnp.concatenate([-sin, jnp.zeros((seq, LANES - half), jnp.float32)], axis=1)
        s_hi = jnp.concatenate([jnp.zeros((seq, half), jnp.float32), sin,
                                jnp.zeros((seq, pad), jnp.float32)], axis=1)
        return c, s_lo, s_hi

    cos_a, sin_a = tables(ATT_HEAD_DIM // ROPE_FRACTION)
    cos_i, sin_i = tables(IDX_HEAD_DIM // ROPE_FRACTION)
    return (*lane_tables(cos_a, sin_a), *lane_tables(cos_i, sin_i), cos_a.T, sin_a.T, cos_i.T, sin_i.T)


def _even_weights(w_in):
    offs = np.cumsum((SSD_INNER, SSD_CONV_DIM, SSD_HEADS, ATT_INNER, ATT_INNER, ATT_INNER,
                      IDX_INNER, IDX_HEAD_DIM, IDX_HEADS))[:-1].tolist()
    z, xbc, dt, q, k, v, qi, ki, wi = jnp.split(w_in, offs, axis=-1)
    pad = jnp.zeros((D_MODEL, LANES - IDX_HEAD_DIM - SSD_HEADS), w_in.dtype)
    wn = jnp.concatenate([z, xbc, k, ki, dt, pad], axis=1).astype(MXU_DTYPE)
    wi_pad = jnp.zeros((D_MODEL, _T_DT - _T_WI - IDX_HEADS), w_in.dtype)
    wt = jnp.concatenate([q, v, qi, wi, wi_pad, dt], axis=1).T.astype(MXU_DTYPE)
    assert wn.shape == (D_MODEL, _N_END) and wt.shape == (_T_END, D_MODEL)
    return wn, wt


def _even_mixer(h, g, w_in, conv_w, conv_b, dt_bias, a_log, d_skip, gate_norm, w_out, rope):
    wn, wt = _even_weights(w_in)
    z, xbc, small, k, ki, qt, vt, qit, lohi, dtt = _even_in(h, g, wn, wt, rope)
    y = _ssd(xbc, small, dtt, z, conv_w, conv_b, dt_bias, a_log, d_skip, gate_norm)
    o = _dsa(qit, lohi, ki, qt, k, vt)
    w_out = w_out.astype(MXU_DTYPE)
    return _even_out(h, y, o, w_out[:SSD_INNER], w_out[SSD_INNER:])


def _forward(x, norm_g, final_g, ffn_w_gu, ffn_w_down, ev_w_in, ev_conv_w, ev_conv_b, ev_dt_bias,
             ev_a_log, ev_d, ev_gate_norm, ev_w_out, od_w_in, od_ln_g, od_ln_b, od_w_s, od_b_s, od_w_out):
    bsz, seq, _ = x.shape
    depth = norm_g.shape[0]
    rope = _rope_inputs(seq)
    wg = ffn_w_gu[..., :FFN_HIDDEN].astype(MXU_DTYPE)
    wu = ffn_w_gu[..., FFN_HIDDEN:].astype(MXU_DTYPE)
    wd = ffn_w_down.astype(MXU_DTYPE)
    fg = final_g[None, :]
    outs = []
    for b in range(bsz):
        h = x[b]
        for layer in range(depth):
            j = layer // 2
            h = _ffn(h, norm_g[layer, 0][None, :], wg[layer, 0], wu[layer, 0], wd[layer, 0], fg, False)
            g1 = norm_g[layer, 1][None, :]
            if layer % 2 == 0:
                h = _even_mixer(h, g1, ev_w_in[j], ev_conv_w[j], ev_conv_b[j], ev_dt_bias[j], ev_a_log[j],
                                ev_d[j], ev_gate_norm[j], ev_w_out[j], rope)
            else:
                bs_full = jnp.repeat(od_b_s[j].T, SG_INNER // SG_GROUPS, axis=1)
                h = _odd(h, g1, od_w_in[j].astype(MXU_DTYPE), od_ln_g[j][None, :], od_ln_b[j][None, :],
                         od_w_s[j], bs_full, od_w_out[j].astype(MXU_DTYPE))
            h = _ffn(h, norm_g[layer, 2][None, :], wg[layer, 1], wu[layer, 1], wd[layer, 1], fg,
                     layer == depth - 1)
        outs.append(h)
    return jnp.stack(outs, axis=0)


def kernel(x, norm_g, final_g, ffn_w_gu, ffn_w_down, ev_w_in, ev_conv_w, ev_conv_b, ev_dt_bias, ev_a_log,
           ev_d, ev_gate_norm, ev_w_out, od_w_in, od_ln_g, od_ln_b, od_w_s, od_b_s, od_w_out):
    return _forward(x, norm_g, final_g, ffn_w_gu, ffn_w_down, ev_w_in, ev_conv_w, ev_conv_b, ev_dt_bias,
                    ev_a_log, ev_d, ev_gate_norm, ev_w_out, od_w_in, od_ln_g, od_ln_b, od_w_s, od_b_s,
                    od_w_out)
```

```python
import functools
import math

import numpy as np
import jax
import jax.numpy as jnp
from jax import lax
from jax.experimental import pallas as pl
from jax.experimental.pallas import tpu as pltpu

D_MODEL = 1024
SSD_HEADS = 16
SSD_HEAD_DIM = 64
SSD_INNER = SSD_HEADS * SSD_HEAD_DIM
SSD_GROUPS = 2
SSD_STATE = 128
SSD_CONV = 4
SSD_CHUNK = 256
SSD_CONV_DIM = SSD_INNER + 2 * SSD_GROUPS * SSD_STATE
ATT_HEADS = 8
ATT_HEAD_DIM = 128
ATT_INNER = ATT_HEADS * ATT_HEAD_DIM
IDX_HEADS = 8
IDX_HEAD_DIM = 64
IDX_INNER = IDX_HEADS * IDX_HEAD_DIM
TOPK_MAX = 256
SG_CHUNK = 128
SG_GROUPS = 8
SG_INNER = 2 * D_MODEL
FFN_HIDDEN = 2816
ROPE_THETA = 500000.0
ROPE_FRACTION = 4
EPS = 1e-6

ATT_ROT_HALF = ATT_HEAD_DIM // ROPE_FRACTION // 2
IDX_ROT_HALF = IDX_HEAD_DIM // ROPE_FRACTION // 2

MXU_DTYPE = jnp.bfloat16
LANES = 128
VMEM_LIMIT_BYTES = 56 * 1024 * 1024

ROW_TILE = 512
FFN_HIDDEN_CHUNK = FFN_HIDDEN // 2
DSA_Q_TILE = 256
DSA_K_TILE = 1024
DSA_CHUNK = 256
LEAD_MASK = -(2 ** 16)
QUIET_NAN_BITS = 0x7FC00000
FINITE_KEY = 0x7F7FFFFF
MIN_NORMAL_EXP = 0x00800000
MIN_NORMAL_CODE = MIN_NORMAL_EXP >> 16
INT_MIN = -(2 ** 31)
NEG_BIG = -0.7 * float(np.finfo(np.float32).max)

_N_Z = 0
_N_XBC = _N_Z + SSD_INNER
_N_K = _N_XBC + SSD_CONV_DIM
_N_SMALL = _N_K + ATT_INNER
_N_END = _N_SMALL + LANES
_SM_KI = 0
_SM_DT = IDX_HEAD_DIM
_T_Q = 0
_T_V = _T_Q + ATT_INNER
_T_QI = _T_V + ATT_INNER
_T_WI = _T_QI + IDX_INNER
_T_DT = _T_WI + 16
_T_END = _T_DT + SSD_HEADS


def _dot(a, b):
    return jnp.dot(a, b, preferred_element_type=jnp.float32)


def _rms(x, g):
    return x * lax.rsqrt(jnp.mean(x * x, axis=-1, keepdims=True) + EPS) * g


def _softplus(x):
    return jnp.maximum(x, 0.0) + jnp.log1p(jnp.exp(-jnp.abs(x)))


def _split3(x):
    hi = x.astype(MXU_DTYPE)
    r = x - hi.astype(jnp.float32)
    mid = r.astype(MXU_DTYPE)
    lo = (r - mid.astype(jnp.float32)).astype(MXU_DTYPE)
    return hi, mid, lo


def _const_spec(shape):
    zeros = (0,) * len(shape)
    return pl.BlockSpec(shape, lambda *_: zeros, pipeline_mode=pl.Buffered(1))


def _params(sem):
    return pltpu.CompilerParams(dimension_semantics=sem, vmem_limit_bytes=VMEM_LIMIT_BYTES)


def _ffn_body(h_ref, g_ref, wg_ref, wu_ref, wd_ref, fg_ref, o_ref, *, final):
    h = h_ref[...]
    xn = _rms(h, g_ref[...]).astype(MXU_DTYPE)
    acc = None
    for c in range(FFN_HIDDEN // FFN_HIDDEN_CHUNK):
        sl = slice(c * FFN_HIDDEN_CHUNK, (c + 1) * FFN_HIDDEN_CHUNK)
        gate = _dot(xn, wg_ref[:, sl])
        up = _dot(xn, wu_ref[:, sl])
        act = (gate * jax.nn.sigmoid(gate) * up).astype(MXU_DTYPE)
        part = _dot(act, wd_ref[sl, :])
        acc = part if acc is None else acc + part
    out = h + 0.5 * acc
    if final:
        out = _rms(out, fg_ref[...])
    o_ref[...] = out


def _ffn(h, g, wg, wu, wd, final_g, final):
    seq = h.shape[0]
    tm = min(ROW_TILE, seq)
    row = pl.BlockSpec((tm, D_MODEL), lambda i: (i, 0))
    return pl.pallas_call(
        functools.partial(_ffn_body, final=final),
        out_shape=jax.ShapeDtypeStruct((seq, D_MODEL), jnp.float32),
        grid=(seq // tm,),
        in_specs=[row, _const_spec((1, D_MODEL)), _const_spec((D_MODEL, FFN_HIDDEN)),
                  _const_spec((D_MODEL, FFN_HIDDEN)), _const_spec((FFN_HIDDEN, D_MODEL)),
                  _const_spec((1, D_MODEL))],
        out_specs=row,
        compiler_params=_params(("parallel",)),
        name="ffn",
    )(h, g, wg, wu, wd, final_g)


def _rope_lanes(x, c, s_lo, s_hi, half):
    return x * c + pltpu.roll(x, LANES - half, 1) * s_lo + pltpu.roll(x, half, 1) * s_hi


def _rope_rows(x, c, s, half):
    x1, x2 = x[:half], x[half:2 * half]
    return jnp.concatenate([x1 * c - x2 * s, x1 * s + x2 * c, x[2 * half:]], axis=0)


def _even_in_body(h_ref, g_ref, wn_ref, wt_ref, ca_ref, sal_ref, sah_ref, ci_ref, sil_ref, sih_ref,
                  cat_ref, sat_ref, cit_ref, sit_ref,
                  z_ref, xbc_ref, small_ref, k_ref, ki_ref, qt_ref, vt_ref, qit_ref, lohi_ref, dtt_ref):
    xn = _rms(h_ref[...], g_ref[...])
    xnb = xn.astype(MXU_DTYPE)
    xnt = xn.T.astype(MXU_DTYPE)

    z_ref[...] = _dot(xnb, wn_ref[:, _N_Z:_N_XBC])
    xbc_ref[...] = _dot(xnb, wn_ref[:, _N_XBC:_N_K])
    kk = _dot(xnb, wn_ref[:, _N_K:_N_SMALL])
    ca, sal, sah = ca_ref[...], sal_ref[...], sah_ref[...]
    for hd in range(ATT_HEADS):
        sl = slice(hd * ATT_HEAD_DIM, (hd + 1) * ATT_HEAD_DIM)
        k_ref[hd] = _rope_lanes(kk[:, sl], ca, sal, sah, ATT_ROT_HALF).astype(k_ref.dtype)
    sm = _dot(xnb, wn_ref[:, _N_SMALL:_N_END])
    small_ref[...] = sm
    smr = _rope_lanes(sm, ci_ref[...], sil_ref[...], sih_ref[...], IDX_ROT_HALF)
    ki_ref[...] = smr[:, _SM_KI:_SM_KI + IDX_HEAD_DIM].astype(ki_ref.dtype)

    att_scale = ATT_HEAD_DIM ** -0.5 * math.log2(math.e)
    qt = _dot(wt_ref[_T_Q:_T_V, :], xnt)
    cat, sat = cat_ref[...], sat_ref[...]
    def put(ref, where, val):
        tb = ref.shape[-1]
        for b in range(ref.shape[0]):
            ref[(b, *where, slice(None))] = val[:, b * tb:(b + 1) * tb].astype(ref.dtype)

    vt = _dot(wt_ref[_T_V:_T_QI, :], xnt)
    for hd in range(ATT_HEADS):
        sl = slice(hd * ATT_HEAD_DIM, (hd + 1) * ATT_HEAD_DIM)
        put(qt_ref, (hd, slice(None)), _rope_rows(qt[sl], cat, sat, ATT_ROT_HALF) * att_scale)
        put(vt_ref, (hd, slice(None)), vt[sl])
    tail = _dot(wt_ref[_T_WI:_T_END, :], xnt)
    wit = tail[0:IDX_HEADS] * (IDX_HEADS ** -0.5 * IDX_HEAD_DIM ** -0.5)
    dtt_ref[...] = tail[_T_DT - _T_WI:_T_END - _T_WI]
    pos = wit > 0.0
    put(lohi_ref, (slice(0, 8),), jnp.where(pos, 0.0, -jnp.inf))
    put(lohi_ref, (slice(8, 16),), jnp.where(pos, jnp.inf, 0.0))
    qit = _dot(wt_ref[_T_QI:_T_WI, :], xnt)
    cit, sit = cit_ref[...], sit_ref[...]
    for hd in range(IDX_HEADS):
        sl = slice(hd * IDX_HEAD_DIM, (hd + 1) * IDX_HEAD_DIM)
        put(qit_ref, (sl,), _rope_rows(qit[sl], cit, sit, IDX_ROT_HALF) * wit[hd:hd + 1])


def _even_in(h, g, wn, wt, rope):
    seq = h.shape[0]
    tm = min(ROW_TILE, seq)
    f32 = jnp.float32

    def rows(width):
        return pl.BlockSpec((tm, width), lambda i: (i, 0))

    def cols(height):
        return pl.BlockSpec((height, tm), lambda i: (0, i))

    def blocks(height, tb):
        return pl.BlockSpec((tm // tb, height, tb), lambda i: (i, 0, 0))

    def head_blocks(tb):
        if tb <= tm:
            return pl.BlockSpec((tm // tb, ATT_HEADS, ATT_HEAD_DIM, tb), lambda i: (i, 0, 0, 0))
        per = tb // tm
        return pl.BlockSpec((1, ATT_HEADS, ATT_HEAD_DIM, tm), lambda i: (i // per, 0, 0, i % per))

    assert (tm % DSA_K_TILE == 0 or DSA_K_TILE % tm == 0) and tm % DSA_Q_TILE == 0
    nq, nk = seq // DSA_Q_TILE, seq // DSA_K_TILE
    out_shape = (
        jax.ShapeDtypeStruct((seq, SSD_INNER), f32),
        jax.ShapeDtypeStruct((seq, SSD_CONV_DIM), f32),
        jax.ShapeDtypeStruct((seq, LANES), f32),
        jax.ShapeDtypeStruct((ATT_HEADS, seq, ATT_HEAD_DIM), MXU_DTYPE),
        jax.ShapeDtypeStruct((seq, IDX_HEAD_DIM), MXU_DTYPE),
        jax.ShapeDtypeStruct((nq, ATT_HEADS, ATT_HEAD_DIM, DSA_Q_TILE), MXU_DTYPE),
        jax.ShapeDtypeStruct((nk, ATT_HEADS, ATT_HEAD_DIM, DSA_K_TILE), MXU_DTYPE),
        jax.ShapeDtypeStruct((nq, IDX_INNER, DSA_Q_TILE), MXU_DTYPE),
        jax.ShapeDtypeStruct((nq, 16, DSA_Q_TILE), f32),
        jax.ShapeDtypeStruct((SSD_HEADS, seq), f32),
    )
    out_specs = (rows(SSD_INNER), rows(SSD_CONV_DIM), rows(LANES),
                 pl.BlockSpec((ATT_HEADS, tm, ATT_HEAD_DIM), lambda i: (0, i, 0)), rows(IDX_HEAD_DIM),
                 head_blocks(DSA_Q_TILE), head_blocks(DSA_K_TILE), blocks(IDX_INNER, DSA_Q_TILE),
                 blocks(16, DSA_Q_TILE), cols(SSD_HEADS))
    in_specs = [rows(D_MODEL), _const_spec((1, D_MODEL)), _const_spec(wn.shape), _const_spec(wt.shape)]
    in_specs += [rows(LANES)] * 6
    in_specs += [cols(ATT_ROT_HALF)] * 2 + [cols(IDX_ROT_HALF)] * 2
    return pl.pallas_call(
        _even_in_body,
        out_shape=out_shape,
        grid=(seq // tm,),
        in_specs=in_specs,
        out_specs=out_specs,
        compiler_params=_params(("parallel",)),
        name="even_in",
    )(h, g, wn, wt, *rope)


def _ssd_body(xbc_ref, small_ref, dtt_ref, z_ref, cw_ref, cb_ref, dtb_ref, dtbt_ref, a_ref, at_ref,
              drep_ref, gn_ref, o_ref, tail_scr, st_scr, y_scr, xdec_scr, decay_scr):
    q = SSD_CHUNK
    f32 = jnp.float32

    @pl.when(pl.program_id(0) == 0)
    def _():
        tail_scr[...] = jnp.zeros_like(tail_scr)
        st_scr[...] = jnp.zeros_like(st_scr)

    x = xbc_ref[...]
    tail = tail_scr[...]
    row8 = lax.broadcasted_iota(jnp.int32, (8, SSD_CONV_DIM), 0)
    conv = x * cw_ref[SSD_CONV - 1:SSD_CONV, :] + cb_ref[...]
    for shift in range(1, SSD_CONV):
        rolled = pltpu.roll(x, shift, 0)
        head = jnp.where(row8 < shift, pltpu.roll(tail, shift, 0), rolled[0:8])
        shifted = jnp.concatenate([head, rolled[8:]], axis=0)
        conv = conv + shifted * cw_ref[SSD_CONV - 1 - shift:SSD_CONV - shift, :]
    tail_scr[...] = x[q - 8:q]
    xbc = conv * jax.nn.sigmoid(conv)
    xs = xbc[:, :SSD_INNER]
    bm = xbc[:, SSD_INNER:SSD_INNER + SSD_GROUPS * SSD_STATE]
    cm = xbc[:, SSD_INNER + SSD_GROUPS * SSD_STATE:]

    dt_col = _softplus(small_ref[:, _SM_DT:_SM_DT + SSD_HEADS] + dtb_ref[...])
    dt_row = _softplus(dtt_ref[...] + dtbt_ref[...])
    adt_col = dt_col * (-jnp.exp(a_ref[...]))
    adt_row = dt_row * (-jnp.exp(at_ref[...]))
    ri = lax.broadcasted_iota(jnp.int32, (q, q), 0)
    ci = lax.broadcasted_iota(jnp.int32, (q, q), 1)
    causal = ri >= ci
    tril = jnp.where(causal, 1.0, 0.0).astype(MXU_DTYPE)
    triu = jnp.where(ri <= ci, 1.0, 0.0).astype(MXU_DTYPE)
    cs_col = sum(_dot(tril, p) for p in _split3(adt_col))
    cs_row = sum(_dot(p, triu) for p in _split3(adt_row))

    cmb = cm.astype(MXU_DTYPE)
    bmb = bm.astype(MXU_DTYPE)
    for g in range(SSD_GROUPS):
        gs = slice(g * SSD_STATE, (g + 1) * SSD_STATE)
        cb = lax.dot_general(cmb[:, gs], bmb[:, gs], (((1,), (1,)), ((), ())),
                             preferred_element_type=f32)
        kpg = SSD_HEADS // SSD_GROUPS
        for hd in range(g * kpg, (g + 1) * kpg):
            hs = slice(hd * SSD_HEAD_DIM, (hd + 1) * SSD_HEAD_DIM)
            col = cs_col[:, hd:hd + 1]
            row = cs_row[hd:hd + 1, :]
            last = cs_row[hd:hd + 1, q - 1:q]
            decay = jnp.exp(jnp.where(causal, col - row, -jnp.inf))
            xdt = xs[:, hs] * dt_col[:, hd:hd + 1]
            y = _dot((cb * decay).astype(MXU_DTYPE), xdt.astype(MXU_DTYPE))
            y_off = _dot(cmb[:, gs], st_scr[:, hs].astype(MXU_DTYPE))
            y_scr[:, hs] = y + y_off * jnp.exp(col)
            xdec_scr[:, hs] = (xdt * jnp.exp(last - col)).astype(xdec_scr.dtype)
            decay_scr[:, hs] = jnp.broadcast_to(jnp.exp(last), (1, SSD_HEAD_DIM))
        ws = slice(g * kpg * SSD_HEAD_DIM, (g + 1) * kpg * SSD_HEAD_DIM)
        bmt = bm[:, gs].T.astype(MXU_DTYPE)
        st_scr[:, ws] = st_scr[:, ws] * decay_scr[:, ws] + _dot(bmt, xdec_scr[:, ws])

    z = z_ref[...]
    y = (y_scr[...] + drep_ref[...] * xs) * (z * jax.nn.sigmoid(z))
    gw = SSD_INNER // SSD_GROUPS
    for g in range(SSD_GROUPS):
        sl = slice(g * gw, (g + 1) * gw)
        seg = y[:, sl]
        seg = seg * lax.rsqrt(jnp.mean(seg * seg, axis=-1, keepdims=True) + EPS)
        o_ref[:, sl] = (seg * gn_ref[:, sl]).astype(o_ref.dtype)


def _ssd(xbc, small, dtt, z, conv_w, conv_b, dt_bias, a_log, d_skip, gate_norm):
    seq = xbc.shape[0]
    q = SSD_CHUNK
    f32 = jnp.float32

    def rows(width):
        return pl.BlockSpec((q, width), lambda i: (i, 0))

    drep = jnp.repeat(d_skip, SSD_HEAD_DIM)[None, :]
    return pl.pallas_call(
        _ssd_body,
        out_shape=jax.ShapeDtypeStruct((seq, SSD_INNER), MXU_DTYPE),
        grid=(seq // q,),
        in_specs=[rows(SSD_CONV_DIM), rows(LANES), pl.BlockSpec((SSD_HEADS, q), lambda i: (0, i)),
                  rows(SSD_INNER), _const_spec((SSD_CONV, SSD_CONV_DIM)), _const_spec((1, SSD_CONV_DIM)),
                  _const_spec((1, SSD_HEADS)), _const_spec((SSD_HEADS, 1)),
                  _const_spec((1, SSD_HEADS)), _const_spec((SSD_HEADS, 1)),
                  _const_spec((1, SSD_INNER)), _const_spec((1, SSD_INNER))],
        out_specs=rows(SSD_INNER),
        scratch_shapes=[pltpu.VMEM((8, SSD_CONV_DIM), f32),
                        pltpu.VMEM((SSD_STATE, SSD_INNER), f32),
                        pltpu.VMEM((q, SSD_INNER), f32),
                        pltpu.VMEM((q, SSD_INNER), MXU_DTYPE),
                        pltpu.VMEM((1, SSD_INNER), f32)],
        compiler_params=_params(("arbitrary",)),
        name="ssd",
    )(xbc, small, dtt, z, conv_w, conv_b[None, :], dt_bias[None, :], dt_bias[:, None],
      a_log[None, :], a_log[:, None], drep, gate_norm[None, :])


def _dsa_body(qi_ref, kj_ref, qit_ref, lohi_ref, ki_ref, qt_ref, k_ref, vt_ref, o_ref,
              keys_scr, lead_scr, y_scr, thr_scr, m_scr, l_scr, acc_scr, logit_scr, *, top_k, idx_bits):
    tq, tk, ck = DSA_Q_TILE, DSA_K_TILE, DSA_CHUNK
    i32 = jnp.int32
    p = pl.program_id(0)
    qi = qi_ref[p]
    kj = kj_ref[p]

    @pl.when(kj == 0)
    def _select():
        chunks_per_tile = tk // ck
        n_tiles = ((qi + 1) * tq + tk - 1) // tk
        n_chunks = n_tiles * chunks_per_tile
        lo = lohi_ref[0:8, :]
        hi = lohi_ref[8:16, :]
        rows = 4 * 8
        t_pos = qi * tq + lax.broadcasted_iota(i32, (rows, tq), 1)
        s_off = lax.broadcasted_iota(i32, (ck, tq), 0)
        s_rows = lax.broadcasted_iota(i32, (rows, tq), 0)

        def project(c, slot):
            kic = ki_ref[pl.ds(pl.multiple_of(c * ck, ck), ck), :]
            for hd in range(IDX_HEADS):
                y_scr[slot, hd] = _dot(kic, qit_ref[hd * IDX_HEAD_DIM:(hd + 1) * IDX_HEAD_DIM, :])

        def emit_keys(c, slot, diagonal):
            for g in range(ck // rows):
                r0 = pl.multiple_of(c * ck + g * rows, rows)
                gs = slice(g * rows, (g + 1) * rows)
                sc = None
                for hd in range(IDX_HEADS):
                    term = jnp.minimum(jnp.maximum(y_scr[slot, hd, gs, :], lo[hd:hd + 1]), hi[hd:hd + 1])
                    sc = term if sc is None else sc + term
                bits = pltpu.bitcast(sc, i32)
                bits = jnp.where((bits & 0x7FFFFFFF) < MIN_NORMAL_EXP, 0, bits)
                key = bits ^ ((bits >> 31) & 0x7FFFFFFF)
                lead = bits & LEAD_MASK
                if diagonal:
                    causal = r0 + s_rows <= t_pos
                    key = jnp.where(causal, key, INT_MIN)
                    lead = jnp.where(causal, lead, QUIET_NAN_BITS)
                keys_scr[pl.ds(r0, rows), :] = key
                lead_scr[pl.ds(r0, rows), :] = pltpu.bitcast(lead, jnp.float32).astype(jnp.bfloat16)

        assert chunks_per_tile % 2 == 0

        def score_tile(t, diagonal):
            for j in range(chunks_per_tile):
                c = t * chunks_per_tile + j
                emit_keys(c, j % 2, diagonal)
                if not (diagonal and j == chunks_per_tile - 1):
                    project(c + 1, (j + 1) % 2)

        project(0, 0)
        lax.fori_loop(0, n_tiles - 1, lambda t, carry: (score_tile(t, False), carry)[1], 0)
        score_tile(n_tiles - 1, True)

        partial_rows = 4 * 8
        s_tile = lax.broadcasted_iota(i32, (partial_rows, tq), 0)

        def key_groups(c):
            for g in range(tk // partial_rows):
                r0 = pl.multiple_of(c * tk + g * partial_rows, partial_rows)
                yield keys_scr[pl.ds(r0, partial_rows), :], r0

        def count(pred):
            def body(c, acc):
                for kk, r0 in key_groups(c):
                    acc = acc + jnp.where(pred(kk, r0), 1.0, 0.0)
                return acc
            acc = lax.fori_loop(0, n_tiles, body, jnp.zeros((partial_rows, tq), jnp.float32))
            return acc.sum(axis=0, keepdims=True)

        def count_lead(cand_code):
            cand_code = jnp.where((cand_code > 0) & (cand_code < MIN_NORMAL_CODE), MIN_NORMAL_CODE,
                                  jnp.where((cand_code < 0) & (cand_code >= -MIN_NORMAL_CODE), 0, cand_code))
            cand_bits = (cand_code ^ ((cand_code >> 31) & 0x7FFF)) << 16
            cand = jnp.broadcast_to(pltpu.bitcast(cand_bits, jnp.float32), (16, tq)).astype(jnp.bfloat16)
            one, zero = jnp.ones((), jnp.bfloat16), jnp.zeros((), jnp.bfloat16)

            def body(c, acc):
                parts = []
                for g in range(tk // 128):
                    r0 = pl.multiple_of(c * tk + g * 128, 128)
                    lead = lead_scr[pl.ds(r0, 128), :].reshape(8, 16, tq)
                    parts += [jnp.where(lead[j] >= cand, one, zero) for j in range(8)]
                while len(parts) > 1:
                    parts = [a + b for a, b in zip(parts[0::2], parts[1::2])]
                return acc + parts[0].astype(jnp.float32)
            acc = lax.fori_loop(0, n_tiles, body, jnp.zeros((16, tq), jnp.float32))
            return acc.sum(axis=0, keepdims=True)

        k_f = jnp.float32(top_k)
        n_t = (qi * tq + 1 + lax.broadcasted_iota(i32, (1, tq), 1)).astype(jnp.float32)

        def lead_step(b, carry):
            code, c_lo, c_hi = carry
            cand = code + lax.shift_left(i32(1), i32(15) - b)
            cnt = count_lead(cand)
            ok = cnt >= k_f
            return jnp.where(ok, cand, code), jnp.where(ok, cnt, c_lo), jnp.where(ok, c_hi, cnt)

        code, c_lo, c_hi = lax.fori_loop(
            0, 16, lead_step, (jnp.full((1, tq), -(2 ** 15), i32), n_t, jnp.zeros((1, tq), jnp.float32)))

        lo_k = jnp.maximum(code << 16, INT_MIN + 1)
        hi_k = jnp.where(code == 2 ** 15 - 1, jnp.iinfo(i32).max, (code + 1) << 16)

        def finished(lo_k, hi_k, c_lo):
            return (c_lo <= k_f) | (lo_k + 1 >= hi_k)

        def key_value(kk):
            kk = jnp.clip(kk, -FINITE_KEY, FINITE_KEY)
            return pltpu.bitcast(kk ^ ((kk >> 31) & 0x7FFFFFFF), jnp.float32)

        def tighten(lo_k, hi_k, c_lo, c_hi, done, it):
            def body(c, carry):
                mn, mx = carry
                for kk, _ in key_groups(c):
                    mn = jnp.minimum(mn, jnp.where(kk >= lo_k, kk, jnp.iinfo(i32).max))
                    mx = jnp.maximum(mx, jnp.where(kk < hi_k, kk, INT_MIN))
                return mn, mx
            mn, mx = lax.fori_loop(0, n_tiles, body, (jnp.full((partial_rows, tq), jnp.iinfo(i32).max, i32),
                                                      jnp.full((partial_rows, tq), INT_MIN, i32)))
            mn, mx = mn.min(axis=0, keepdims=True), mx.max(axis=0, keepdims=True)
            return jnp.where(done, lo_k, mn), jnp.where(done, hi_k, mx + 1), c_lo, c_hi

        def probe(lo_k, hi_k, c_lo, c_hi, done, it):
            below, span = c_lo - k_f, c_lo - c_hi
            end = jnp.where(below + below >= span, hi_k - 1, lo_k + 1)
            lo_v, hi_v = key_value(lo_k), key_value(hi_k)
            guess_v = lo_v + (below + 0.5) / jnp.maximum(span, 1.0) * (hi_v - lo_v)
            guess_b = pltpu.bitcast(guess_v, i32)
            guess = guess_b ^ ((guess_b >> 31) & 0x7FFFFFFF)
            mid = (lo_k >> 1) + (hi_k >> 1) + (lo_k & hi_k & 1)
            phase = it % 4
            cand = jnp.where(phase == 1, end, jnp.where(phase == 3, mid, guess))
            cand = jnp.where(done, lo_k, jnp.maximum(lo_k + 1, jnp.minimum(hi_k - 1, cand)))
            cnt = count(lambda kk, r0: kk >= cand)
            up = jnp.logical_and(~done, cnt >= k_f)
            down = jnp.logical_and(~done, cnt < k_f)
            return (jnp.where(up, cand, lo_k), jnp.where(down, cand, hi_k),
                    jnp.where(up, cnt, c_lo), jnp.where(down, cnt, c_hi))

        def search_cond(carry):
            return jnp.logical_and(carry[0] < 4 * 32, carry[1] > 0)

        def search_step(carry):
            it, _, lo_k, hi_k, c_lo, c_hi = carry
            done = finished(lo_k, hi_k, c_lo)
            lo_k, hi_k, c_lo, c_hi = lax.cond(it % 4 == 0, tighten, probe, lo_k, hi_k, c_lo, c_hi, done, it)
            still = jnp.where(finished(lo_k, hi_k, c_lo), 0.0, 1.0)
            return it + 1, jnp.max(still).astype(i32), lo_k, hi_k, c_lo, c_hi

        active = jnp.max(jnp.where(finished(lo_k, hi_k, c_lo), 0.0, 1.0)).astype(i32)
        _, _, thr, _, n_ge, _ = lax.while_loop(search_cond, search_step, (i32(0), active, lo_k, hi_k, c_lo, c_hi))
        thr_scr[0:1, :] = thr

        @pl.when(jnp.max(n_ge) > top_k)
        def _ties():
            need = top_k - count(lambda kk, r0: kk > thr)

            def idx_step(b, cut):
                cand = cut + lax.shift_left(i32(1), i32(idx_bits - 1) - b)
                cnt = count(lambda kk, r0: jnp.where(kk == thr, r0 + s_tile, cand) < cand)
                return jnp.where(cnt < need, cand, cut)

            cut = lax.fori_loop(0, idx_bits, idx_step, jnp.zeros((1, tq), i32))
            cut = jnp.where(n_ge > top_k, cut, jnp.iinfo(i32).max)

            def drop(c, carry):
                r0 = pl.multiple_of(c * ck, ck)
                kk = keys_scr[pl.ds(r0, ck), :]
                excess = jnp.where(kk == thr, r0 + s_off, INT_MIN) > cut
                keys_scr[pl.ds(r0, ck), :] = jnp.where(excess, INT_MIN, kk)
                return carry

            lax.fori_loop(0, n_chunks, drop, 0)

        m_scr[...] = jnp.full_like(m_scr, NEG_BIG)
        l_scr[...] = jnp.zeros_like(l_scr)
        acc_scr[...] = jnp.zeros_like(acc_scr)

    sel = keys_scr[pl.ds(pl.multiple_of(kj * tk, tk), tk), :] >= thr_scr[0:1, :]
    bias = jnp.where(sel, 0.0, -jnp.inf)
    col_max = []
    for hd in range(ATT_HEADS):
        logit = _dot(k_ref[hd], qt_ref[hd]) + bias
        logit_scr[hd] = logit
        col_max.append(jnp.max(logit, axis=0, keepdims=True))
    m_old = m_scr[...]
    m_new = jnp.maximum(m_old, jnp.concatenate(col_max, axis=0))
    alpha = jnp.exp2(m_old - m_new)
    m_scr[...] = m_new
    col_sum = []
    for hd in range(ATT_HEADS):
        prob = jnp.exp2(logit_scr[hd] - m_new[hd:hd + 1])
        col_sum.append(jnp.sum(prob, axis=0, keepdims=True))
        acc_scr[hd] = alpha[hd:hd + 1] * acc_scr[hd] + _dot(vt_ref[hd], prob.astype(MXU_DTYPE))
    l_scr[...] = alpha * l_scr[...] + jnp.concatenate(col_sum, axis=0)

    @pl.when(kj == ((qi + 1) * tq - 1) // tk)
    def _finish():
        for hd in range(ATT_HEADS):
            hs = slice(hd * ATT_HEAD_DIM, (hd + 1) * ATT_HEAD_DIM)
            o_ref[:, hs] = (acc_scr[hd] / l_scr[hd:hd + 1, :]).T.astype(o_ref.dtype)


def _dsa(qit, lohi, ki, qt, k, vt):
    seq = k.shape[1]
    tq, tk = DSA_Q_TILE, DSA_K_TILE
    assert seq % tk == 0 and tk % tq == 0 and tq % DSA_CHUNK == 0
    assert k.shape == (ATT_HEADS, seq, ATT_HEAD_DIM)
    assert qt.shape == (seq // tq, ATT_HEADS, ATT_HEAD_DIM, tq) and vt.shape == (seq // tk, ATT_HEADS, ATT_HEAD_DIM, tk)
    nq = seq // tq
    pairs = [(i, j) for i in range(nq) for j in range(((i + 1) * tq - 1) // tk + 1)]
    qi_idx = jnp.asarray([pq for pq, _ in pairs], jnp.int32)
    kj_idx = jnp.asarray([pk for _, pk in pairs], jnp.int32)
    top_k = min(TOPK_MAX, seq // 4)
    idx_bits = max(1, (seq - 1).bit_length())
    grid_spec = pltpu.PrefetchScalarGridSpec(
        num_scalar_prefetch=2,
        grid=(len(pairs),),
        in_specs=[
            pl.BlockSpec((None, IDX_INNER, tq), lambda p, qi, kj: (qi[p], 0, 0)),
            pl.BlockSpec((None, 16, tq), lambda p, qi, kj: (qi[p], 0, 0)),
            pl.BlockSpec((seq, IDX_HEAD_DIM), lambda p, qi, kj: (0, 0), pipeline_mode=pl.Buffered(1)),
            pl.BlockSpec((None, ATT_HEADS, ATT_HEAD_DIM, tq), lambda p, qi, kj: (qi[p], 0, 0, 0)),
            pl.BlockSpec((ATT_HEADS, tk, ATT_HEAD_DIM), lambda p, qi, kj: (0, kj[p], 0)),
            pl.BlockSpec((None, ATT_HEADS, ATT_HEAD_DIM, tk), lambda p, qi, kj: (kj[p], 0, 0, 0)),
        ],
        out_specs=pl.BlockSpec((tq, ATT_INNER), lambda p, qi, kj: (qi[p], 0)),
        scratch_shapes=[pltpu.VMEM((seq, tq), jnp.int32),
                        pltpu.VMEM((seq, tq), jnp.bfloat16),
                        pltpu.VMEM((2, IDX_HEADS, DSA_CHUNK, tq), jnp.float32),
                        pltpu.VMEM((8, tq), jnp.int32),
                        pltpu.VMEM((ATT_HEADS, tq), jnp.float32),
                        pltpu.VMEM((ATT_HEADS, tq), jnp.float32),
                        pltpu.VMEM((ATT_HEADS, ATT_HEAD_DIM, tq), jnp.float32),
                        pltpu.VMEM((ATT_HEADS, tk, tq), jnp.float32)],
    )
    return pl.pallas_call(
        functools.partial(_dsa_body, top_k=top_k, idx_bits=idx_bits),
        out_shape=jax.ShapeDtypeStruct((seq, ATT_INNER), MXU_DTYPE),
        grid_spec=grid_spec,
        compiler_params=_params(("arbitrary",)),
        name="dsa",
    )(qi_idx, kj_idx, qit, lohi, ki, qt, k, vt)


def _even_out_body(h_ref, y_ref, o_ref, wy_ref, wo_ref, out_ref):
    out_ref[...] = h_ref[...] + _dot(y_ref[...], wy_ref[...]) + _dot(o_ref[...], wo_ref[...])


def _even_out(h, y, o, wy, wo):
    seq = h.shape[0]
    tm = min(ROW_TILE, seq)
    row = pl.BlockSpec((tm, D_MODEL), lambda i: (i, 0))
    return pl.pallas_call(
        _even_out_body,
        out_shape=jax.ShapeDtypeStruct((seq, D_MODEL), jnp.float32),
        grid=(seq // tm,),
        in_specs=[row, row, row, _const_spec((SSD_INNER, D_MODEL)), _const_spec((ATT_INNER, D_MODEL))],
        out_specs=row,
        compiler_params=_params(("parallel",)),
        name="even_out",
    )(h, y, o, wy, wo)


def _odd_body(h_ref, g_ref, win_ref, lng_ref, lnb_ref, ws_ref, bs_ref, wout_ref, o_ref, gated_scr):
    h = h_ref[...]
    tm = h.shape[0]
    xn = _rms(h, g_ref[...]).astype(MXU_DTYPE)
    u = jax.nn.gelu(_dot(xn, win_ref[:, :SG_INNER]))
    v = jax.nn.gelu(_dot(xn, win_ref[:, SG_INNER:]))
    mu = jnp.mean(v, axis=-1, keepdims=True)
    vc = v - mu
    var = jnp.mean(vc * vc, axis=-1, keepdims=True)
    v = (vc * lax.rsqrt(var + EPS) * lng_ref[...] + lnb_ref[...]).astype(MXU_DTYPE)
    ri = lax.broadcasted_iota(jnp.int32, (SG_CHUNK, SG_CHUNK), 0)
    ci = lax.broadcasted_iota(jnp.int32, (SG_CHUNK, SG_CHUNK), 1)
    gw = SG_INNER // SG_GROUPS
    for g in range(SG_GROUPS):
        w = jnp.where(ri >= ci, ws_ref[g], 0.0).astype(MXU_DTYPE)
        gs = slice(g * gw, (g + 1) * gw)
        for c in range(tm // SG_CHUNK):
            rs = slice(c * SG_CHUNK, (c + 1) * SG_CHUNK)
            mixed = _dot(w, v[rs, gs]) + bs_ref[:, gs]
            gated_scr[rs, gs] = (u[rs, gs] * mixed).astype(gated_scr.dtype)
    o_ref[...] = h + _dot(gated_scr[...], wout_ref[...])


def _odd(h, g, win, ln_g, ln_b, w_s, bs_full, wout):
    seq = h.shape[0]
    tm = min(ROW_TILE, seq)
    row = pl.BlockSpec((tm, D_MODEL), lambda i: (i, 0))
    return pl.pallas_call(
        _odd_body,
        out_shape=jax.ShapeDtypeStruct((seq, D_MODEL), jnp.float32),
        grid=(seq // tm,),
        in_specs=[row, _const_spec((1, D_MODEL)), _const_spec((D_MODEL, 2 * SG_INNER)),
                  _const_spec((1, SG_INNER)), _const_spec((1, SG_INNER)),
                  _const_spec((SG_GROUPS, SG_CHUNK, SG_CHUNK)), _const_spec((SG_CHUNK, SG_INNER)),
                  _const_spec((SG_INNER, D_MODEL))],
        out_specs=row,
        scratch_shapes=[pltpu.VMEM((tm, SG_INNER), MXU_DTYPE)],
        compiler_params=_params(("parallel",)),
        name="odd",
    )(h, g, win, ln_g, ln_b, w_s, bs_full, wout)


def _rope_inputs(seq):
    def tables(rot_dim):
        inv = 1.0 / (ROPE_THETA ** (jnp.arange(0, rot_dim, 2, dtype=jnp.float32) / rot_dim))
        ang = jnp.arange(seq, dtype=jnp.float32)[:, None] * inv[None, :]
        return jnp.cos(ang), jnp.sin(ang)

    def lane_tables(cos, sin):
        half = cos.shape[1]
        pad = LANES - 2 * half
        c = jnp.concatenate([cos, cos, jnp.ones((seq, pad), jnp.float32)], axis=1)
        s_lo = jnp.concatenate([-sin, jnp.zeros((seq, LANES - half), jnp.float32)], axis=1)
        s_hi = jnp.concatenate([jnp.zeros((seq, half), jnp.float32), sin,
                                jnp.zeros((seq, pad), jnp.float32)], axis=1)
        return c, s_lo, s_hi

    cos_a, sin_a = tables(ATT_HEAD_DIM // ROPE_FRACTION)
    cos_i, sin_i = tables(IDX_HEAD_DIM // ROPE_FRACTION)
    return (*lane_tables(cos_a, sin_a), *lane_tables(cos_i, sin_i), cos_a.T, sin_a.T, cos_i.T, sin_i.T)


def _even_weights(w_in):
    offs = np.cumsum((SSD_INNER, SSD_CONV_DIM, SSD_HEADS, ATT_INNER, ATT_INNER, ATT_INNER,
                      IDX_INNER, IDX_HEAD_DIM, IDX_HEADS))[:-1].tolist()
    z, xbc, dt, q, k, v, qi, ki, wi = jnp.split(w_in, offs, axis=-1)
    pad = jnp.zeros((D_MODEL, LANES - IDX_HEAD_DIM - SSD_HEADS), w_in.dtype)
    wn = jnp.concatenate([z, xbc, k, ki, dt, pad], axis=1).astype(MXU_DTYPE)
    wi_pad = jnp.zeros((D_MODEL, _T_DT - _T_WI - IDX_HEADS), w_in.dtype)
    wt = jnp.concatenate([q, v, qi, wi, wi_pad, dt], axis=1).T.astype(MXU_DTYPE)
    assert wn.shape == (D_MODEL, _N_END) and wt.shape == (_T_END, D_MODEL)
    return wn, wt


def _even_mixer(h, g, w_in, conv_w, conv_b, dt_bias, a_log, d_skip, gate_norm, w_out, rope):
    wn, wt = _even_weights(w_in)
    z, xbc, small, k, ki, qt, vt, qit, lohi, dtt = _even_in(h, g, wn, wt, rope)
    y = _ssd(xbc, small, dtt, z, conv_w, conv_b, dt_bias, a_log, d_skip, gate_norm)
    o = _dsa(qit, lohi, ki, qt, k, vt)
    w_out = w_out.astype(MXU_DTYPE)
    return _even_out(h, y, o, w_out[:SSD_INNER], w_out[SSD_INNER:])


def _forward(x, norm_g, final_g, ffn_w_gu, ffn_w_down, ev_w_in, ev_conv_w, ev_conv_b, ev_dt_bias,
             ev_a_log, ev_d, ev_gate_norm, ev_w_out, od_w_in, od_ln_g, od_ln_b, od_w_s, od_b_s, od_w_out):
    bsz, seq, _ = x.shape
    depth = norm_g.shape[0]
    rope = _rope_inputs(seq)
    wg = ffn_w_gu[..., :FFN_HIDDEN].astype(MXU_DTYPE)
    wu = ffn_w_gu[..., FFN_HIDDEN:].astype(MXU_DTYPE)
    wd = ffn_w_down.astype(MXU_DTYPE)
    fg = final_g[None, :]
    outs = []
    for b in range(bsz):
        h = x[b]
        for layer in range(depth):
            j = layer // 2
            h = _ffn(h, norm_g[layer, 0][None, :], wg[layer, 0], wu[layer, 0], wd[layer, 0], fg, False)
            g1 = norm_g[layer, 1][None, :]
            if layer % 2 == 0:
                h = _even_mixer(h, g1, ev_w_in[j], ev_conv_w[j], ev_conv_b[j], ev_dt_bias[j], ev_a_log[j],
                                ev_d[j], ev_gate_norm[j], ev_w_out[j], rope)
            else:
                bs_full = jnp.repeat(od_b_s[j].T, SG_INNER // SG_GROUPS, axis=1)
                h = _odd(h, g1, od_w_in[j].astype(MXU_DTYPE), od_ln_g[j][None, :], od_ln_b[j][None, :],
                         od_w_s[j], bs_full, od_w_out[j].astype(MXU_DTYPE))
            h = _ffn(h, norm_g[layer, 2][None, :], wg[layer, 1], wu[layer, 1], wd[layer, 1], fg,
                     layer == depth - 1)
        outs.append(h)
    return jnp.stack(outs, axis=0)


def kernel(x, norm_g, final_g, ffn_w_gu, ffn_w_down, ev_w_in, ev_conv_w, ev_conv_b, ev_dt_bias, ev_a_log,
           ev_d, ev_gate_norm, ev_w_out, od_w_in, od_ln_g, od_ln_b, od_w_s, od_b_s, od_w_out):
    return _forward(x, norm_g, final_g, ffn_w_gu, ffn_w_down, ev_w_in, ev_conv_w, ev_conv_b, ev_dt_bias,
                    ev_a_log, ev_d, ev_gate_norm, ev_w_out, od_w_in, od_ln_g, od_ln_b, od_w_s, od_b_s,
                    od_w_out)
```

```python
import functools
import math

import numpy as np
import jax
import jax.numpy as jnp
from jax import lax
from jax.experimental import pallas as pl
from jax.experimental.pallas import tpu as pltpu

D_MODEL = 1024
SSD_HEADS = 16
SSD_HEAD_DIM = 64
SSD_INNER = SSD_HEADS * SSD_HEAD_DIM
SSD_GROUPS = 2
SSD_STATE = 128
SSD_CONV = 4
SSD_CHUNK = 256
SSD_CONV_DIM = SSD_INNER + 2 * SSD_GROUPS * SSD_STATE
ATT_HEADS = 8
ATT_HEAD_DIM = 128
ATT_INNER = ATT_HEADS * ATT_HEAD_DIM
IDX_HEADS = 8
IDX_HEAD_DIM = 64
IDX_INNER = IDX_HEADS * IDX_HEAD_DIM
TOPK_MAX = 256
SG_CHUNK = 128
SG_GROUPS = 8
SG_INNER = 2 * D_MODEL
FFN_HIDDEN = 2816
ROPE_THETA = 500000.0
ROPE_FRACTION = 4
EPS = 1e-6

ATT_ROT_HALF = ATT_HEAD_DIM // ROPE_FRACTION // 2
IDX_ROT_HALF = IDX_HEAD_DIM // ROPE_FRACTION // 2

MXU_DTYPE = jnp.bfloat16
LANES = 128
VMEM_LIMIT_BYTES = 56 * 1024 * 1024

ROW_TILE = 512
FFN_HIDDEN_CHUNK = FFN_HIDDEN // 2
DSA_Q_TILE = 256
DSA_K_TILE = 512
DSA_CHUNK = 256
TIE_EXTRACT_MAX = 8
LEAD_MASK = -(2 ** 16)
QUIET_NAN_BITS = 0x7FC00000
FINITE_KEY = 0x7F7FFFFF
MIN_NORMAL_EXP = 0x00800000
MIN_NORMAL_CODE = MIN_NORMAL_EXP >> 16
INT_MIN = -(2 ** 31)
NEG_BIG = -0.7 * float(np.finfo(np.float32).max)

_N_Z = 0
_N_XBC = _N_Z + SSD_INNER
_N_K = _N_XBC + SSD_CONV_DIM
_N_SMALL = _N_K + ATT_INNER
_N_END = _N_SMALL + LANES
_SM_KI = 0
_SM_DT = IDX_HEAD_DIM
_T_Q = 0
_T_V = _T_Q + ATT_INNER
_T_QI = _T_V + ATT_INNER
_T_WI = _T_QI + IDX_INNER
_T_DT = _T_WI + 16
_T_END = _T_DT + SSD_HEADS


def _dot(a, b):
    return jnp.dot(a, b, preferred_element_type=jnp.float32)


def _rms(x, g):
    return x * lax.rsqrt(jnp.mean(x * x, axis=-1, keepdims=True) + EPS) * g


def _softplus(x):
    return jnp.maximum(x, 0.0) + jnp.log1p(jnp.exp(-jnp.abs(x)))


def _split3(x):
    hi = x.astype(MXU_DTYPE)
    r = x - hi.astype(jnp.float32)
    mid = r.astype(MXU_DTYPE)
    lo = (r - mid.astype(jnp.float32)).astype(MXU_DTYPE)
    return hi, mid, lo


def _const_spec(shape):
    zeros = (0,) * len(shape)
    return pl.BlockSpec(shape, lambda *_: zeros, pipeline_mode=pl.Buffered(1))


def _params(sem):
    return pltpu.CompilerParams(dimension_semantics=sem, vmem_limit_bytes=VMEM_LIMIT_BYTES)


def _ffn_body(h_ref, g_ref, wg_ref, wu_ref, wd_ref, fg_ref, o_ref, *, final):
    h = h_ref[...]
    xn = _rms(h, g_ref[...]).astype(MXU_DTYPE)
    acc = None
    for c in range(FFN_HIDDEN // FFN_HIDDEN_CHUNK):
        sl = slice(c * FFN_HIDDEN_CHUNK, (c + 1) * FFN_HIDDEN_CHUNK)
        gate = _dot(xn, wg_ref[:, sl])
        up = _dot(xn, wu_ref[:, sl])
        act = (gate * jax.nn.sigmoid(gate) * up).astype(MXU_DTYPE)
        part = _dot(act, wd_ref[sl, :])
        acc = part if acc is None else acc + part
    out = h + 0.5 * acc
    if final:
        out = _rms(out, fg_ref[...])
    o_ref[...] = out


def _ffn(h, g, wg, wu, wd, final_g, final):
    seq = h.shape[0]
    tm = min(ROW_TILE, seq)
    row = pl.BlockSpec((tm, D_MODEL), lambda i: (i, 0))
    return pl.pallas_call(
        functools.partial(_ffn_body, final=final),
        out_shape=jax.ShapeDtypeStruct((seq, D_MODEL), jnp.float32),
        grid=(seq // tm,),
        in_specs=[row, _const_spec((1, D_MODEL)), _const_spec((D_MODEL, FFN_HIDDEN)),
                  _const_spec((D_MODEL, FFN_HIDDEN)), _const_spec((FFN_HIDDEN, D_MODEL)),
                  _const_spec((1, D_MODEL))],
        out_specs=row,
        compiler_params=_params(("parallel",)),
        name="ffn",
    )(h, g, wg, wu, wd, final_g)


def _rope_lanes(x, c, s_lo, s_hi, half):
    return x * c + pltpu.roll(x, LANES - half, 1) * s_lo + pltpu.roll(x, half, 1) * s_hi


def _rope_rows(x, c, s, half):
    x1, x2 = x[:half], x[half:2 * half]
    return jnp.concatenate([x1 * c - x2 * s, x1 * s + x2 * c, x[2 * half:]], axis=0)


def _even_in_body(h_ref, g_ref, wn_ref, wt_ref, ca_ref, sal_ref, sah_ref, ci_ref, sil_ref, sih_ref,
                  cat_ref, sat_ref, cit_ref, sit_ref,
                  z_ref, xbc_ref, small_ref, k_ref, ki_ref, qt_ref, vt_ref, qit_ref, lohi_ref, dtt_ref):
    xn = _rms(h_ref[...], g_ref[...])
    xnb = xn.astype(MXU_DTYPE)
    xnt = xn.T.astype(MXU_DTYPE)

    z_ref[...] = _dot(xnb, wn_ref[:, _N_Z:_N_XBC])
    xbc_ref[...] = _dot(xnb, wn_ref[:, _N_XBC:_N_K])
    kk = _dot(xnb, wn_ref[:, _N_K:_N_SMALL])
    ca, sal, sah = ca_ref[...], sal_ref[...], sah_ref[...]
    for hd in range(ATT_HEADS):
        sl = slice(hd * ATT_HEAD_DIM, (hd + 1) * ATT_HEAD_DIM)
        k_ref[hd] = _rope_lanes(kk[:, sl], ca, sal, sah, ATT_ROT_HALF).astype(k_ref.dtype)
    sm = _dot(xnb, wn_ref[:, _N_SMALL:_N_END])
    small_ref[...] = sm
    smr = _rope_lanes(sm, ci_ref[...], sil_ref[...], sih_ref[...], IDX_ROT_HALF)
    ki_ref[...] = smr[:, _SM_KI:_SM_KI + IDX_HEAD_DIM].astype(ki_ref.dtype)

    att_scale = ATT_HEAD_DIM ** -0.5 * math.log2(math.e)
    qt = _dot(wt_ref[_T_Q:_T_V, :], xnt)
    cat, sat = cat_ref[...], sat_ref[...]
    def put(ref, where, val):
        tb = ref.shape[-1]
        for b in range(ref.shape[0]):
            ref[(b, *where, slice(None))] = val[:, b * tb:(b + 1) * tb].astype(ref.dtype)

    vt = _dot(wt_ref[_T_V:_T_QI, :], xnt)
    for hd in range(ATT_HEADS):
        sl = slice(hd * ATT_HEAD_DIM, (hd + 1) * ATT_HEAD_DIM)
        put(qt_ref, (hd, slice(None)), _rope_rows(qt[sl], cat, sat, ATT_ROT_HALF) * att_scale)
        put(vt_ref, (hd, slice(None)), vt[sl])
    tail = _dot(wt_ref[_T_WI:_T_END, :], xnt)
    wit = tail[0:IDX_HEADS] * (IDX_HEADS ** -0.5 * IDX_HEAD_DIM ** -0.5)
    dtt_ref[...] = tail[_T_DT - _T_WI:_T_END - _T_WI]
    pos = wit > 0.0
    put(lohi_ref, (slice(0, 8),), jnp.where(pos, 0.0, -jnp.inf))
    put(lohi_ref, (slice(8, 16),), jnp.where(pos, jnp.inf, 0.0))
    qit = _dot(wt_ref[_T_QI:_T_WI, :], xnt)
    cit, sit = cit_ref[...], sit_ref[...]
    for hd in range(IDX_HEADS):
        sl = slice(hd * IDX_HEAD_DIM, (hd + 1) * IDX_HEAD_DIM)
        put(qit_ref, (sl,), _rope_rows(qit[sl], cit, sit, IDX_ROT_HALF) * wit[hd:hd + 1])


def _even_in(h, g, wn, wt, rope):
    seq = h.shape[0]
    tm = min(ROW_TILE, seq)
    f32 = jnp.float32

    def rows(width):
        return pl.BlockSpec((tm, width), lambda i: (i, 0))

    def cols(height):
        return pl.BlockSpec((height, tm), lambda i: (0, i))

    def blocks(height, tb):
        return pl.BlockSpec((tm // tb, height, tb), lambda i: (i, 0, 0))

    def head_blocks(tb):
        return pl.BlockSpec((tm // tb, ATT_HEADS, ATT_HEAD_DIM, tb), lambda i: (i, 0, 0, 0))

    assert tm % DSA_K_TILE == 0 and tm % DSA_Q_TILE == 0
    nq, nk = seq // DSA_Q_TILE, seq // DSA_K_TILE
    out_shape = (
        jax.ShapeDtypeStruct((seq, SSD_INNER), f32),
        jax.ShapeDtypeStruct((seq, SSD_CONV_DIM), f32),
        jax.ShapeDtypeStruct((seq, LANES), f32),
        jax.ShapeDtypeStruct((ATT_HEADS, seq, ATT_HEAD_DIM), MXU_DTYPE),
        jax.ShapeDtypeStruct((seq, IDX_HEAD_DIM), MXU_DTYPE),
        jax.ShapeDtypeStruct((nq, ATT_HEADS, ATT_HEAD_DIM, DSA_Q_TILE), MXU_DTYPE),
        jax.ShapeDtypeStruct((nk, ATT_HEADS, ATT_HEAD_DIM, DSA_K_TILE), MXU_DTYPE),
        jax.ShapeDtypeStruct((nq, IDX_INNER, DSA_Q_TILE), MXU_DTYPE),
        jax.ShapeDtypeStruct((nq, 16, DSA_Q_TILE), f32),
        jax.ShapeDtypeStruct((SSD_HEADS, seq), f32),
    )
    out_specs = (rows(SSD_INNER), rows(SSD_CONV_DIM), rows(LANES),
                 pl.BlockSpec((ATT_HEADS, tm, ATT_HEAD_DIM), lambda i: (0, i, 0)), rows(IDX_HEAD_DIM),
                 head_blocks(DSA_Q_TILE), head_blocks(DSA_K_TILE), blocks(IDX_INNER, DSA_Q_TILE),
                 blocks(16, DSA_Q_TILE), cols(SSD_HEADS))
    in_specs = [rows(D_MODEL), _const_spec((1, D_MODEL)), _const_spec(wn.shape), _const_spec(wt.shape)]
    in_specs += [rows(LANES)] * 6
    in_specs += [cols(ATT_ROT_HALF)] * 2 + [cols(IDX_ROT_HALF)] * 2
    return pl.pallas_call(
        _even_in_body,
        out_shape=out_shape,
        grid=(seq // tm,),
        in_specs=in_specs,
        out_specs=out_specs,
        compiler_params=_params(("parallel",)),
        name="even_in",
    )(h, g, wn, wt, *rope)


def _ssd_body(xbc_ref, small_ref, dtt_ref, z_ref, cw_ref, cb_ref, dtb_ref, dtbt_ref, a_ref, at_ref,
              drep_ref, gn_ref, o_ref, tail_scr, st_scr, y_scr, xdec_scr, decay_scr):
    q = SSD_CHUNK
    f32 = jnp.float32

    @pl.when(pl.program_id(0) == 0)
    def _():
        tail_scr[...] = jnp.zeros_like(tail_scr)
        st_scr[...] = jnp.zeros_like(st_scr)

    x = xbc_ref[...]
    tail = tail_scr[...]
    row8 = lax.broadcasted_iota(jnp.int32, (8, SSD_CONV_DIM), 0)
    conv = x * cw_ref[SSD_CONV - 1:SSD_CONV, :] + cb_ref[...]
    for shift in range(1, SSD_CONV):
        rolled = pltpu.roll(x, shift, 0)
        head = jnp.where(row8 < shift, pltpu.roll(tail, shift, 0), rolled[0:8])
        shifted = jnp.concatenate([head, rolled[8:]], axis=0)
        conv = conv + shifted * cw_ref[SSD_CONV - 1 - shift:SSD_CONV - shift, :]
    tail_scr[...] = x[q - 8:q]
    xbc = conv * jax.nn.sigmoid(conv)
    xs = xbc[:, :SSD_INNER]
    bm = xbc[:, SSD_INNER:SSD_INNER + SSD_GROUPS * SSD_STATE]
    cm = xbc[:, SSD_INNER + SSD_GROUPS * SSD_STATE:]

    dt_col = _softplus(small_ref[:, _SM_DT:_SM_DT + SSD_HEADS] + dtb_ref[...])
    dt_row = _softplus(dtt_ref[...] + dtbt_ref[...])
    adt_col = dt_col * (-jnp.exp(a_ref[...]))
    adt_row = dt_row * (-jnp.exp(at_ref[...]))
    ri = lax.broadcasted_iota(jnp.int32, (q, q), 0)
    ci = lax.broadcasted_iota(jnp.int32, (q, q), 1)
    causal = ri >= ci
    tril = jnp.where(causal, 1.0, 0.0).astype(MXU_DTYPE)
    triu = jnp.where(ri <= ci, 1.0, 0.0).astype(MXU_DTYPE)
    cs_col = sum(_dot(tril, p) for p in _split3(adt_col))
    cs_row = sum(_dot(p, triu) for p in _split3(adt_row))

    cmb = cm.astype(MXU_DTYPE)
    bmb = bm.astype(MXU_DTYPE)
    for g in range(SSD_GROUPS):
        gs = slice(g * SSD_STATE, (g + 1) * SSD_STATE)
        cb = lax.dot_general(cmb[:, gs], bmb[:, gs], (((1,), (1,)), ((), ())),
                             preferred_element_type=f32)
        kpg = SSD_HEADS // SSD_GROUPS
        for hd in range(g * kpg, (g + 1) * kpg):
            hs = slice(hd * SSD_HEAD_DIM, (hd + 1) * SSD_HEAD_DIM)
            col = cs_col[:, hd:hd + 1]
            row = cs_row[hd:hd + 1, :]
            last = cs_row[hd:hd + 1, q - 1:q]
            decay = jnp.exp(jnp.where(causal, col - row, -jnp.inf))
            xdt = xs[:, hs] * dt_col[:, hd:hd + 1]
            y = _dot((cb * decay).astype(MXU_DTYPE), xdt.astype(MXU_DTYPE))
            y_off = _dot(cmb[:, gs], st_scr[:, hs].astype(MXU_DTYPE))
            y_scr[:, hs] = y + y_off * jnp.exp(col)
            xdec_scr[:, hs] = (xdt * jnp.exp(last - col)).astype(xdec_scr.dtype)
            decay_scr[:, hs] = jnp.broadcast_to(jnp.exp(last), (1, SSD_HEAD_DIM))
        ws = slice(g * kpg * SSD_HEAD_DIM, (g + 1) * kpg * SSD_HEAD_DIM)
        bmt = bm[:, gs].T.astype(MXU_DTYPE)
        st_scr[:, ws] = st_scr[:, ws] * decay_scr[:, ws] + _dot(bmt, xdec_scr[:, ws])

    z = z_ref[...]
    y = (y_scr[...] + drep_ref[...] * xs) * (z * jax.nn.sigmoid(z))
    gw = SSD_INNER // SSD_GROUPS
    for g in range(SSD_GROUPS):
        sl = slice(g * gw, (g + 1) * gw)
        seg = y[:, sl]
        seg = seg * lax.rsqrt(jnp.mean(seg * seg, axis=-1, keepdims=True) + EPS)
        o_ref[:, sl] = (seg * gn_ref[:, sl]).astype(o_ref.dtype)


def _ssd(xbc, small, dtt, z, conv_w, conv_b, dt_bias, a_log, d_skip, gate_norm):
    seq = xbc.shape[0]
    q = SSD_CHUNK
    f32 = jnp.float32

    def rows(width):
        return pl.BlockSpec((q, width), lambda i: (i, 0))

    drep = jnp.repeat(d_skip, SSD_HEAD_DIM)[None, :]
    return pl.pallas_call(
        _ssd_body,
        out_shape=jax.ShapeDtypeStruct((seq, SSD_INNER), MXU_DTYPE),
        grid=(seq // q,),
        in_specs=[rows(SSD_CONV_DIM), rows(LANES), pl.BlockSpec((SSD_HEADS, q), lambda i: (0, i)),
                  rows(SSD_INNER), _const_spec((SSD_CONV, SSD_CONV_DIM)), _const_spec((1, SSD_CONV_DIM)),
                  _const_spec((1, SSD_HEADS)), _const_spec((SSD_HEADS, 1)),
                  _const_spec((1, SSD_HEADS)), _const_spec((SSD_HEADS, 1)),
                  _const_spec((1, SSD_INNER)), _const_spec((1, SSD_INNER))],
        out_specs=rows(SSD_INNER),
        scratch_shapes=[pltpu.VMEM((8, SSD_CONV_DIM), f32),
                        pltpu.VMEM((SSD_STATE, SSD_INNER), f32),
                        pltpu.VMEM((q, SSD_INNER), f32),
                        pltpu.VMEM((q, SSD_INNER), MXU_DTYPE),
                        pltpu.VMEM((1, SSD_INNER), f32)],
        compiler_params=_params(("arbitrary",)),
        name="ssd",
    )(xbc, small, dtt, z, conv_w, conv_b[None, :], dt_bias[None, :], dt_bias[:, None],
      a_log[None, :], a_log[:, None], drep, gate_norm[None, :])


def _dsa_body(qi_ref, kj_ref, qit_ref, lohi_ref, ki_ref, qt_ref, k_ref, vt_ref, o_ref,
              keys_scr, lead_scr, y_scr, thr_scr, m_scr, l_scr, acc_scr, logit_scr, *, top_k, idx_bits):
    tq, tk, ck = DSA_Q_TILE, DSA_K_TILE, DSA_CHUNK
    i32 = jnp.int32
    p = pl.program_id(0)
    qi = qi_ref[p]
    kj = kj_ref[p]

    @pl.when(kj == 0)
    def _select():
        chunks_per_tile = tk // ck
        n_tiles = ((qi + 1) * tq + tk - 1) // tk
        n_chunks = n_tiles * chunks_per_tile
        lo = lohi_ref[0:8, :]
        hi = lohi_ref[8:16, :]
        rows = 4 * 8
        t_pos = qi * tq + lax.broadcasted_iota(i32, (rows, tq), 1)
        s_off = lax.broadcasted_iota(i32, (ck, tq), 0)
        s_rows = lax.broadcasted_iota(i32, (rows, tq), 0)

        def project(c, slot):
            kic = ki_ref[pl.ds(pl.multiple_of(c * ck, ck), ck), :]
            for hd in range(IDX_HEADS):
                y_scr[slot, hd] = _dot(kic, qit_ref[hd * IDX_HEAD_DIM:(hd + 1) * IDX_HEAD_DIM, :])

        def emit_keys(c, slot, diagonal):
            for g in range(ck // rows):
                r0 = pl.multiple_of(c * ck + g * rows, rows)
                gs = slice(g * rows, (g + 1) * rows)
                sc = None
                for hd in range(IDX_HEADS):
                    term = jnp.minimum(jnp.maximum(y_scr[slot, hd, gs, :], lo[hd:hd + 1]), hi[hd:hd + 1])
                    sc = term if sc is None else sc + term
                bits = pltpu.bitcast(sc, i32)
                bits = jnp.where((bits & 0x7FFFFFFF) < MIN_NORMAL_EXP, 0, bits)
                key = bits ^ ((bits >> 31) & 0x7FFFFFFF)
                lead = bits & LEAD_MASK
                if diagonal:
                    causal = r0 + s_rows <= t_pos
                    key = jnp.where(causal, key, INT_MIN)
                    lead = jnp.where(causal, lead, QUIET_NAN_BITS)
                keys_scr[pl.ds(r0, rows), :] = key
                lead_scr[pl.ds(r0, rows), :] = pltpu.bitcast(lead, jnp.float32).astype(jnp.bfloat16)

        assert chunks_per_tile % 2 == 0

        def score_tile(t, diagonal):
            for j in range(chunks_per_tile):
                c = t * chunks_per_tile + j
                emit_keys(c, j % 2, diagonal)
                if not (diagonal and j == chunks_per_tile - 1):
                    project(c + 1, (j + 1) % 2)

        project(0, 0)
        lax.fori_loop(0, n_tiles - 1, lambda t, carry: (score_tile(t, False), carry)[1], 0)
        score_tile(n_tiles - 1, True)

        partial_rows = 4 * 8
        s_tile = lax.broadcasted_iota(i32, (partial_rows, tq), 0)

        def key_groups(c):
            for g in range(tk // partial_rows):
                r0 = pl.multiple_of(c * tk + g * partial_rows, partial_rows)
                yield keys_scr[pl.ds(r0, partial_rows), :], r0

        def count(pred):
            def body(c, acc):
                for kk, r0 in key_groups(c):
                    acc = acc + jnp.where(pred(kk, r0), 1.0, 0.0)
                return acc
            acc = lax.fori_loop(0, n_tiles, body, jnp.zeros((partial_rows, tq), jnp.float32))
            return acc.sum(axis=0, keepdims=True)

        def count_lead(cand_code):
            cand_code = jnp.where((cand_code > 0) & (cand_code < MIN_NORMAL_CODE), MIN_NORMAL_CODE,
                                  jnp.where((cand_code < 0) & (cand_code >= -MIN_NORMAL_CODE), 0, cand_code))
            cand_bits = (cand_code ^ ((cand_code >> 31) & 0x7FFF)) << 16
            cand = jnp.broadcast_to(pltpu.bitcast(cand_bits, jnp.float32), (16, tq)).astype(jnp.bfloat16)
            one, zero = jnp.ones((), jnp.bfloat16), jnp.zeros((), jnp.bfloat16)

            def body(c, acc):
                parts = []
                for g in range(tk // 128):
                    r0 = pl.multiple_of(c * tk + g * 128, 128)
                    lead = lead_scr[pl.ds(r0, 128), :].reshape(8, 16, tq)
                    parts += [jnp.where(lead[j] >= cand, one, zero) for j in range(8)]
                while len(parts) > 1:
                    parts = [a + b for a, b in zip(parts[0::2], parts[1::2])]
                return acc + parts[0].astype(jnp.float32)
            acc = lax.fori_loop(0, n_tiles, body, jnp.zeros((16, tq), jnp.float32))
            return acc.sum(axis=0, keepdims=True)

        k_f = jnp.float32(top_k)
        n_t = (qi * tq + 1 + lax.broadcasted_iota(i32, (1, tq), 1)).astype(jnp.float32)

        def lead_step(b, carry):
            code, c_lo, c_hi = carry
            cand = code + lax.shift_left(i32(1), i32(15) - b)
            cnt = count_lead(cand)
            ok = cnt >= k_f
            return jnp.where(ok, cand, code), jnp.where(ok, cnt, c_lo), jnp.where(ok, c_hi, cnt)

        code, c_lo, c_hi = lax.fori_loop(
            0, 16, lead_step, (jnp.full((1, tq), -(2 ** 15), i32), n_t, jnp.zeros((1, tq), jnp.float32)))

        lo_k = jnp.maximum(code << 16, INT_MIN + 1)
        hi_k = jnp.where(code == 2 ** 15 - 1, jnp.iinfo(i32).max, (code + 1) << 16)

        def finished(lo_k, hi_k, c_lo):
            return (c_lo <= k_f) | (lo_k + 1 >= hi_k)

        def key_value(kk):
            kk = jnp.clip(kk, -FINITE_KEY, FINITE_KEY)
            return pltpu.bitcast(kk ^ ((kk >> 31) & 0x7FFFFFFF), jnp.float32)

        def tighten(lo_k, hi_k, c_lo, c_hi, done, it):
            def body(c, carry):
                mn, mx = carry
                for kk, _ in key_groups(c):
                    mn = jnp.minimum(mn, jnp.where(kk >= lo_k, kk, jnp.iinfo(i32).max))
                    mx = jnp.maximum(mx, jnp.where(kk < hi_k, kk, INT_MIN))
                return mn, mx
            mn, mx = lax.fori_loop(0, n_tiles, body, (jnp.full((partial_rows, tq), jnp.iinfo(i32).max, i32),
                                                      jnp.full((partial_rows, tq), INT_MIN, i32)))
            mn, mx = mn.min(axis=0, keepdims=True), mx.max(axis=0, keepdims=True)
            return jnp.where(done, lo_k, mn), jnp.where(done, hi_k, mx + 1), c_lo, c_hi

        def probe(lo_k, hi_k, c_lo, c_hi, done, it):
            below, span = c_lo - k_f, c_lo - c_hi
            end = jnp.where(below + below >= span, hi_k - 1, lo_k + 1)
            lo_v, hi_v = key_value(lo_k), key_value(hi_k)
            guess_v = lo_v + (below + 0.5) / jnp.maximum(span, 1.0) * (hi_v - lo_v)
            guess_b = pltpu.bitcast(guess_v, i32)
            guess = guess_b ^ ((guess_b >> 31) & 0x7FFFFFFF)
            mid = (lo_k >> 1) + (hi_k >> 1) + (lo_k & hi_k & 1)
            phase = it % 4
            cand = jnp.where(phase == 1, end, jnp.where(phase == 3, mid, guess))
            cand = jnp.where(done, lo_k, jnp.maximum(lo_k + 1, jnp.minimum(hi_k - 1, cand)))
            cnt = count(lambda kk, r0: kk >= cand)
            up = jnp.logical_and(~done, cnt >= k_f)
            down = jnp.logical_and(~done, cnt < k_f)
            return (jnp.where(up, cand, lo_k), jnp.where(down, cand, hi_k),
                    jnp.where(up, cnt, c_lo), jnp.where(down, cnt, c_hi))

        def search_cond(carry):
            return jnp.logical_and(carry[0] < 4 * 32, carry[1] > 0)

        def search_step(carry):
            it, _, lo_k, hi_k, c_lo, c_hi = carry
            done = finished(lo_k, hi_k, c_lo)
            lo_k, hi_k, c_lo, c_hi = lax.cond(it % 4 == 0, tighten, probe, lo_k, hi_k, c_lo, c_hi, done, it)
            still = jnp.where(finished(lo_k, hi_k, c_lo), 0.0, 1.0)
            return it + 1, jnp.max(still).astype(i32), lo_k, hi_k, c_lo, c_hi

        active = jnp.max(jnp.where(finished(lo_k, hi_k, c_lo), 0.0, 1.0)).astype(i32)
        _, _, thr, _, n_ge, n_gt = lax.while_loop(search_cond, search_step, (i32(0), active, lo_k, hi_k, c_lo, c_hi))
        thr_scr[0:1, :] = thr

        tied = n_ge > k_f
        need = jnp.where(tied, k_f - n_gt, 0.0)
        max_need = jnp.max(need)

        @pl.when(max_need > 0.0)
        def _ties():
            int_max = jnp.iinfo(i32).max

            def by_extraction(_):
                def next_tie(j, cut):
                    def body(c, mn):
                        for kk, r0 in key_groups(c):
                            idx = r0 + s_tile
                            mn = jnp.minimum(mn, jnp.where((kk == thr) & (idx > cut), idx, int_max))
                        return mn
                    mn = lax.fori_loop(0, n_tiles, body, jnp.full((partial_rows, tq), int_max, i32))
                    return jnp.where(j.astype(jnp.float32) < need, mn.min(axis=0, keepdims=True), cut)
                return lax.fori_loop(0, max_need.astype(i32), next_tie, jnp.full((1, tq), -1, i32))

            def by_index_bits(_):
                def idx_step(b, cut):
                    cand = cut + lax.shift_left(i32(1), i32(idx_bits - 1) - b)
                    cnt = count(lambda kk, r0: jnp.where(kk == thr, r0 + s_tile, cand) < cand)
                    return jnp.where(cnt < need, cand, cut)
                return lax.fori_loop(0, idx_bits, idx_step, jnp.zeros((1, tq), i32))

            cut = lax.cond(max_need <= TIE_EXTRACT_MAX, by_extraction, by_index_bits, 0)
            cut = jnp.where(tied, cut, int_max)

            def drop(c, carry):
                r0 = pl.multiple_of(c * ck, ck)
                kk = keys_scr[pl.ds(r0, ck), :]
                excess = jnp.where(kk == thr, r0 + s_off, INT_MIN) > cut
                keys_scr[pl.ds(r0, ck), :] = jnp.where(excess, INT_MIN, kk)
                return carry

            lax.fori_loop(0, n_chunks, drop, 0)

        m_scr[...] = jnp.full_like(m_scr, NEG_BIG)
        l_scr[...] = jnp.zeros_like(l_scr)
        acc_scr[...] = jnp.zeros_like(acc_scr)

    sel = keys_scr[pl.ds(pl.multiple_of(kj * tk, tk), tk), :] >= thr_scr[0:1, :]
    bias = jnp.where(sel, 0.0, -jnp.inf)
    col_max = []
    for hd in range(ATT_HEADS):
        logit = _dot(k_ref[hd], qt_ref[hd]) + bias
        logit_scr[hd] = logit
        col_max.append(jnp.max(logit, axis=0, keepdims=True))
    m_old = m_scr[...]
    m_new = jnp.maximum(m_old, jnp.concatenate(col_max, axis=0))
    alpha = jnp.exp2(m_old - m_new)
    m_scr[...] = m_new
    col_sum = []
    for hd in range(ATT_HEADS):
        prob = jnp.exp2(logit_scr[hd] - m_new[hd:hd + 1])
        col_sum.append(jnp.sum(prob, axis=0, keepdims=True))
        acc_scr[hd] = alpha[hd:hd + 1] * acc_scr[hd] + _dot(vt_ref[hd], prob.astype(MXU_DTYPE))
    l_scr[...] = alpha * l_scr[...] + jnp.concatenate(col_sum, axis=0)

    @pl.when(kj == ((qi + 1) * tq - 1) // tk)
    def _finish():
        for hd in range(ATT_HEADS):
            hs = slice(hd * ATT_HEAD_DIM, (hd + 1) * ATT_HEAD_DIM)
            o_ref[:, hs] = (acc_scr[hd] / l_scr[hd:hd + 1, :]).T.astype(o_ref.dtype)


def _dsa(qit, lohi, ki, qt, k, vt):
    seq = k.shape[1]
    tq, tk = DSA_Q_TILE, DSA_K_TILE
    assert seq % tk == 0 and tk % tq == 0 and tq % DSA_CHUNK == 0
    assert k.shape == (ATT_HEADS, seq, ATT_HEAD_DIM)
    assert qt.shape == (seq // tq, ATT_HEADS, ATT_HEAD_DIM, tq) and vt.shape == (seq // tk, ATT_HEADS, ATT_HEAD_DIM, tk)
    nq = seq // tq
    pairs = [(i, j) for i in range(nq) for j in range(((i + 1) * tq - 1) // tk + 1)]
    qi_idx = jnp.asarray([pq for pq, _ in pairs], jnp.int32)
    kj_idx = jnp.asarray([pk for _, pk in pairs], jnp.int32)
    top_k = min(TOPK_MAX, seq // 4)
    idx_bits = max(1, (seq - 1).bit_length())
    grid_spec = pltpu.PrefetchScalarGridSpec(
        num_scalar_prefetch=2,
        grid=(len(pairs),),
        in_specs=[
            pl.BlockSpec((None, IDX_INNER, tq), lambda p, qi, kj: (qi[p], 0, 0)),
            pl.BlockSpec((None, 16, tq), lambda p, qi, kj: (qi[p], 0, 0)),
            pl.BlockSpec((seq, IDX_HEAD_DIM), lambda p, qi, kj: (0, 0), pipeline_mode=pl.Buffered(1)),
            pl.BlockSpec((None, ATT_HEADS, ATT_HEAD_DIM, tq), lambda p, qi, kj: (qi[p], 0, 0, 0)),
            pl.BlockSpec((ATT_HEADS, tk, ATT_HEAD_DIM), lambda p, qi, kj: (0, kj[p], 0)),
            pl.BlockSpec((None, ATT_HEADS, ATT_HEAD_DIM, tk), lambda p, qi, kj: (kj[p], 0, 0, 0)),
        ],
        out_specs=pl.BlockSpec((tq, ATT_INNER), lambda p, qi, kj: (qi[p], 0)),
        scratch_shapes=[pltpu.VMEM((seq, tq), jnp.int32),
                        pltpu.VMEM((seq, tq), jnp.bfloat16),
                        pltpu.VMEM((2, IDX_HEADS, DSA_CHUNK, tq), jnp.float32),
                        pltpu.VMEM((8, tq), jnp.int32),
                        pltpu.VMEM((ATT_HEADS, tq), jnp.float32),
                        pltpu.VMEM((ATT_HEADS, tq), jnp.float32),
                        pltpu.VMEM((ATT_HEADS, ATT_HEAD_DIM, tq), jnp.float32),
                        pltpu.VMEM((ATT_HEADS, tk, tq), jnp.float32)],
    )
    return pl.pallas_call(
        functools.partial(_dsa_body, top_k=top_k, idx_bits=idx_bits),
        out_shape=jax.ShapeDtypeStruct((seq, ATT_INNER), MXU_DTYPE),
        grid_spec=grid_spec,
        compiler_params=_params(("arbitrary",)),
        name="dsa",
    )(qi_idx, kj_idx, qit, lohi, ki, qt, k, vt)


def _even_out_body(h_ref, y_ref, o_ref, wy_ref, wo_ref, out_ref):
    out_ref[...] = h_ref[...] + _dot(y_ref[...], wy_ref[...]) + _dot(o_ref[...], wo_ref[...])


def _even_out(h, y, o, wy, wo):
    seq = h.shape[0]
    tm = min(ROW_TILE, seq)
    row = pl.BlockSpec((tm, D_MODEL), lambda i: (i, 0))
    return pl.pallas_call(
        _even_out_body,
        out_shape=jax.ShapeDtypeStruct((seq, D_MODEL), jnp.float32),
        grid=(seq // tm,),
        in_specs=[row, row, row, _const_spec((SSD_INNER, D_MODEL)), _const_spec((ATT_INNER, D_MODEL))],
        out_specs=row,
        compiler_params=_params(("parallel",)),
        name="even_out",
    )(h, y, o, wy, wo)


def _odd_body(h_ref, g_ref, win_ref, lng_ref, lnb_ref, ws_ref, bs_ref, wout_ref, o_ref, gated_scr):
    h = h_ref[...]
    tm = h.shape[0]
    xn = _rms(h, g_ref[...]).astype(MXU_DTYPE)
    u = jax.nn.gelu(_dot(xn, win_ref[:, :SG_INNER]))
    v = jax.nn.gelu(_dot(xn, win_ref[:, SG_INNER:]))
    mu = jnp.mean(v, axis=-1, keepdims=True)
    vc = v - mu
    var = jnp.mean(vc * vc, axis=-1, keepdims=True)
    v = (vc * lax.rsqrt(var + EPS) * lng_ref[...] + lnb_ref[...]).astype(MXU_DTYPE)
    ri = lax.broadcasted_iota(jnp.int32, (SG_CHUNK, SG_CHUNK), 0)
    ci = lax.broadcasted_iota(jnp.int32, (SG_CHUNK, SG_CHUNK), 1)
    gw = SG_INNER // SG_GROUPS
    for g in range(SG_GROUPS):
        w = jnp.where(ri >= ci, ws_ref[g], 0.0).astype(MXU_DTYPE)
        gs = slice(g * gw, (g + 1) * gw)
        for c in range(tm // SG_CHUNK):
            rs = slice(c * SG_CHUNK, (c + 1) * SG_CHUNK)
            mixed = _dot(w, v[rs, gs]) + bs_ref[:, gs]
            gated_scr[rs, gs] = (u[rs, gs] * mixed).astype(gated_scr.dtype)
    o_ref[...] = h + _dot(gated_scr[...], wout_ref[...])


def _odd(h, g, win, ln_g, ln_b, w_s, bs_full, wout):
    seq = h.shape[0]
    tm = min(ROW_TILE, seq)
    row = pl.BlockSpec((tm, D_MODEL), lambda i: (i, 0))
    return pl.pallas_call(
        _odd_body,
        out_shape=jax.ShapeDtypeStruct((seq, D_MODEL), jnp.float32),
        grid=(seq // tm,),
        in_specs=[row, _const_spec((1, D_MODEL)), _const_spec((D_MODEL, 2 * SG_INNER)),
                  _const_spec((1, SG_INNER)), _const_spec((1, SG_INNER)),
                  _const_spec((SG_GROUPS, SG_CHUNK, SG_CHUNK)), _const_spec((SG_CHUNK, SG_INNER)),
                  _const_spec((SG_INNER, D_MODEL))],
        out_specs=row,
        scratch_shapes=[pltpu.VMEM((tm, SG_INNER), MXU_DTYPE)],
        compiler_params=_params(("parallel",)),
        name="odd",
    )(h, g, win, ln_g, ln_b, w_s, bs_full, wout)


def _rope_inputs(seq):
    def tables(rot_dim):
        inv = 1.0 / (ROPE_THETA ** (jnp.arange(0, rot_dim, 2, dtype=jnp.float32) / rot_dim))
        ang = jnp.arange(seq, dtype=jnp.float32)[:, None] * inv[None, :]
        return jnp.cos(ang), jnp.sin(ang)

    def lane_tables(cos, sin):
        half = cos.shape[1]
        pad = LANES - 2 * half
        c = jnp.concatenate([cos, cos, jnp.ones((seq, pad), jnp.float32)], axis=1)
        s_lo = jnp.concatenate([-sin, jnp.zeros((seq, LANES - half), jnp.float32)], axis=1)
        s_hi = jnp.concatenate([jnp.zeros((seq, half), jnp.float32), sin,
                                jnp.zeros((seq, pad), jnp.float32)], axis=1)
        return c, s_lo, s_hi

    cos_a, sin_a = tables(ATT_HEAD_DIM // ROPE_FRACTION)
    cos_i, sin_i = tables(IDX_HEAD_DIM // ROPE_FRACTION)
    return (*lane_tables(cos_a, sin_a), *lane_tables(cos_i, sin_i), cos_a.T, sin_a.T, cos_i.T, sin_i.T)


def _even_weights(w_in):
    offs = np.cumsum((SSD_INNER, SSD_CONV_DIM, SSD_HEADS, ATT_INNER, ATT_INNER, ATT_INNER,
                      IDX_INNER, IDX_HEAD_DIM, IDX_HEADS))[:-1].tolist()
    z, xbc, dt, q, k, v, qi, ki, wi = jnp.split(w_in, offs, axis=-1)
    pad = jnp.zeros((D_MODEL, LANES - IDX_HEAD_DIM - SSD_HEADS), w_in.dtype)
    wn = jnp.concatenate([z, xbc, k, ki, dt, pad], axis=1).astype(MXU_DTYPE)
    wi_pad = jnp.zeros((D_MODEL, _T_DT - _T_WI - IDX_HEADS), w_in.dtype)
    wt = jnp.concatenate([q, v, qi, wi, wi_pad, dt], axis=1).T.astype(MXU_DTYPE)
    assert wn.shape == (D_MODEL, _N_END) and wt.shape == (_T_END, D_MODEL)
    return wn, wt


def _even_mixer(h, g, w_in, conv_w, conv_b, dt_bias, a_log, d_skip, gate_norm, w_out, rope):
    wn, wt = _even_weights(w_in)
    z, xbc, small, k, ki, qt, vt, qit, lohi, dtt = _even_in(h, g, wn, wt, rope)
    y = _ssd(xbc, small, dtt, z, conv_w, conv_b, dt_bias, a_log, d_skip, gate_norm)
    o = _dsa(qit, lohi, ki, qt, k, vt)
    w_out = w_out.astype(MXU_DTYPE)
    return _even_out(h, y, o, w_out[:SSD_INNER], w_out[SSD_INNER:])


def _forward(x, norm_g, final_g, ffn_w_gu, ffn_w_down, ev_w_in, ev_conv_w, ev_conv_b, ev_dt_bias,
             ev_a_log, ev_d, ev_gate_norm, ev_w_out, od_w_in, od_ln_g, od_ln_b, od_w_s, od_b_s, od_w_out):
    bsz, seq, _ = x.shape
    depth = norm_g.shape[0]
    rope = _rope_inputs(seq)
    wg = ffn_w_gu[..., :FFN_HIDDEN].astype(MXU_DTYPE)
    wu = ffn_w_gu[..., FFN_HIDDEN:].astype(MXU_DTYPE)
    wd = ffn_w_down.astype(MXU_DTYPE)
    fg = final_g[None, :]
    outs = []
    for b in range(bsz):
        h = x[b]
        for layer in range(depth):
            j = layer // 2
            h = _ffn(h, norm_g[layer, 0][None, :], wg[layer, 0], wu[layer, 0], wd[layer, 0], fg, False)
            g1 = norm_g[layer, 1][None, :]
            if layer % 2 == 0:
                h = _even_mixer(h, g1, ev_w_in[j], ev_conv_w[j], ev_conv_b[j], ev_dt_bias[j], ev_a_log[j],
                                ev_d[j], ev_gate_norm[j], ev_w_out[j], rope)
            else:
                bs_full = jnp.repeat(od_b_s[j].T, SG_INNER // SG_GROUPS, axis=1)
                h = _odd(h, g1, od_w_in[j].astype(MXU_DTYPE), od_ln_g[j][None, :], od_ln_b[j][None, :],
                         od_w_s[j], bs_full, od_w_out[j].astype(MXU_DTYPE))
            h = _ffn(h, norm_g[layer, 2][None, :], wg[layer, 1], wu[layer, 1], wd[layer, 1], fg,
                     layer == depth - 1)
        outs.append(h)
    return outs[0][None] if bsz == 1 else jnp.stack(outs, axis=0)


def kernel(x, norm_g, final_g, ffn_w_gu, ffn_w_down, ev_w_in, ev_conv_w, ev_conv_b, ev_dt_bias, ev_a_log,
           ev_d, ev_gate_norm, ev_w_out, od_w_in, od_ln_g, od_ln_b, od_w_s, od_b_s, od_w_out):
    return _forward(x, norm_g, final_g, ffn_w_gu, ffn_w_down, ev_w_in, ev_conv_w, ev_conv_b, ev_dt_bias,
                    ev_a_log, ev_d, ev_gate_norm, ev_w_out, od_w_in, od_ln_g, od_ln_b, od_w_s, od_b_s,
                    od_w_out)
```

```python
import functools
import math

import numpy as np
import jax
import jax.numpy as jnp
from jax import lax
from jax.experimental import pallas as pl
from jax.experimental.pallas import tpu as pltpu

D_MODEL = 1024
SSD_HEADS = 16
SSD_HEAD_DIM = 64
SSD_INNER = SSD_HEADS * SSD_HEAD_DIM
SSD_GROUPS = 2
SSD_STATE = 128
SSD_CONV = 4
SSD_CHUNK = 256
SSD_CONV_DIM = SSD_INNER + 2 * SSD_GROUPS * SSD_STATE
ATT_HEADS = 8
ATT_HEAD_DIM = 128
ATT_INNER = ATT_HEADS * ATT_HEAD_DIM
IDX_HEADS = 8
IDX_HEAD_DIM = 64
IDX_INNER = IDX_HEADS * IDX_HEAD_DIM
TOPK_MAX = 256
SG_CHUNK = 128
SG_GROUPS = 8
SG_INNER = 2 * D_MODEL
FFN_HIDDEN = 2816
ROPE_THETA = 500000.0
ROPE_FRACTION = 4
EPS = 1e-6

ATT_ROT_HALF = ATT_HEAD_DIM // ROPE_FRACTION // 2
IDX_ROT_HALF = IDX_HEAD_DIM // ROPE_FRACTION // 2

MXU_DTYPE = jnp.bfloat16
LANES = 128
VMEM_LIMIT_BYTES = 56 * 1024 * 1024

ROW_TILE = 512
FFN_HIDDEN_CHUNK = FFN_HIDDEN // 2
DSA_Q_TILE = 256
DSA_K_TILE = 512
DSA_CHUNK = 256
TIE_EXTRACT_MAX = 8
SEARCH_ROUND = 5
LEAD_MASK = -(2 ** 16)
QUIET_NAN_BITS = 0x7FC00000
FINITE_KEY = 0x7F7FFFFF
MIN_NORMAL_EXP = 0x00800000
MIN_NORMAL_CODE = MIN_NORMAL_EXP >> 16
INT_MIN = -(2 ** 31)
NEG_BIG = -0.7 * float(np.finfo(np.float32).max)

_N_Z = 0
_N_XBC = _N_Z + SSD_INNER
_N_K = _N_XBC + SSD_CONV_DIM
_N_SMALL = _N_K + ATT_INNER
_N_END = _N_SMALL + LANES
_SM_KI = 0
_SM_DT = IDX_HEAD_DIM
_T_Q = 0
_T_V = _T_Q + ATT_INNER
_T_QI = _T_V + ATT_INNER
_T_WI = _T_QI + IDX_INNER
_T_DT = _T_WI + 16
_T_END = _T_DT + SSD_HEADS


def _dot(a, b):
    return jnp.dot(a, b, preferred_element_type=jnp.float32)


def _rms(x, g):
    return x * lax.rsqrt(jnp.mean(x * x, axis=-1, keepdims=True) + EPS) * g


def _softplus(x):
    return jnp.maximum(x, 0.0) + jnp.log1p(jnp.exp(-jnp.abs(x)))


def _split3(x):
    hi = x.astype(MXU_DTYPE)
    r = x - hi.astype(jnp.float32)
    mid = r.astype(MXU_DTYPE)
    lo = (r - mid.astype(jnp.float32)).astype(MXU_DTYPE)
    return hi, mid, lo


def _const_spec(shape):
    zeros = (0,) * len(shape)
    return pl.BlockSpec(shape, lambda *_: zeros, pipeline_mode=pl.Buffered(1))


def _params(sem):
    return pltpu.CompilerParams(dimension_semantics=sem, vmem_limit_bytes=VMEM_LIMIT_BYTES)


def _ffn_body(h_ref, g_ref, wg_ref, wu_ref, wd_ref, fg_ref, o_ref, *, final):
    h = h_ref[...]
    xn = _rms(h, g_ref[...]).astype(MXU_DTYPE)
    acc = None
    for c in range(FFN_HIDDEN // FFN_HIDDEN_CHUNK):
        sl = slice(c * FFN_HIDDEN_CHUNK, (c + 1) * FFN_HIDDEN_CHUNK)
        gate = _dot(xn, wg_ref[:, sl])
        up = _dot(xn, wu_ref[:, sl])
        act = (gate * jax.nn.sigmoid(gate) * up).astype(MXU_DTYPE)
        part = _dot(act, wd_ref[sl, :])
        acc = part if acc is None else acc + part
    out = h + 0.5 * acc
    if final:
        out = _rms(out, fg_ref[...])
    o_ref[...] = out


def _ffn(h, g, wg, wu, wd, final_g, final):
    seq = h.shape[0]
    tm = min(ROW_TILE, seq)
    row = pl.BlockSpec((tm, D_MODEL), lambda i: (i, 0))
    return pl.pallas_call(
        functools.partial(_ffn_body, final=final),
        out_shape=jax.ShapeDtypeStruct((seq, D_MODEL), jnp.float32),
        grid=(seq // tm,),
        in_specs=[row, _const_spec((1, D_MODEL)), _const_spec((D_MODEL, FFN_HIDDEN)),
                  _const_spec((D_MODEL, FFN_HIDDEN)), _const_spec((FFN_HIDDEN, D_MODEL)),
                  _const_spec((1, D_MODEL))],
        out_specs=row,
        compiler_params=_params(("parallel",)),
        name="ffn",
    )(h, g, wg, wu, wd, final_g)


def _rope_lanes(x, c, s_lo, s_hi, half):
    return x * c + pltpu.roll(x, LANES - half, 1) * s_lo + pltpu.roll(x, half, 1) * s_hi


def _rope_rows(x, c, s, half):
    x1, x2 = x[:half], x[half:2 * half]
    return jnp.concatenate([x1 * c - x2 * s, x1 * s + x2 * c, x[2 * half:]], axis=0)


def _even_in_body(h_ref, g_ref, wn_ref, wt_ref, ca_ref, sal_ref, sah_ref, ci_ref, sil_ref, sih_ref,
                  cat_ref, sat_ref, cit_ref, sit_ref,
                  z_ref, xbc_ref, small_ref, k_ref, ki_ref, qt_ref, vt_ref, qit_ref, lohi_ref, dtt_ref):
    xn = _rms(h_ref[...], g_ref[...])
    xnb = xn.astype(MXU_DTYPE)
    xnt = xn.T.astype(MXU_DTYPE)

    z_ref[...] = _dot(xnb, wn_ref[:, _N_Z:_N_XBC])
    xbc_ref[...] = _dot(xnb, wn_ref[:, _N_XBC:_N_K])
    kk = _dot(xnb, wn_ref[:, _N_K:_N_SMALL])
    ca, sal, sah = ca_ref[...], sal_ref[...], sah_ref[...]
    for hd in range(ATT_HEADS):
        sl = slice(hd * ATT_HEAD_DIM, (hd + 1) * ATT_HEAD_DIM)
        k_ref[hd] = _rope_lanes(kk[:, sl], ca, sal, sah, ATT_ROT_HALF).astype(k_ref.dtype)
    sm = _dot(xnb, wn_ref[:, _N_SMALL:_N_END])
    small_ref[...] = sm
    smr = _rope_lanes(sm, ci_ref[...], sil_ref[...], sih_ref[...], IDX_ROT_HALF)
    ki_ref[...] = smr[:, _SM_KI:_SM_KI + IDX_HEAD_DIM].astype(ki_ref.dtype)

    att_scale = ATT_HEAD_DIM ** -0.5 * math.log2(math.e)
    qt = _dot(wt_ref[_T_Q:_T_V, :], xnt)
    cat, sat = cat_ref[...], sat_ref[...]
    def put(ref, where, val):
        tb = ref.shape[-1]
        for b in range(ref.shape[0]):
            ref[(b, *where, slice(None))] = val[:, b * tb:(b + 1) * tb].astype(ref.dtype)

    vt = _dot(wt_ref[_T_V:_T_QI, :], xnt)
    for hd in range(ATT_HEADS):
        sl = slice(hd * ATT_HEAD_DIM, (hd + 1) * ATT_HEAD_DIM)
        put(qt_ref, (hd, slice(None)), _rope_rows(qt[sl], cat, sat, ATT_ROT_HALF) * att_scale)
        put(vt_ref, (hd, slice(None)), vt[sl])
    tail = _dot(wt_ref[_T_WI:_T_END, :], xnt)
    wit = tail[0:IDX_HEADS] * (IDX_HEADS ** -0.5 * IDX_HEAD_DIM ** -0.5)
    dtt_ref[...] = tail[_T_DT - _T_WI:_T_END - _T_WI]
    pos = wit > 0.0
    put(lohi_ref, (slice(0, 8),), jnp.where(pos, 0.0, -jnp.inf))
    put(lohi_ref, (slice(8, 16),), jnp.where(pos, jnp.inf, 0.0))
    qit = _dot(wt_ref[_T_QI:_T_WI, :], xnt)
    cit, sit = cit_ref[...], sit_ref[...]
    for hd in range(IDX_HEADS):
        sl = slice(hd * IDX_HEAD_DIM, (hd + 1) * IDX_HEAD_DIM)
        put(qit_ref, (sl,), _rope_rows(qit[sl], cit, sit, IDX_ROT_HALF) * wit[hd:hd + 1])


def _even_in(h, g, wn, wt, rope):
    seq = h.shape[0]
    tm = min(ROW_TILE, seq)
    f32 = jnp.float32

    def rows(width):
        return pl.BlockSpec((tm, width), lambda i: (i, 0))

    def cols(height):
        return pl.BlockSpec((height, tm), lambda i: (0, i))

    def blocks(height, tb):
        return pl.BlockSpec((tm // tb, height, tb), lambda i: (i, 0, 0))

    def head_blocks(tb):
        return pl.BlockSpec((tm // tb, ATT_HEADS, ATT_HEAD_DIM, tb), lambda i: (i, 0, 0, 0))

    assert tm % DSA_K_TILE == 0 and tm % DSA_Q_TILE == 0
    nq, nk = seq // DSA_Q_TILE, seq // DSA_K_TILE
    out_shape = (
        jax.ShapeDtypeStruct((seq, SSD_INNER), f32),
        jax.ShapeDtypeStruct((seq, SSD_CONV_DIM), f32),
        jax.ShapeDtypeStruct((seq, LANES), f32),
        jax.ShapeDtypeStruct((ATT_HEADS, seq, ATT_HEAD_DIM), MXU_DTYPE),
        jax.ShapeDtypeStruct((seq, IDX_HEAD_DIM), MXU_DTYPE),
        jax.ShapeDtypeStruct((nq, ATT_HEADS, ATT_HEAD_DIM, DSA_Q_TILE), MXU_DTYPE),
        jax.ShapeDtypeStruct((nk, ATT_HEADS, ATT_HEAD_DIM, DSA_K_TILE), MXU_DTYPE),
        jax.ShapeDtypeStruct((nq, IDX_INNER, DSA_Q_TILE), MXU_DTYPE),
        jax.ShapeDtypeStruct((nq, 16, DSA_Q_TILE), f32),
        jax.ShapeDtypeStruct((SSD_HEADS, seq), f32),
    )
    out_specs = (rows(SSD_INNER), rows(SSD_CONV_DIM), rows(LANES),
                 pl.BlockSpec((ATT_HEADS, tm, ATT_HEAD_DIM), lambda i: (0, i, 0)), rows(IDX_HEAD_DIM),
                 head_blocks(DSA_Q_TILE), head_blocks(DSA_K_TILE), blocks(IDX_INNER, DSA_Q_TILE),
                 blocks(16, DSA_Q_TILE), cols(SSD_HEADS))
    in_specs = [rows(D_MODEL), _const_spec((1, D_MODEL)), _const_spec(wn.shape), _const_spec(wt.shape)]
    in_specs += [rows(LANES)] * 6
    in_specs += [cols(ATT_ROT_HALF)] * 2 + [cols(IDX_ROT_HALF)] * 2
    return pl.pallas_call(
        _even_in_body,
        out_shape=out_shape,
        grid=(seq // tm,),
        in_specs=in_specs,
        out_specs=out_specs,
        compiler_params=_params(("parallel",)),
        name="even_in",
    )(h, g, wn, wt, *rope)


def _ssd_body(xbc_ref, small_ref, dtt_ref, z_ref, cw_ref, cb_ref, dtb_ref, dtbt_ref, a_ref, at_ref,
              drep_ref, gn_ref, o_ref, tail_scr, st_scr, y_scr, xdec_scr, decay_scr):
    q = SSD_CHUNK
    f32 = jnp.float32

    @pl.when(pl.program_id(0) == 0)
    def _():
        tail_scr[...] = jnp.zeros_like(tail_scr)
        st_scr[...] = jnp.zeros_like(st_scr)

    x = xbc_ref[...]
    tail = tail_scr[...]
    row8 = lax.broadcasted_iota(jnp.int32, (8, SSD_CONV_DIM), 0)
    conv = x * cw_ref[SSD_CONV - 1:SSD_CONV, :] + cb_ref[...]
    for shift in range(1, SSD_CONV):
        rolled = pltpu.roll(x, shift, 0)
        head = jnp.where(row8 < shift, pltpu.roll(tail, shift, 0), rolled[0:8])
        shifted = jnp.concatenate([head, rolled[8:]], axis=0)
        conv = conv + shifted * cw_ref[SSD_CONV - 1 - shift:SSD_CONV - shift, :]
    tail_scr[...] = x[q - 8:q]
    xbc = conv * jax.nn.sigmoid(conv)
    xs = xbc[:, :SSD_INNER]
    bm = xbc[:, SSD_INNER:SSD_INNER + SSD_GROUPS * SSD_STATE]
    cm = xbc[:, SSD_INNER + SSD_GROUPS * SSD_STATE:]

    dt_col = _softplus(small_ref[:, _SM_DT:_SM_DT + SSD_HEADS] + dtb_ref[...])
    dt_row = _softplus(dtt_ref[...] + dtbt_ref[...])
    adt_col = dt_col * (-jnp.exp(a_ref[...]))
    adt_row = dt_row * (-jnp.exp(at_ref[...]))
    ri = lax.broadcasted_iota(jnp.int32, (q, q), 0)
    ci = lax.broadcasted_iota(jnp.int32, (q, q), 1)
    causal = ri >= ci
    tril = jnp.where(causal, 1.0, 0.0).astype(MXU_DTYPE)
    triu = jnp.where(ri <= ci, 1.0, 0.0).astype(MXU_DTYPE)
    cs_col = sum(_dot(tril, p) for p in _split3(adt_col))
    cs_row = sum(_dot(p, triu) for p in _split3(adt_row))

    cmb = cm.astype(MXU_DTYPE)
    bmb = bm.astype(MXU_DTYPE)
    for g in range(SSD_GROUPS):
        gs = slice(g * SSD_STATE, (g + 1) * SSD_STATE)
        cb = lax.dot_general(cmb[:, gs], bmb[:, gs], (((1,), (1,)), ((), ())),
                             preferred_element_type=f32)
        kpg = SSD_HEADS // SSD_GROUPS
        for hd in range(g * kpg, (g + 1) * kpg):
            hs = slice(hd * SSD_HEAD_DIM, (hd + 1) * SSD_HEAD_DIM)
            col = cs_col[:, hd:hd + 1]
            row = cs_row[hd:hd + 1, :]
            last = cs_row[hd:hd + 1, q - 1:q]
            decay = jnp.exp(jnp.where(causal, col - row, -jnp.inf))
            xdt = xs[:, hs] * dt_col[:, hd:hd + 1]
            y = _dot((cb * decay).astype(MXU_DTYPE), xdt.astype(MXU_DTYPE))
            y_off = _dot(cmb[:, gs], st_scr[:, hs].astype(MXU_DTYPE))
            y_scr[:, hs] = y + y_off * jnp.exp(col)
            xdec_scr[:, hs] = (xdt * jnp.exp(last - col)).astype(xdec_scr.dtype)
            decay_scr[:, hs] = jnp.broadcast_to(jnp.exp(last), (1, SSD_HEAD_DIM))
        ws = slice(g * kpg * SSD_HEAD_DIM, (g + 1) * kpg * SSD_HEAD_DIM)
        bmt = bm[:, gs].T.astype(MXU_DTYPE)
        st_scr[:, ws] = st_scr[:, ws] * decay_scr[:, ws] + _dot(bmt, xdec_scr[:, ws])

    z = z_ref[...]
    y = (y_scr[...] + drep_ref[...] * xs) * (z * jax.nn.sigmoid(z))
    gw = SSD_INNER // SSD_GROUPS
    for g in range(SSD_GROUPS):
        sl = slice(g * gw, (g + 1) * gw)
        seg = y[:, sl]
        seg = seg * lax.rsqrt(jnp.mean(seg * seg, axis=-1, keepdims=True) + EPS)
        o_ref[:, sl] = (seg * gn_ref[:, sl]).astype(o_ref.dtype)


def _ssd(xbc, small, dtt, z, conv_w, conv_b, dt_bias, a_log, d_skip, gate_norm):
    seq = xbc.shape[0]
    q = SSD_CHUNK
    f32 = jnp.float32

    def rows(width):
        return pl.BlockSpec((q, width), lambda i: (i, 0))

    drep = jnp.repeat(d_skip, SSD_HEAD_DIM)[None, :]
    return pl.pallas_call(
        _ssd_body,
        out_shape=jax.ShapeDtypeStruct((seq, SSD_INNER), MXU_DTYPE),
        grid=(seq // q,),
        in_specs=[rows(SSD_CONV_DIM), rows(LANES), pl.BlockSpec((SSD_HEADS, q), lambda i: (0, i)),
                  rows(SSD_INNER), _const_spec((SSD_CONV, SSD_CONV_DIM)), _const_spec((1, SSD_CONV_DIM)),
                  _const_spec((1, SSD_HEADS)), _const_spec((SSD_HEADS, 1)),
                  _const_spec((1, SSD_HEADS)), _const_spec((SSD_HEADS, 1)),
                  _const_spec((1, SSD_INNER)), _const_spec((1, SSD_INNER))],
        out_specs=rows(SSD_INNER),
        scratch_shapes=[pltpu.VMEM((8, SSD_CONV_DIM), f32),
                        pltpu.VMEM((SSD_STATE, SSD_INNER), f32),
                        pltpu.VMEM((q, SSD_INNER), f32),
                        pltpu.VMEM((q, SSD_INNER), MXU_DTYPE),
                        pltpu.VMEM((1, SSD_INNER), f32)],
        compiler_params=_params(("arbitrary",)),
        name="ssd",
    )(xbc, small, dtt, z, conv_w, conv_b[None, :], dt_bias[None, :], dt_bias[:, None],
      a_log[None, :], a_log[:, None], drep, gate_norm[None, :])


def _dsa_body(qi_ref, kj_ref, qit_ref, lohi_ref, ki_ref, qt_ref, k_ref, vt_ref, o_ref,
              keys_scr, lead_scr, y_scr, thr_scr, m_scr, l_scr, acc_scr, logit_scr, *, top_k, idx_bits):
    tq, tk, ck = DSA_Q_TILE, DSA_K_TILE, DSA_CHUNK
    i32 = jnp.int32
    p = pl.program_id(0)
    qi = qi_ref[p]
    kj = kj_ref[p]

    @pl.when(kj == 0)
    def _select():
        chunks_per_tile = tk // ck
        n_tiles = ((qi + 1) * tq + tk - 1) // tk
        n_chunks = n_tiles * chunks_per_tile
        lo = lohi_ref[0:8, :]
        hi = lohi_ref[8:16, :]
        rows = 4 * 8
        t_pos = qi * tq + lax.broadcasted_iota(i32, (rows, tq), 1)
        s_off = lax.broadcasted_iota(i32, (ck, tq), 0)
        s_rows = lax.broadcasted_iota(i32, (rows, tq), 0)

        def project(c, slot):
            kic = ki_ref[pl.ds(pl.multiple_of(c * ck, ck), ck), :]
            for hd in range(IDX_HEADS):
                y_scr[slot, hd] = _dot(kic, qit_ref[hd * IDX_HEAD_DIM:(hd + 1) * IDX_HEAD_DIM, :])

        def emit_keys(c, slot, diagonal):
            for g in range(ck // rows):
                r0 = pl.multiple_of(c * ck + g * rows, rows)
                gs = slice(g * rows, (g + 1) * rows)
                sc = None
                for hd in range(IDX_HEADS):
                    term = jnp.minimum(jnp.maximum(y_scr[slot, hd, gs, :], lo[hd:hd + 1]), hi[hd:hd + 1])
                    sc = term if sc is None else sc + term
                bits = pltpu.bitcast(sc, i32)
                bits = jnp.where((bits & 0x7FFFFFFF) < MIN_NORMAL_EXP, 0, bits)
                key = bits ^ ((bits >> 31) & 0x7FFFFFFF)
                lead = bits & LEAD_MASK
                if diagonal:
                    causal = r0 + s_rows <= t_pos
                    key = jnp.where(causal, key, INT_MIN)
                    lead = jnp.where(causal, lead, QUIET_NAN_BITS)
                keys_scr[pl.ds(r0, rows), :] = key
                lead_scr[pl.ds(r0, rows), :] = pltpu.bitcast(lead, jnp.float32).astype(jnp.bfloat16)

        assert chunks_per_tile % 2 == 0

        def score_tile(t, diagonal):
            for j in range(chunks_per_tile):
                c = t * chunks_per_tile + j
                emit_keys(c, j % 2, diagonal)
                if not (diagonal and j == chunks_per_tile - 1):
                    project(c + 1, (j + 1) % 2)

        project(0, 0)
        lax.fori_loop(0, n_tiles - 1, lambda t, carry: (score_tile(t, False), carry)[1], 0)
        score_tile(n_tiles - 1, True)

        partial_rows = 4 * 8
        s_tile = lax.broadcasted_iota(i32, (partial_rows, tq), 0)

        def key_groups(c):
            for g in range(tk // partial_rows):
                r0 = pl.multiple_of(c * tk + g * partial_rows, partial_rows)
                yield keys_scr[pl.ds(r0, partial_rows), :], r0

        def count(pred):
            def body(c, acc):
                for kk, r0 in key_groups(c):
                    acc = acc + jnp.where(pred(kk, r0), 1.0, 0.0)
                return acc
            acc = lax.fori_loop(0, n_tiles, body, jnp.zeros((partial_rows, tq), jnp.float32))
            return acc.sum(axis=0, keepdims=True)

        def count_lead(cand_code):
            cand_code = jnp.where((cand_code > 0) & (cand_code < MIN_NORMAL_CODE), MIN_NORMAL_CODE,
                                  jnp.where((cand_code < 0) & (cand_code >= -MIN_NORMAL_CODE), 0, cand_code))
            cand_bits = (cand_code ^ ((cand_code >> 31) & 0x7FFF)) << 16
            cand = jnp.broadcast_to(pltpu.bitcast(cand_bits, jnp.float32), (16, tq)).astype(jnp.bfloat16)
            one, zero = jnp.ones((), jnp.bfloat16), jnp.zeros((), jnp.bfloat16)

            def body(c, acc):
                parts = []
                for g in range(tk // 128):
                    r0 = pl.multiple_of(c * tk + g * 128, 128)
                    lead = lead_scr[pl.ds(r0, 128), :].reshape(8, 16, tq)
                    parts += [jnp.where(lead[j] >= cand, one, zero) for j in range(8)]
                while len(parts) > 1:
                    parts = [a + b for a, b in zip(parts[0::2], parts[1::2])]
                return acc + parts[0].astype(jnp.float32)
            acc = lax.fori_loop(0, n_tiles, body, jnp.zeros((16, tq), jnp.float32))
            return acc.sum(axis=0, keepdims=True)

        k_f = jnp.float32(top_k)
        n_t = (qi * tq + 1 + lax.broadcasted_iota(i32, (1, tq), 1)).astype(jnp.float32)

        def lead_step(b, carry):
            code, c_lo, c_hi = carry
            cand = code + lax.shift_left(i32(1), i32(15) - b)
            cnt = count_lead(cand)
            ok = cnt >= k_f
            return jnp.where(ok, cand, code), jnp.where(ok, cnt, c_lo), jnp.where(ok, c_hi, cnt)

        code, c_lo, c_hi = lax.fori_loop(
            0, 16, lead_step, (jnp.full((1, tq), -(2 ** 15), i32), n_t, jnp.zeros((1, tq), jnp.float32)))

        lo_k = jnp.maximum(code << 16, INT_MIN + 1)
        hi_k = jnp.where(code == 2 ** 15 - 1, jnp.iinfo(i32).max, (code + 1) << 16)

        def finished(lo_k, hi_k, c_lo):
            return (c_lo <= k_f) | (lo_k + 1 >= hi_k)

        def key_value(kk):
            kk = jnp.clip(kk, -FINITE_KEY, FINITE_KEY)
            return pltpu.bitcast(kk ^ ((kk >> 31) & 0x7FFFFFFF), jnp.float32)

        def tighten(lo_k, hi_k, c_lo, c_hi, done, it):
            def body(c, carry):
                mn, mx = carry
                for kk, _ in key_groups(c):
                    mn = jnp.minimum(mn, jnp.where(kk >= lo_k, kk, jnp.iinfo(i32).max))
                    mx = jnp.maximum(mx, jnp.where(kk < hi_k, kk, INT_MIN))
                return mn, mx
            mn, mx = lax.fori_loop(0, n_tiles, body, (jnp.full((partial_rows, tq), jnp.iinfo(i32).max, i32),
                                                      jnp.full((partial_rows, tq), INT_MIN, i32)))
            mn, mx = mn.min(axis=0, keepdims=True), mx.max(axis=0, keepdims=True)
            return jnp.where(done, lo_k, mn), jnp.where(done, hi_k, mx + 1), c_lo, c_hi

        def probe(lo_k, hi_k, c_lo, c_hi, done, it):
            below, span = c_lo - k_f, c_lo - c_hi
            end = jnp.where(below + below >= span, hi_k - 1, lo_k + 1)
            lo_v, hi_v = key_value(lo_k), key_value(hi_k)
            guess_v = lo_v + (below + 0.5) / jnp.maximum(span, 1.0) * (hi_v - lo_v)
            guess_b = pltpu.bitcast(guess_v, i32)
            guess = guess_b ^ ((guess_b >> 31) & 0x7FFFFFFF)
            mid = (lo_k >> 1) + (hi_k >> 1) + (lo_k & hi_k & 1)
            phase = it % SEARCH_ROUND
            cand = jnp.where(phase == 1, end, jnp.where(phase == 3, mid, guess))
            cand = jnp.where(done, lo_k, jnp.maximum(lo_k + 1, jnp.minimum(hi_k - 1, cand)))
            cnt = count(lambda kk, r0: kk >= cand)
            up = jnp.logical_and(~done, cnt >= k_f)
            down = jnp.logical_and(~done, cnt < k_f)
            return (jnp.where(up, cand, lo_k), jnp.where(down, cand, hi_k),
                    jnp.where(up, cnt, c_lo), jnp.where(down, cnt, c_hi))

        def search_cond(carry):
            return jnp.logical_and(carry[0] < SEARCH_ROUND * 32, carry[1] > 0)

        def search_step(carry):
            it, _, lo_k, hi_k, c_lo, c_hi = carry
            done = finished(lo_k, hi_k, c_lo)
            lo_k, hi_k, c_lo, c_hi = lax.cond(it % SEARCH_ROUND == 0, tighten, probe, lo_k, hi_k, c_lo, c_hi, done, it)
            still = jnp.where(finished(lo_k, hi_k, c_lo), 0.0, 1.0)
            return it + 1, jnp.max(still).astype(i32), lo_k, hi_k, c_lo, c_hi

        active = jnp.max(jnp.where(finished(lo_k, hi_k, c_lo), 0.0, 1.0)).astype(i32)
        _, _, thr, _, n_ge, n_gt = lax.while_loop(search_cond, search_step, (i32(0), active, lo_k, hi_k, c_lo, c_hi))
        thr_scr[0:1, :] = thr

        tied = n_ge > k_f
        need = jnp.where(tied, k_f - n_gt, 0.0)
        max_need = jnp.max(need)

        @pl.when(max_need > 0.0)
        def _ties():
            int_max = jnp.iinfo(i32).max

            def by_extraction(_):
                def next_tie(j, cut):
                    def body(c, mn):
                        for kk, r0 in key_groups(c):
                            idx = r0 + s_tile
                            mn = jnp.minimum(mn, jnp.where((kk == thr) & (idx > cut), idx, int_max))
                        return mn
                    mn = lax.fori_loop(0, n_tiles, body, jnp.full((partial_rows, tq), int_max, i32))
                    return jnp.where(j.astype(jnp.float32) < need, mn.min(axis=0, keepdims=True), cut)
                return lax.fori_loop(0, max_need.astype(i32), next_tie, jnp.full((1, tq), -1, i32))

            def by_index_bits(_):
                def idx_step(b, cut):
                    cand = cut + lax.shift_left(i32(1), i32(idx_bits - 1) - b)
                    cnt = count(lambda kk, r0: jnp.where(kk == thr, r0 + s_tile, cand) < cand)
                    return jnp.where(cnt < need, cand, cut)
                return lax.fori_loop(0, idx_bits, idx_step, jnp.zeros((1, tq), i32))

            cut = lax.cond(max_need <= TIE_EXTRACT_MAX, by_extraction, by_index_bits, 0)
            cut = jnp.where(tied, cut, int_max)

            def drop(c, carry):
                r0 = pl.multiple_of(c * ck, ck)
                kk = keys_scr[pl.ds(r0, ck), :]
                excess = jnp.where(kk == thr, r0 + s_off, INT_MIN) > cut
                keys_scr[pl.ds(r0, ck), :] = jnp.where(excess, INT_MIN, kk)
                return carry

            lax.fori_loop(0, n_chunks, drop, 0)

        m_scr[...] = jnp.full_like(m_scr, NEG_BIG)
        l_scr[...] = jnp.zeros_like(l_scr)
        acc_scr[...] = jnp.zeros_like(acc_scr)

    sel = keys_scr[pl.ds(pl.multiple_of(kj * tk, tk), tk), :] >= thr_scr[0:1, :]
    bias = jnp.where(sel, 0.0, -jnp.inf)
    col_max = []
    for hd in range(ATT_HEADS):
        logit = _dot(k_ref[hd], qt_ref[hd]) + bias
        logit_scr[hd] = logit
        col_max.append(jnp.max(logit, axis=0, keepdims=True))
    m_old = m_scr[...]
    m_new = jnp.maximum(m_old, jnp.concatenate(col_max, axis=0))
    alpha = jnp.exp2(m_old - m_new)
    m_scr[...] = m_new
    col_sum = []
    for hd in range(ATT_HEADS):
        prob = jnp.exp2(logit_scr[hd] - m_new[hd:hd + 1])
        col_sum.append(jnp.sum(prob, axis=0, keepdims=True))
        acc_scr[hd] = alpha[hd:hd + 1] * acc_scr[hd] + _dot(vt_ref[hd], prob.astype(MXU_DTYPE))
    l_scr[...] = alpha * l_scr[...] + jnp.concatenate(col_sum, axis=0)

    @pl.when(kj == ((qi + 1) * tq - 1) // tk)
    def _finish():
        for hd in range(ATT_HEADS):
            hs = slice(hd * ATT_HEAD_DIM, (hd + 1) * ATT_HEAD_DIM)
            o_ref[:, hs] = (acc_scr[hd] / l_scr[hd:hd + 1, :]).T.astype(o_ref.dtype)


def _dsa(qit, lohi, ki, qt, k, vt):
    seq = k.shape[1]
    tq, tk = DSA_Q_TILE, DSA_K_TILE
    assert seq % tk == 0 and tk % tq == 0 and tq % DSA_CHUNK == 0
    assert k.shape == (ATT_HEADS, seq, ATT_HEAD_DIM)
    assert qt.shape == (seq // tq, ATT_HEADS, ATT_HEAD_DIM, tq) and vt.shape == (seq // tk, ATT_HEADS, ATT_HEAD_DIM, tk)
    nq = seq // tq
    pairs = [(i, j) for i in range(nq) for j in range(((i + 1) * tq - 1) // tk + 1)]
    qi_idx = jnp.asarray([pq for pq, _ in pairs], jnp.int32)
    kj_idx = jnp.asarray([pk for _, pk in pairs], jnp.int32)
    top_k = min(TOPK_MAX, seq // 4)
    idx_bits = max(1, (seq - 1).bit_length())
    grid_spec = pltpu.PrefetchScalarGridSpec(
        num_scalar_prefetch=2,
        grid=(len(pairs),),
        in_specs=[
            pl.BlockSpec((None, IDX_INNER, tq), lambda p, qi, kj: (qi[p], 0, 0)),
            pl.BlockSpec((None, 16, tq), lambda p, qi, kj: (qi[p], 0, 0)),
            pl.BlockSpec((seq, IDX_HEAD_DIM), lambda p, qi, kj: (0, 0), pipeline_mode=pl.Buffered(1)),
            pl.BlockSpec((None, ATT_HEADS, ATT_HEAD_DIM, tq), lambda p, qi, kj: (qi[p], 0, 0, 0)),
            pl.BlockSpec((ATT_HEADS, tk, ATT_HEAD_DIM), lambda p, qi, kj: (0, kj[p], 0)),
            pl.BlockSpec((None, ATT_HEADS, ATT_HEAD_DIM, tk), lambda p, qi, kj: (kj[p], 0, 0, 0)),
        ],
        out_specs=pl.BlockSpec((tq, ATT_INNER), lambda p, qi, kj: (qi[p], 0)),
        scratch_shapes=[pltpu.VMEM((seq, tq), jnp.int32),
                        pltpu.VMEM((seq, tq), jnp.bfloat16),
                        pltpu.VMEM((2, IDX_HEADS, DSA_CHUNK, tq), jnp.float32),
                        pltpu.VMEM((8, tq), jnp.int32),
                        pltpu.VMEM((ATT_HEADS, tq), jnp.float32),
                        pltpu.VMEM((ATT_HEADS, tq), jnp.float32),
                        pltpu.VMEM((ATT_HEADS, ATT_HEAD_DIM, tq), jnp.float32),
                        pltpu.VMEM((ATT_HEADS, tk, tq), jnp.float32)],
    )
    return pl.pallas_call(
        functools.partial(_dsa_body, top_k=top_k, idx_bits=idx_bits),
        out_shape=jax.ShapeDtypeStruct((seq, ATT_INNER), MXU_DTYPE),
        grid_spec=grid_spec,
        compiler_params=_params(("arbitrary",)),
        name="dsa",
    )(qi_idx, kj_idx, qit, lohi, ki, qt, k, vt)


def _even_out_body(h_ref, y_ref, o_ref, wy_ref, wo_ref, out_ref):
    out_ref[...] = h_ref[...] + _dot(y_ref[...], wy_ref[...]) + _dot(o_ref[...], wo_ref[...])


def _even_out(h, y, o, wy, wo):
    seq = h.shape[0]
    tm = min(ROW_TILE, seq)
    row = pl.BlockSpec((tm, D_MODEL), lambda i: (i, 0))
    return pl.pallas_call(
        _even_out_body,
        out_shape=jax.ShapeDtypeStruct((seq, D_MODEL), jnp.float32),
        grid=(seq // tm,),
        in_specs=[row, row, row, _const_spec((SSD_INNER, D_MODEL)), _const_spec((ATT_INNER, D_MODEL))],
        out_specs=row,
        compiler_params=_params(("parallel",)),
        name="even_out",
    )(h, y, o, wy, wo)


def _odd_body(h_ref, g_ref, win_ref, lng_ref, lnb_ref, ws_ref, bs_ref, wout_ref, o_ref, gated_scr):
    h = h_ref[...]
    tm = h.shape[0]
    xn = _rms(h, g_ref[...]).astype(MXU_DTYPE)
    u = jax.nn.gelu(_dot(xn, win_ref[:, :SG_INNER]))
    v = jax.nn.gelu(_dot(xn, win_ref[:, SG_INNER:]))
    mu = jnp.mean(v, axis=-1, keepdims=True)
    vc = v - mu
    var = jnp.mean(vc * vc, axis=-1, keepdims=True)
    v = (vc * lax.rsqrt(var + EPS) * lng_ref[...] + lnb_ref[...]).astype(MXU_DTYPE)
    ri = lax.broadcasted_iota(jnp.int32, (SG_CHUNK, SG_CHUNK), 0)
    ci = lax.broadcasted_iota(jnp.int32, (SG_CHUNK, SG_CHUNK), 1)
    gw = SG_INNER // SG_GROUPS
    for g in range(SG_GROUPS):
        w = jnp.where(ri >= ci, ws_ref[g], 0.0).astype(MXU_DTYPE)
        gs = slice(g * gw, (g + 1) * gw)
        for c in range(tm // SG_CHUNK):
            rs = slice(c * SG_CHUNK, (c + 1) * SG_CHUNK)
            mixed = _dot(w, v[rs, gs]) + bs_ref[:, gs]
            gated_scr[rs, gs] = (u[rs, gs] * mixed).astype(gated_scr.dtype)
    o_ref[...] = h + _dot(gated_scr[...], wout_ref[...])


def _odd(h, g, win, ln_g, ln_b, w_s, bs_full, wout):
    seq = h.shape[0]
    tm = min(ROW_TILE, seq)
    row = pl.BlockSpec((tm, D_MODEL), lambda i: (i, 0))
    return pl.pallas_call(
        _odd_body,
        out_shape=jax.ShapeDtypeStruct((seq, D_MODEL), jnp.float32),
        grid=(seq // tm,),
        in_specs=[row, _const_spec((1, D_MODEL)), _const_spec((D_MODEL, 2 * SG_INNER)),
                  _const_spec((1, SG_INNER)), _const_spec((1, SG_INNER)),
                  _const_spec((SG_GROUPS, SG_CHUNK, SG_CHUNK)), _const_spec((SG_CHUNK, SG_INNER)),
                  _const_spec((SG_INNER, D_MODEL))],
        out_specs=row,
        scratch_shapes=[pltpu.VMEM((tm, SG_INNER), MXU_DTYPE)],
        compiler_params=_params(("parallel",)),
        name="odd",
    )(h, g, win, ln_g, ln_b, w_s, bs_full, wout)


def _rope_inputs(seq):
    def tables(rot_dim):
        inv = 1.0 / (ROPE_THETA ** (jnp.arange(0, rot_dim, 2, dtype=jnp.float32) / rot_dim))
        ang = jnp.arange(seq, dtype=jnp.float32)[:, None] * inv[None, :]
        return jnp.cos(ang), jnp.sin(ang)

    def lane_tables(cos, sin):
        half = cos.shape[1]
        pad = LANES - 2 * half
        c = jnp.concatenate([cos, cos, jnp.ones((seq, pad), jnp.float32)], axis=1)
        s_lo = jnp.concatenate([-sin, jnp.zeros((seq, LANES - half), jnp.float32)], axis=1)
        s_hi = jnp.concatenate([jnp.zeros((seq, half), jnp.float32), sin,
                                jnp.zeros((seq, pad), jnp.float32)], axis=1)
        return c, s_lo, s_hi

    cos_a, sin_a = tables(ATT_HEAD_DIM // ROPE_FRACTION)
    cos_i, sin_i = tables(IDX_HEAD_DIM // ROPE_FRACTION)
    return (*lane_tables(cos_a, sin_a), *lane_tables(cos_i, sin_i), cos_a.T, sin_a.T, cos_i.T, sin_i.T)


def _even_weights(w_in):
    offs = np.cumsum((SSD_INNER, SSD_CONV_DIM, SSD_HEADS, ATT_INNER, ATT_INNER, ATT_INNER,
                      IDX_INNER, IDX_HEAD_DIM, IDX_HEADS))[:-1].tolist()
    z, xbc, dt, q, k, v, qi, ki, wi = jnp.split(w_in, offs, axis=-1)
    pad = jnp.zeros((D_MODEL, LANES - IDX_HEAD_DIM - SSD_HEADS), w_in.dtype)
    wn = jnp.concatenate([z, xbc, k, ki, dt, pad], axis=1).astype(MXU_DTYPE)
    wi_pad = jnp.zeros((D_MODEL, _T_DT - _T_WI - IDX_HEADS), w_in.dtype)
    wt = jnp.concatenate([q, v, qi, wi, wi_pad, dt], axis=1).T.astype(MXU_DTYPE)
    assert wn.shape == (D_MODEL, _N_END) and wt.shape == (_T_END, D_MODEL)
    return wn, wt


def _even_mixer(h, g, w_in, conv_w, conv_b, dt_bias, a_log, d_skip, gate_norm, w_out, rope):
    wn, wt = _even_weights(w_in)
    z, xbc, small, k, ki, qt, vt, qit, lohi, dtt = _even_in(h, g, wn, wt, rope)
    y = _ssd(xbc, small, dtt, z, conv_w, conv_b, dt_bias, a_log, d_skip, gate_norm)
    o = _dsa(qit, lohi, ki, qt, k, vt)
    w_out = w_out.astype(MXU_DTYPE)
    return _even_out(h, y, o, w_out[:SSD_INNER], w_out[SSD_INNER:])


def _forward(x, norm_g, final_g, ffn_w_gu, ffn_w_down, ev_w_in, ev_conv_w, ev_conv_b, ev_dt_bias,
             ev_a_log, ev_d, ev_gate_norm, ev_w_out, od_w_in, od_ln_g, od_ln_b, od_w_s, od_b_s, od_w_out):
    bsz, seq, _ = x.shape
    depth = norm_g.shape[0]
    rope = _rope_inputs(seq)
    wg = ffn_w_gu[..., :FFN_HIDDEN].astype(MXU_DTYPE)
    wu = ffn_w_gu[..., FFN_HIDDEN:].astype(MXU_DTYPE)
    wd = ffn_w_down.astype(MXU_DTYPE)
    fg = final_g[None, :]
    outs = []
    for b in range(bsz):
        h = x[b]
        for layer in range(depth):
            j = layer // 2
            h = _ffn(h, norm_g[layer, 0][None, :], wg[layer, 0], wu[layer, 0], wd[layer, 0], fg, False)
            g1 = norm_g[layer, 1][None, :]
            if layer % 2 == 0:
                h = _even_mixer(h, g1, ev_w_in[j], ev_conv_w[j], ev_conv_b[j], ev_dt_bias[j], ev_a_log[j],
                                ev_d[j], ev_gate_norm[j], ev_w_out[j], rope)
            else:
                bs_full = jnp.repeat(od_b_s[j].T, SG_INNER // SG_GROUPS, axis=1)
                h = _odd(h, g1, od_w_in[j].astype(MXU_DTYPE), od_ln_g[j][None, :], od_ln_b[j][None, :],
                         od_w_s[j], bs_full, od_w_out[j].astype(MXU_DTYPE))
            h = _ffn(h, norm_g[layer, 2][None, :], wg[layer, 1], wu[layer, 1], wd[layer, 1], fg,
                     layer == depth - 1)
        outs.append(h)
    return outs[0][None] if bsz == 1 else jnp.stack(outs, axis=0)


def kernel(x, norm_g, final_g, ffn_w_gu, ffn_w_down, ev_w_in, ev_conv_w, ev_conv_b, ev_dt_bias, ev_a_log,
           ev_d, ev_gate_norm, ev_w_out, od_w_in, od_ln_g, od_ln_b, od_w_s, od_b_s, od_w_out):
    return _forward(x, norm_g, final_g, ffn_w_gu, ffn_w_down, ev_w_in, ev_conv_w, ev_conv_b, ev_dt_bias,
                    ev_a_log, ev_d, ev_gate_norm, ev_w_out, od_w_in, od_ln_g, od_ln_b, od_w_s, od_b_s,
                    od_w_out)
```

```python
import functools
import math

import numpy as np
import jax
import jax.numpy as jnp
from jax import lax
from jax.experimental import pallas as pl
from jax.experimental.pallas import tpu as pltpu

D_MODEL = 1024
SSD_HEADS = 16
SSD_HEAD_DIM = 64
SSD_INNER = SSD_HEADS * SSD_HEAD_DIM
SSD_GROUPS = 2
SSD_STATE = 128
SSD_CONV = 4
SSD_CHUNK = 256
SSD_CONV_DIM = SSD_INNER + 2 * SSD_GROUPS * SSD_STATE
ATT_HEADS = 8
ATT_HEAD_DIM = 128
ATT_INNER = ATT_HEADS * ATT_HEAD_DIM
IDX_HEADS = 8
IDX_HEAD_DIM = 64
IDX_INNER = IDX_HEADS * IDX_HEAD_DIM
TOPK_MAX = 256
SG_CHUNK = 128
SG_GROUPS = 8
SG_INNER = 2 * D_MODEL
FFN_HIDDEN = 2816
ROPE_THETA = 500000.0
ROPE_FRACTION = 4
EPS = 1e-6

ATT_ROT_HALF = ATT_HEAD_DIM // ROPE_FRACTION // 2
IDX_ROT_HALF = IDX_HEAD_DIM // ROPE_FRACTION // 2

MXU_DTYPE = jnp.bfloat16
LANES = 128
VMEM_LIMIT_BYTES = 56 * 1024 * 1024

ROW_TILE = 512
FFN_HIDDEN_CHUNK = FFN_HIDDEN // 2
DSA_Q_TILE = 256
DSA_K_TILE = 512
DSA_CHUNK = 256
TIE_EXTRACT_MAX = 8
SEARCH_ROUND = 5
LEAD_MASK = -(2 ** 16)
QUIET_NAN_BITS = 0x7FC00000
FINITE_KEY = 0x7F7FFFFF
MIN_NORMAL_EXP = 0x00800000
MIN_NORMAL_CODE = MIN_NORMAL_EXP >> 16
INT_MIN = -(2 ** 31)
NEG_BIG = -0.7 * float(np.finfo(np.float32).max)

_N_Z = 0
_N_XBC = _N_Z + SSD_INNER
_N_K = _N_XBC + SSD_CONV_DIM
_N_SMALL = _N_K + ATT_INNER
_N_END = _N_SMALL + LANES
_SM_KI = 0
_SM_DT = IDX_HEAD_DIM
_T_Q = 0
_T_V = _T_Q + ATT_INNER
_T_QI = _T_V + ATT_INNER
_T_WI = _T_QI + IDX_INNER
_T_DT = _T_WI + 16
_T_END = _T_DT + SSD_HEADS


def _dot(a, b):
    return jnp.dot(a, b, preferred_element_type=jnp.float32)


def _rms(x, g):
    return x * lax.rsqrt(jnp.mean(x * x, axis=-1, keepdims=True) + EPS) * g


def _softplus(x):
    return jnp.maximum(x, 0.0) + jnp.log1p(jnp.exp(-jnp.abs(x)))


def _split3(x):
    hi = x.astype(MXU_DTYPE)
    r = x - hi.astype(jnp.float32)
    mid = r.astype(MXU_DTYPE)
    lo = (r - mid.astype(jnp.float32)).astype(MXU_DTYPE)
    return hi, mid, lo


def _const_spec(shape):
    zeros = (0,) * len(shape)
    return pl.BlockSpec(shape, lambda *_: zeros, pipeline_mode=pl.Buffered(1))


def _params(sem):
    return pltpu.CompilerParams(dimension_semantics=sem, vmem_limit_bytes=VMEM_LIMIT_BYTES)


def _ffn_body(h_ref, g_ref, wg_ref, wu_ref, wd_ref, fg_ref, o_ref, *, final):
    h = h_ref[...]
    xn = _rms(h, g_ref[...]).astype(MXU_DTYPE)
    acc = None
    for c in range(FFN_HIDDEN // FFN_HIDDEN_CHUNK):
        sl = slice(c * FFN_HIDDEN_CHUNK, (c + 1) * FFN_HIDDEN_CHUNK)
        gate = _dot(xn, wg_ref[:, sl])
        up = _dot(xn, wu_ref[:, sl])
        act = (gate * jax.nn.sigmoid(gate) * up).astype(MXU_DTYPE)
        part = _dot(act, wd_ref[sl, :])
        acc = part if acc is None else acc + part
    out = h + 0.5 * acc
    if final:
        out = _rms(out, fg_ref[...])
    o_ref[...] = out


def _ffn(h, g, wg, wu, wd, final_g, final):
    seq = h.shape[0]
    tm = min(ROW_TILE, seq)
    row = pl.BlockSpec((tm, D_MODEL), lambda i: (i, 0))
    return pl.pallas_call(
        functools.partial(_ffn_body, final=final),
        out_shape=jax.ShapeDtypeStruct((seq, D_MODEL), jnp.float32),
        grid=(seq // tm,),
        in_specs=[row, _const_spec((1, D_MODEL)), _const_spec((D_MODEL, FFN_HIDDEN)),
                  _const_spec((D_MODEL, FFN_HIDDEN)), _const_spec((FFN_HIDDEN, D_MODEL)),
                  _const_spec((1, D_MODEL))],
        out_specs=row,
        compiler_params=_params(("parallel",)),
        name="ffn",
    )(h, g, wg, wu, wd, final_g)


def _rope_lanes(x, c, s_lo, s_hi, half):
    return x * c + pltpu.roll(x, LANES - half, 1) * s_lo + pltpu.roll(x, half, 1) * s_hi


def _rope_rows(x, c, s, half):
    x1, x2 = x[:half], x[half:2 * half]
    return jnp.concatenate([x1 * c - x2 * s, x1 * s + x2 * c, x[2 * half:]], axis=0)


def _even_in_body(h_ref, g_ref, wn_ref, wt_ref, ca_ref, sal_ref, sah_ref, ci_ref, sil_ref, sih_ref,
                  cat_ref, sat_ref, cit_ref, sit_ref,
                  z_ref, xbc_ref, small_ref, k_ref, ki_ref, qt_ref, vt_ref, qit_ref, lohi_ref, dtt_ref):
    xn = _rms(h_ref[...], g_ref[...])
    xnb = xn.astype(MXU_DTYPE)
    xnt = xn.T.astype(MXU_DTYPE)

    z_ref[...] = _dot(xnb, wn_ref[:, _N_Z:_N_XBC])
    xbc_ref[...] = _dot(xnb, wn_ref[:, _N_XBC:_N_K])
    kk = _dot(xnb, wn_ref[:, _N_K:_N_SMALL])
    ca, sal, sah = ca_ref[...], sal_ref[...], sah_ref[...]
    for hd in range(ATT_HEADS):
        sl = slice(hd * ATT_HEAD_DIM, (hd + 1) * ATT_HEAD_DIM)
        k_ref[hd] = _rope_lanes(kk[:, sl], ca, sal, sah, ATT_ROT_HALF).astype(k_ref.dtype)
    sm = _dot(xnb, wn_ref[:, _N_SMALL:_N_END])
    small_ref[...] = sm
    smr = _rope_lanes(sm, ci_ref[...], sil_ref[...], sih_ref[...], IDX_ROT_HALF)
    ki_ref[...] = smr[:, _SM_KI:_SM_KI + IDX_HEAD_DIM].astype(ki_ref.dtype)

    att_scale = ATT_HEAD_DIM ** -0.5 * math.log2(math.e)
    qt = _dot(wt_ref[_T_Q:_T_V, :], xnt)
    cat, sat = cat_ref[...], sat_ref[...]
    def put(ref, where, val):
        tb = ref.shape[-1]
        for b in range(ref.shape[0]):
            ref[(b, *where, slice(None))] = val[:, b * tb:(b + 1) * tb].astype(ref.dtype)

    vt = _dot(wt_ref[_T_V:_T_QI, :], xnt)
    for hd in range(ATT_HEADS):
        sl = slice(hd * ATT_HEAD_DIM, (hd + 1) * ATT_HEAD_DIM)
        put(qt_ref, (hd, slice(None)), _rope_rows(qt[sl], cat, sat, ATT_ROT_HALF) * att_scale)
        put(vt_ref, (hd, slice(None)), vt[sl])
    tail = _dot(wt_ref[_T_WI:_T_END, :], xnt)
    wit = tail[0:IDX_HEADS] * (IDX_HEADS ** -0.5 * IDX_HEAD_DIM ** -0.5)
    dtt_ref[...] = tail[_T_DT - _T_WI:_T_END - _T_WI]
    pos = wit > 0.0
    put(lohi_ref, (slice(0, 8),), jnp.where(pos, 0.0, -jnp.inf))
    put(lohi_ref, (slice(8, 16),), jnp.where(pos, jnp.inf, 0.0))
    qit = _dot(wt_ref[_T_QI:_T_WI, :], xnt)
    cit, sit = cit_ref[...], sit_ref[...]
    for hd in range(IDX_HEADS):
        sl = slice(hd * IDX_HEAD_DIM, (hd + 1) * IDX_HEAD_DIM)
        put(qit_ref, (sl,), _rope_rows(qit[sl], cit, sit, IDX_ROT_HALF) * wit[hd:hd + 1])


def _even_in(h, g, wn, wt, rope):
    seq = h.shape[0]
    tm = min(ROW_TILE, seq)
    f32 = jnp.float32

    def rows(width):
        return pl.BlockSpec((tm, width), lambda i: (i, 0))

    def cols(height):
        return pl.BlockSpec((height, tm), lambda i: (0, i))

    def blocks(height, tb):
        return pl.BlockSpec((tm // tb, height, tb), lambda i: (i, 0, 0))

    def head_blocks(tb):
        return pl.BlockSpec((tm // tb, ATT_HEADS, ATT_HEAD_DIM, tb), lambda i: (i, 0, 0, 0))

    assert tm % DSA_K_TILE == 0 and tm % DSA_Q_TILE == 0
    nq, nk = seq // DSA_Q_TILE, seq // DSA_K_TILE
    out_shape = (
        jax.ShapeDtypeStruct((seq, SSD_INNER), f32),
        jax.ShapeDtypeStruct((seq, SSD_CONV_DIM), f32),
        jax.ShapeDtypeStruct((seq, LANES), f32),
        jax.ShapeDtypeStruct((ATT_HEADS, seq, ATT_HEAD_DIM), MXU_DTYPE),
        jax.ShapeDtypeStruct((seq, IDX_HEAD_DIM), MXU_DTYPE),
        jax.ShapeDtypeStruct((nq, ATT_HEADS, ATT_HEAD_DIM, DSA_Q_TILE), MXU_DTYPE),
        jax.ShapeDtypeStruct((nk, ATT_HEADS, ATT_HEAD_DIM, DSA_K_TILE), MXU_DTYPE),
        jax.ShapeDtypeStruct((nq, IDX_INNER, DSA_Q_TILE), MXU_DTYPE),
        jax.ShapeDtypeStruct((nq, 16, DSA_Q_TILE), f32),
        jax.ShapeDtypeStruct((SSD_HEADS, seq), f32),
    )
    out_specs = (rows(SSD_INNER), rows(SSD_CONV_DIM), rows(LANES),
                 pl.BlockSpec((ATT_HEADS, tm, ATT_HEAD_DIM), lambda i: (0, i, 0)), rows(IDX_HEAD_DIM),
                 head_blocks(DSA_Q_TILE), head_blocks(DSA_K_TILE), blocks(IDX_INNER, DSA_Q_TILE),
                 blocks(16, DSA_Q_TILE), cols(SSD_HEADS))
    in_specs = [rows(D_MODEL), _const_spec((1, D_MODEL)), _const_spec(wn.shape), _const_spec(wt.shape)]
    in_specs += [rows(LANES)] * 6
    in_specs += [cols(ATT_ROT_HALF)] * 2 + [cols(IDX_ROT_HALF)] * 2
    return pl.pallas_call(
        _even_in_body,
        out_shape=out_shape,
        grid=(seq // tm,),
        in_specs=in_specs,
        out_specs=out_specs,
        compiler_params=_params(("parallel",)),
        name="even_in",
    )(h, g, wn, wt, *rope)


def _ssd_body(xbc_ref, small_ref, dtt_ref, z_ref, cw_ref, cb_ref, dtb_ref, dtbt_ref, a_ref, at_ref,
              drep_ref, gn_ref, o_ref, tail_scr, st_scr, y_scr, xdec_scr, decay_scr):
    q = SSD_CHUNK
    f32 = jnp.float32

    @pl.when(pl.program_id(0) == 0)
    def _():
        tail_scr[...] = jnp.zeros_like(tail_scr)
        st_scr[...] = jnp.zeros_like(st_scr)

    x = xbc_ref[...]
    tail = tail_scr[...]
    row8 = lax.broadcasted_iota(jnp.int32, (8, SSD_CONV_DIM), 0)
    conv = x * cw_ref[SSD_CONV - 1:SSD_CONV, :] + cb_ref[...]
    for shift in range(1, SSD_CONV):
        rolled = pltpu.roll(x, shift, 0)
        head = jnp.where(row8 < shift, pltpu.roll(tail, shift, 0), rolled[0:8])
        shifted = jnp.concatenate([head, rolled[8:]], axis=0)
        conv = conv + shifted * cw_ref[SSD_CONV - 1 - shift:SSD_CONV - shift, :]
    tail_scr[...] = x[q - 8:q]
    xbc = conv * jax.nn.sigmoid(conv)
    xs = xbc[:, :SSD_INNER]
    bm = xbc[:, SSD_INNER:SSD_INNER + SSD_GROUPS * SSD_STATE]
    cm = xbc[:, SSD_INNER + SSD_GROUPS * SSD_STATE:]

    dt_col = _softplus(small_ref[:, _SM_DT:_SM_DT + SSD_HEADS] + dtb_ref[...])
    dt_row = _softplus(dtt_ref[...] + dtbt_ref[...])
    adt_col = dt_col * (-jnp.exp(a_ref[...]))
    adt_row = dt_row * (-jnp.exp(at_ref[...]))
    ri = lax.broadcasted_iota(jnp.int32, (q, q), 0)
    ci = lax.broadcasted_iota(jnp.int32, (q, q), 1)
    causal = ri >= ci
    tril = jnp.where(causal, 1.0, 0.0).astype(MXU_DTYPE)
    triu = jnp.where(ri <= ci, 1.0, 0.0).astype(MXU_DTYPE)
    cs_col = sum(_dot(tril, p) for p in _split3(adt_col))
    cs_row = sum(_dot(p, triu) for p in _split3(adt_row))

    cmb = cm.astype(MXU_DTYPE)
    bmb = bm.astype(MXU_DTYPE)
    for g in range(SSD_GROUPS):
        gs = slice(g * SSD_STATE, (g + 1) * SSD_STATE)
        cb = lax.dot_general(cmb[:, gs], bmb[:, gs], (((1,), (1,)), ((), ())),
                             preferred_element_type=f32)
        kpg = SSD_HEADS // SSD_GROUPS
        for hd in range(g * kpg, (g + 1) * kpg):
            hs = slice(hd * SSD_HEAD_DIM, (hd + 1) * SSD_HEAD_DIM)
            col = cs_col[:, hd:hd + 1]
            row = cs_row[hd:hd + 1, :]
            last = cs_row[hd:hd + 1, q - 1:q]
            decay = jnp.exp(jnp.where(causal, col - row, -jnp.inf))
            xdt = xs[:, hs] * dt_col[:, hd:hd + 1]
            y = _dot((cb * decay).astype(MXU_DTYPE), xdt.astype(MXU_DTYPE))
            y_off = _dot(cmb[:, gs], st_scr[:, hs].astype(MXU_DTYPE))
            y_scr[:, hs] = y + y_off * jnp.exp(col)
            xdec_scr[:, hs] = (xdt * jnp.exp(last - col)).astype(xdec_scr.dtype)
            decay_scr[:, hs] = jnp.broadcast_to(jnp.exp(last), (1, SSD_HEAD_DIM))
        ws = slice(g * kpg * SSD_HEAD_DIM, (g + 1) * kpg * SSD_HEAD_DIM)
        bmt = bm[:, gs].T.astype(MXU_DTYPE)
        st_scr[:, ws] = st_scr[:, ws] * decay_scr[:, ws] + _dot(bmt, xdec_scr[:, ws])

    z = z_ref[...]
    y = (y_scr[...] + drep_ref[...] * xs) * (z * jax.nn.sigmoid(z))
    gw = SSD_INNER // SSD_GROUPS
    for g in range(SSD_GROUPS):
        sl = slice(g * gw, (g + 1) * gw)
        seg = y[:, sl]
        seg = seg * lax.rsqrt(jnp.mean(seg * seg, axis=-1, keepdims=True) + EPS)
        o_ref[:, sl] = (seg * gn_ref[:, sl]).astype(o_ref.dtype)


def _ssd(xbc, small, dtt, z, conv_w, conv_b, dt_bias, a_log, d_skip, gate_norm):
    seq = xbc.shape[0]
    q = SSD_CHUNK
    f32 = jnp.float32

    def rows(width):
        return pl.BlockSpec((q, width), lambda i: (i, 0))

    drep = jnp.repeat(d_skip, SSD_HEAD_DIM)[None, :]
    return pl.pallas_call(
        _ssd_body,
        out_shape=jax.ShapeDtypeStruct((seq, SSD_INNER), MXU_DTYPE),
        grid=(seq // q,),
        in_specs=[rows(SSD_CONV_DIM), rows(LANES), pl.BlockSpec((SSD_HEADS, q), lambda i: (0, i)),
                  rows(SSD_INNER), _const_spec((SSD_CONV, SSD_CONV_DIM)), _const_spec((1, SSD_CONV_DIM)),
                  _const_spec((1, SSD_HEADS)), _const_spec((SSD_HEADS, 1)),
                  _const_spec((1, SSD_HEADS)), _const_spec((SSD_HEADS, 1)),
                  _const_spec((1, SSD_INNER)), _const_spec((1, SSD_INNER))],
        out_specs=rows(SSD_INNER),
        scratch_shapes=[pltpu.VMEM((8, SSD_CONV_DIM), f32),
                        pltpu.VMEM((SSD_STATE, SSD_INNER), f32),
                        pltpu.VMEM((q, SSD_INNER), f32),
                        pltpu.VMEM((q, SSD_INNER), MXU_DTYPE),
                        pltpu.VMEM((1, SSD_INNER), f32)],
        compiler_params=_params(("arbitrary",)),
        name="ssd",
    )(xbc, small, dtt, z, conv_w, conv_b[None, :], dt_bias[None, :], dt_bias[:, None],
      a_log[None, :], a_log[:, None], drep, gate_norm[None, :])


def _dsa_body(qi_ref, kj_ref, qit_ref, lohi_ref, ki_ref, qt_ref, k_ref, vt_ref, o_ref,
              keys_scr, lead_scr, y_scr, thr_scr, m_scr, l_scr, acc_scr, logit_scr, top_scr, *, top_k, idx_bits):
    tq, tk, ck = DSA_Q_TILE, DSA_K_TILE, DSA_CHUNK
    i32 = jnp.int32
    p = pl.program_id(0)
    qi = qi_ref[p]
    kj = kj_ref[p]

    @pl.when(kj == 0)
    def _select():
        chunks_per_tile = tk // ck
        n_tiles = ((qi + 1) * tq + tk - 1) // tk
        n_chunks = n_tiles * chunks_per_tile
        lo = lohi_ref[0:8, :]
        hi = lohi_ref[8:16, :]
        rows = 4 * 8
        t_pos = qi * tq + lax.broadcasted_iota(i32, (rows, tq), 1)
        s_off = lax.broadcasted_iota(i32, (ck, tq), 0)
        s_rows = lax.broadcasted_iota(i32, (rows, tq), 0)

        def project(c, slot):
            kic = ki_ref[pl.ds(pl.multiple_of(c * ck, ck), ck), :]
            for hd in range(IDX_HEADS):
                y_scr[slot, hd] = _dot(kic, qit_ref[hd * IDX_HEAD_DIM:(hd + 1) * IDX_HEAD_DIM, :])

        def emit_keys(c, slot, diagonal):
            for g in range(ck // rows):
                r0 = pl.multiple_of(c * ck + g * rows, rows)
                gs = slice(g * rows, (g + 1) * rows)
                sc = None
                for hd in range(IDX_HEADS):
                    term = jnp.minimum(jnp.maximum(y_scr[slot, hd, gs, :], lo[hd:hd + 1]), hi[hd:hd + 1])
                    sc = term if sc is None else sc + term
                bits = pltpu.bitcast(sc, i32)
                bits = jnp.where((bits & 0x7FFFFFFF) < MIN_NORMAL_EXP, 0, bits)
                key = bits ^ ((bits >> 31) & 0x7FFFFFFF)
                lead = bits & LEAD_MASK
                if diagonal:
                    causal = r0 + s_rows <= t_pos
                    key = jnp.where(causal, key, INT_MIN)
                    lead = jnp.where(causal, lead, QUIET_NAN_BITS)
                keys_scr[pl.ds(r0, rows), :] = key
                lead_scr[pl.ds(r0, rows), :] = pltpu.bitcast(lead, jnp.float32).astype(jnp.bfloat16)

        assert chunks_per_tile % 2 == 0

        def score_tile(t, diagonal):
            for j in range(chunks_per_tile):
                c = t * chunks_per_tile + j
                emit_keys(c, j % 2, diagonal)
                if not (diagonal and j == chunks_per_tile - 1):
                    project(c + 1, (j + 1) % 2)

        project(0, 0)
        lax.fori_loop(0, n_tiles - 1, lambda t, carry: (score_tile(t, False), carry)[1], 0)
        score_tile(n_tiles - 1, True)

        partial_rows = 4 * 8
        s_tile = lax.broadcasted_iota(i32, (partial_rows, tq), 0)

        def key_groups(c):
            for g in range(tk // partial_rows):
                r0 = pl.multiple_of(c * tk + g * partial_rows, partial_rows)
                yield keys_scr[pl.ds(r0, partial_rows), :], r0

        def count(pred):
            def body(c, acc):
                for kk, r0 in key_groups(c):
                    acc = acc + jnp.where(pred(kk, r0), 1.0, 0.0)
                return acc
            acc = lax.fori_loop(0, n_tiles, body, jnp.zeros((partial_rows, tq), jnp.float32))
            return acc.sum(axis=0, keepdims=True)

        def count_lead(cand_code):
            cand_code = jnp.where((cand_code > 0) & (cand_code < MIN_NORMAL_CODE), MIN_NORMAL_CODE,
                                  jnp.where((cand_code < 0) & (cand_code >= -MIN_NORMAL_CODE), 0, cand_code))
            cand_bits = (cand_code ^ ((cand_code >> 31) & 0x7FFF)) << 16
            cand = jnp.broadcast_to(pltpu.bitcast(cand_bits, jnp.float32), (16, tq)).astype(jnp.bfloat16)
            one, zero = jnp.ones((), jnp.bfloat16), jnp.zeros((), jnp.bfloat16)

            def body(c, acc):
                parts = []
                for g in range(tk // 128):
                    r0 = pl.multiple_of(c * tk + g * 128, 128)
                    lead = lead_scr[pl.ds(r0, 128), :].reshape(8, 16, tq)
                    parts += [jnp.where(lead[j] >= cand, one, zero) for j in range(8)]
                while len(parts) > 1:
                    parts = [a + b for a, b in zip(parts[0::2], parts[1::2])]
                return acc + parts[0].astype(jnp.float32)
            acc = lax.fori_loop(0, n_tiles, body, jnp.zeros((16, tq), jnp.float32))
            return acc.sum(axis=0, keepdims=True)

        k_f = jnp.float32(top_k)
        n_t = (qi * tq + 1 + lax.broadcasted_iota(i32, (1, tq), 1)).astype(jnp.float32)

        def lead_step(b, carry):
            code, c_lo, c_hi = carry
            cand = code + lax.shift_left(i32(1), i32(15) - b)
            cnt = count_lead(cand)
            ok = cnt >= k_f
            return jnp.where(ok, cand, code), jnp.where(ok, cnt, c_lo), jnp.where(ok, c_hi, cnt)

        code, c_lo, c_hi = lax.fori_loop(
            0, 16, lead_step, (jnp.full((1, tq), -(2 ** 15), i32), n_t, jnp.zeros((1, tq), jnp.float32)))

        lo_k = jnp.maximum(code << 16, INT_MIN + 1)
        hi_k = jnp.where(code == 2 ** 15 - 1, jnp.iinfo(i32).max, (code + 1) << 16)

        def finished(lo_k, hi_k, c_lo):
            return (c_lo <= k_f) | (lo_k + 1 >= hi_k)

        def key_value(kk):
            kk = jnp.clip(kk, -FINITE_KEY, FINITE_KEY)
            return pltpu.bitcast(kk ^ ((kk >> 31) & 0x7FFFFFFF), jnp.float32)

        def tighten(lo_k, hi_k, c_lo, c_hi, done, it):
            def body(c, carry):
                mn, mx = carry
                for kk, _ in key_groups(c):
                    mn = jnp.minimum(mn, jnp.where(kk >= lo_k, kk, jnp.iinfo(i32).max))
                    mx = jnp.maximum(mx, jnp.where(kk < hi_k, kk, INT_MIN))
                return mn, mx
            mn, mx = lax.fori_loop(0, n_tiles, body, (jnp.full((partial_rows, tq), jnp.iinfo(i32).max, i32),
                                                      jnp.full((partial_rows, tq), INT_MIN, i32)))
            mn, mx = mn.min(axis=0, keepdims=True), mx.max(axis=0, keepdims=True)
            return jnp.where(done, lo_k, mn), jnp.where(done, hi_k, mx + 1), c_lo, c_hi

        def probe(lo_k, hi_k, c_lo, c_hi, done, it):
            below, span = c_lo - k_f, c_lo - c_hi
            end = jnp.where(below + below >= span, hi_k - 1, lo_k + 1)
            lo_v, hi_v = key_value(lo_k), key_value(hi_k)
            guess_v = lo_v + (below + 0.5) / jnp.maximum(span, 1.0) * (hi_v - lo_v)
            guess_b = pltpu.bitcast(guess_v, i32)
            guess = guess_b ^ ((guess_b >> 31) & 0x7FFFFFFF)
            mid = (lo_k >> 1) + (hi_k >> 1) + (lo_k & hi_k & 1)
            phase = it % SEARCH_ROUND
            cand = jnp.where(phase == 1, end, jnp.where(phase == 3, mid, guess))
            cand = jnp.where(done, lo_k, jnp.maximum(lo_k + 1, jnp.minimum(hi_k - 1, cand)))
            cnt = count(lambda kk, r0: kk >= cand)
            up = jnp.logical_and(~done, cnt >= k_f)
            down = jnp.logical_and(~done, cnt < k_f)
            return (jnp.where(up, cand, lo_k), jnp.where(down, cand, hi_k),
                    jnp.where(up, cnt, c_lo), jnp.where(down, cnt, c_hi))

        def search_cond(carry):
            return jnp.logical_and(carry[0] < SEARCH_ROUND * 32, carry[1] > 0)

        def search_step(carry):
            it, _, lo_k, hi_k, c_lo, c_hi = carry
            done = finished(lo_k, hi_k, c_lo)
            lo_k, hi_k, c_lo, c_hi = lax.cond(it % SEARCH_ROUND == 0, tighten, probe, lo_k, hi_k, c_lo, c_hi, done, it)
            still = jnp.where(finished(lo_k, hi_k, c_lo), 0.0, 1.0)
            return it + 1, jnp.max(still).astype(i32), lo_k, hi_k, c_lo, c_hi

        active = jnp.max(jnp.where(finished(lo_k, hi_k, c_lo), 0.0, 1.0)).astype(i32)
        _, _, thr, _, n_ge, n_gt = lax.while_loop(search_cond, search_step, (i32(0), active, lo_k, hi_k, c_lo, c_hi))
        thr_scr[0:1, :] = thr

        tied = n_ge > k_f
        need = jnp.where(tied, k_f - n_gt, 0.0)
        max_need = jnp.max(need)

        @pl.when(max_need > 0.0)
        def _ties():
            int_max = jnp.iinfo(i32).max

            def by_extraction(_):
                def next_tie(j, cut):
                    def body(c, mn):
                        for kk, r0 in key_groups(c):
                            idx = r0 + s_tile
                            mn = jnp.minimum(mn, jnp.where((kk == thr) & (idx > cut), idx, int_max))
                        return mn
                    mn = lax.fori_loop(0, n_tiles, body, jnp.full((partial_rows, tq), int_max, i32))
                    return jnp.where(j.astype(jnp.float32) < need, mn.min(axis=0, keepdims=True), cut)
                return lax.fori_loop(0, max_need.astype(i32), next_tie, jnp.full((1, tq), -1, i32))

            def by_index_bits(_):
                def idx_step(b, cut):
                    cand = cut + lax.shift_left(i32(1), i32(idx_bits - 1) - b)
                    cnt = count(lambda kk, r0: jnp.where(kk == thr, r0 + s_tile, cand) < cand)
                    return jnp.where(cnt < need, cand, cut)
                return lax.fori_loop(0, idx_bits, idx_step, jnp.zeros((1, tq), i32))

            cut = lax.cond(max_need <= TIE_EXTRACT_MAX, by_extraction, by_index_bits, 0)
            cut = jnp.where(tied, cut, int_max)

            def drop(c, carry):
                r0 = pl.multiple_of(c * ck, ck)
                kk = keys_scr[pl.ds(r0, ck), :]
                excess = jnp.where(kk == thr, r0 + s_off, INT_MIN) > cut
                keys_scr[pl.ds(r0, ck), :] = jnp.where(excess, INT_MIN, kk)
                return carry

            lax.fori_loop(0, n_chunks, drop, 0)

        m_scr[...] = jnp.full_like(m_scr, NEG_BIG)
        l_scr[...] = jnp.zeros_like(l_scr)
        acc_scr[...] = jnp.zeros_like(acc_scr)
        top_scr[...] = jnp.full_like(top_scr, -jnp.inf)
        logit_scr[1] = jnp.full(logit_scr.shape[1:], -jnp.inf, jnp.float32)

    n_att = ((qi + 1) * tq - 1) // tk + 1
    row0 = pl.multiple_of(jnp.minimum(kj, n_att - 1) * tk, tk)
    thr_now = jnp.where(kj < n_att, thr_scr[0:1, :], jnp.iinfo(i32).max)
    bias = jnp.where(keys_scr[pl.ds(row0, tk), :] >= thr_now, 0.0, -jnp.inf)

    def attend(cur, prev):
        m_old = m_scr[...]
        m_new = jnp.maximum(m_old, top_scr[...])
        alpha = jnp.exp2(m_old - m_new)
        m_scr[...] = m_new
        col_sum, col_max = [], []
        for hd in range(ATT_HEADS):
            prob = jnp.exp2(logit_scr[prev, hd] - m_new[hd:hd + 1])
            col_sum.append(jnp.sum(prob, axis=0, keepdims=True))
            acc_scr[hd] = alpha[hd:hd + 1] * acc_scr[hd] + _dot(vt_ref[hd], prob.astype(MXU_DTYPE))
            logit = _dot(k_ref[hd], qt_ref[hd]) + bias
            logit_scr[cur, hd] = logit
            col_max.append(jnp.max(logit, axis=0, keepdims=True))
        l_scr[...] = alpha * l_scr[...] + jnp.concatenate(col_sum, axis=0)
        top_scr[...] = jnp.concatenate(col_max, axis=0)

    for parity in range(2):
        pl.when(kj % 2 == parity)(functools.partial(attend, parity, 1 - parity))

    @pl.when(kj == n_att)
    def _finish():
        for hd in range(ATT_HEADS):
            hs = slice(hd * ATT_HEAD_DIM, (hd + 1) * ATT_HEAD_DIM)
            o_ref[:, hs] = (acc_scr[hd] / l_scr[hd:hd + 1, :]).T.astype(o_ref.dtype)


def _dsa(qit, lohi, ki, qt, k, vt):
    seq = k.shape[1]
    tq, tk = DSA_Q_TILE, DSA_K_TILE
    assert seq % tk == 0 and tk % tq == 0 and tq % DSA_CHUNK == 0
    assert k.shape == (ATT_HEADS, seq, ATT_HEAD_DIM)
    assert qt.shape == (seq // tq, ATT_HEADS, ATT_HEAD_DIM, tq) and vt.shape == (seq // tk, ATT_HEADS, ATT_HEAD_DIM, tk)
    nq = seq // tq
    def n_att(i):
        return ((i + 1) * tq - 1) // tk + 1

    pairs = [(i, j) for i in range(nq) for j in range(n_att(i) + 1)]
    qi_idx = jnp.asarray([pq for pq, _ in pairs], jnp.int32)
    kj_idx = jnp.asarray([pk for _, pk in pairs], jnp.int32)
    top_k = min(TOPK_MAX, seq // 4)
    idx_bits = max(1, (seq - 1).bit_length())
    grid_spec = pltpu.PrefetchScalarGridSpec(
        num_scalar_prefetch=2,
        grid=(len(pairs),),
        in_specs=[
            pl.BlockSpec((None, IDX_INNER, tq), lambda p, qi, kj: (qi[p], 0, 0)),
            pl.BlockSpec((None, 16, tq), lambda p, qi, kj: (qi[p], 0, 0)),
            pl.BlockSpec((seq, IDX_HEAD_DIM), lambda p, qi, kj: (0, 0), pipeline_mode=pl.Buffered(1)),
            pl.BlockSpec((None, ATT_HEADS, ATT_HEAD_DIM, tq), lambda p, qi, kj: (qi[p], 0, 0, 0)),
            pl.BlockSpec((ATT_HEADS, tk, ATT_HEAD_DIM),
                         lambda p, qi, kj: (0, jnp.minimum(kj[p], n_att(qi[p]) - 1), 0)),
            pl.BlockSpec((None, ATT_HEADS, ATT_HEAD_DIM, tk),
                         lambda p, qi, kj: (jnp.maximum(kj[p] - 1, 0), 0, 0, 0)),
        ],
        out_specs=pl.BlockSpec((tq, ATT_INNER), lambda p, qi, kj: (qi[p], 0)),
        scratch_shapes=[pltpu.VMEM((seq, tq), jnp.int32),
                        pltpu.VMEM((seq, tq), jnp.bfloat16),
                        pltpu.VMEM((2, IDX_HEADS, DSA_CHUNK, tq), jnp.float32),
                        pltpu.VMEM((8, tq), jnp.int32),
                        pltpu.VMEM((ATT_HEADS, tq), jnp.float32),
                        pltpu.VMEM((ATT_HEADS, tq), jnp.float32),
                        pltpu.VMEM((ATT_HEADS, ATT_HEAD_DIM, tq), jnp.float32),
                        pltpu.VMEM((2, ATT_HEADS, tk, tq), jnp.float32),
                        pltpu.VMEM((ATT_HEADS, tq), jnp.float32)],
    )
    return pl.pallas_call(
        functools.partial(_dsa_body, top_k=top_k, idx_bits=idx_bits),
        out_shape=jax.ShapeDtypeStruct((seq, ATT_INNER), MXU_DTYPE),
        grid_spec=grid_spec,
        compiler_params=_params(("arbitrary",)),
        name="dsa",
    )(qi_idx, kj_idx, qit, lohi, ki, qt, k, vt)


def _even_out_body(h_ref, y_ref, o_ref, wy_ref, wo_ref, out_ref):
    out_ref[...] = h_ref[...] + _dot(y_ref[...], wy_ref[...]) + _dot(o_ref[...], wo_ref[...])


def _even_out(h, y, o, wy, wo):
    seq = h.shape[0]
    tm = min(ROW_TILE, seq)
    row = pl.BlockSpec((tm, D_MODEL), lambda i: (i, 0))
    return pl.pallas_call(
        _even_out_body,
        out_shape=jax.ShapeDtypeStruct((seq, D_MODEL), jnp.float32),
        grid=(seq // tm,),
        in_specs=[row, row, row, _const_spec((SSD_INNER, D_MODEL)), _const_spec((ATT_INNER, D_MODEL))],
        out_specs=row,
        compiler_params=_params(("parallel",)),
        name="even_out",
    )(h, y, o, wy, wo)


def _odd_body(h_ref, g_ref, win_ref, lng_ref, lnb_ref, ws_ref, bs_ref, wout_ref, o_ref, gated_scr):
    h = h_ref[...]
    tm = h.shape[0]
    xn = _rms(h, g_ref[...]).astype(MXU_DTYPE)
    u = jax.nn.gelu(_dot(xn, win_ref[:, :SG_INNER]))
    v = jax.nn.gelu(_dot(xn, win_ref[:, SG_INNER:]))
    mu = jnp.mean(v, axis=-1, keepdims=True)
    vc = v - mu
    var = jnp.mean(vc * vc, axis=-1, keepdims=True)
    v = (vc * lax.rsqrt(var + EPS) * lng_ref[...] + lnb_ref[...]).astype(MXU_DTYPE)
    ri = lax.broadcasted_iota(jnp.int32, (SG_CHUNK, SG_CHUNK), 0)
    ci = lax.broadcasted_iota(jnp.int32, (SG_CHUNK, SG_CHUNK), 1)
    gw = SG_INNER // SG_GROUPS
    for g in range(SG_GROUPS):
        w = jnp.where(ri >= ci, ws_ref[g], 0.0).astype(MXU_DTYPE)
        gs = slice(g * gw, (g + 1) * gw)
        for c in range(tm // SG_CHUNK):
            rs = slice(c * SG_CHUNK, (c + 1) * SG_CHUNK)
            mixed = _dot(w, v[rs, gs]) + bs_ref[:, gs]
            gated_scr[rs, gs] = (u[rs, gs] * mixed).astype(gated_scr.dtype)
    o_ref[...] = h + _dot(gated_scr[...], wout_ref[...])


def _odd(h, g, win, ln_g, ln_b, w_s, bs_full, wout):
    seq = h.shape[0]
    tm = min(ROW_TILE, seq)
    row = pl.BlockSpec((tm, D_MODEL), lambda i: (i, 0))
    return pl.pallas_call(
        _odd_body,
        out_shape=jax.ShapeDtypeStruct((seq, D_MODEL), jnp.float32),
        grid=(seq // tm,),
        in_specs=[row, _const_spec((1, D_MODEL)), _const_spec((D_MODEL, 2 * SG_INNER)),
                  _const_spec((1, SG_INNER)), _const_spec((1, SG_INNER)),
                  _const_spec((SG_GROUPS, SG_CHUNK, SG_CHUNK)), _const_spec((SG_CHUNK, SG_INNER)),
                  _const_spec((SG_INNER, D_MODEL))],
        out_specs=row,
        scratch_shapes=[pltpu.VMEM((tm, SG_INNER), MXU_DTYPE)],
        compiler_params=_params(("parallel",)),
        name="odd",
    )(h, g, win, ln_g, ln_b, w_s, bs_full, wout)


def _rope_inputs(seq):
    def tables(rot_dim):
        inv = 1.0 / (ROPE_THETA ** (jnp.arange(0, rot_dim, 2, dtype=jnp.float32) / rot_dim))
        ang = jnp.arange(seq, dtype=jnp.float32)[:, None] * inv[None, :]
        return jnp.cos(ang), jnp.sin(ang)

    def lane_tables(cos, sin):
        half = cos.shape[1]
        pad = LANES - 2 * half
        c = jnp.concatenate([cos, cos, jnp.ones((seq, pad), jnp.float32)], axis=1)
        s_lo = jnp.concatenate([-sin, jnp.zeros((seq, LANES - half), jnp.float32)], axis=1)
        s_hi = jnp.concatenate([jnp.zeros((seq, half), jnp.float32), sin,
                                jnp.zeros((seq, pad), jnp.float32)], axis=1)
        return c, s_lo, s_hi

    cos_a, sin_a = tables(ATT_HEAD_DIM // ROPE_FRACTION)
    cos_i, sin_i = tables(IDX_HEAD_DIM // ROPE_FRACTION)
    return (*lane_tables(cos_a, sin_a), *lane_tables(cos_i, sin_i), cos_a.T, sin_a.T, cos_i.T, sin_i.T)


def _even_weights(w_in):
    offs = np.cumsum((SSD_INNER, SSD_CONV_DIM, SSD_HEADS, ATT_INNER, ATT_INNER, ATT_INNER,
                      IDX_INNER, IDX_HEAD_DIM, IDX_HEADS))[:-1].tolist()
    z, xbc, dt, q, k, v, qi, ki, wi = jnp.split(w_in, offs, axis=-1)
    pad = jnp.zeros((D_MODEL, LANES - IDX_HEAD_DIM - SSD_HEADS), w_in.dtype)
    wn = jnp.concatenate([z, xbc, k, ki, dt, pad], axis=1).astype(MXU_DTYPE)
    wi_pad = jnp.zeros((D_MODEL, _T_DT - _T_WI - IDX_HEADS), w_in.dtype)
    wt = jnp.concatenate([q, v, qi, wi, wi_pad, dt], axis=1).T.astype(MXU_DTYPE)
    assert wn.shape == (D_MODEL, _N_END) and wt.shape == (_T_END, D_MODEL)
    return wn, wt


def _even_mixer(h, g, w_in, conv_w, conv_b, dt_bias, a_log, d_skip, gate_norm, w_out, rope):
    wn, wt = _even_weights(w_in)
    z, xbc, small, k, ki, qt, vt, qit, lohi, dtt = _even_in(h, g, wn, wt, rope)
    y = _ssd(xbc, small, dtt, z, conv_w, conv_b, dt_bias, a_log, d_skip, gate_norm)
    o = _dsa(qit, lohi, ki, qt, k, vt)
    w_out = w_out.astype(MXU_DTYPE)
    return _even_out(h, y, o, w_out[:SSD_INNER], w_out[SSD_INNER:])


def _forward(x, norm_g, final_g, ffn_w_gu, ffn_w_down, ev_w_in, ev_conv_w, ev_conv_b, ev_dt_bias,
             ev_a_log, ev_d, ev_gate_norm, ev_w_out, od_w_in, od_ln_g, od_ln_b, od_w_s, od_b_s, od_w_out):
    bsz, seq, _ = x.shape
    depth = norm_g.shape[0]
    rope = _rope_inputs(seq)
    wg = ffn_w_gu[..., :FFN_HIDDEN].astype(MXU_DTYPE)
    wu = ffn_w_gu[..., FFN_HIDDEN:].astype(MXU_DTYPE)
    wd = ffn_w_down.astype(MXU_DTYPE)
    fg = final_g[None, :]
    outs = []
    for b in range(bsz):
        h = x[b]
        for layer in range(depth):
            j = layer // 2
            h = _ffn(h, norm_g[layer, 0][None, :], wg[layer, 0], wu[layer, 0], wd[layer, 0], fg, False)
            g1 = norm_g[layer, 1][None, :]
            if layer % 2 == 0:
                h = _even_mixer(h, g1, ev_w_in[j], ev_conv_w[j], ev_conv_b[j], ev_dt_bias[j], ev_a_log[j],
                                ev_d[j], ev_gate_norm[j], ev_w_out[j], rope)
            else:
                bs_full = jnp.repeat(od_b_s[j].T, SG_INNER // SG_GROUPS, axis=1)
                h = _odd(h, g1, od_w_in[j].astype(MXU_DTYPE), od_ln_g[j][None, :], od_ln_b[j][None, :],
                         od_w_s[j], bs_full, od_w_out[j].astype(MXU_DTYPE))
            h = _ffn(h, norm_g[layer, 2][None, :], wg[layer, 1], wu[layer, 1], wd[layer, 1], fg,
                     layer == depth - 1)
        outs.append(h)
    return outs[0][None] if bsz == 1 else jnp.stack(outs, axis=0)


def kernel(x, norm_g, final_g, ffn_w_gu, ffn_w_down, ev_w_in, ev_conv_w, ev_conv_b, ev_dt_bias, ev_a_log,
           ev_d, ev_gate_norm, ev_w_out, od_w_in, od_ln_g, od_ln_b, od_w_s, od_b_s, od_w_out):
    return _forward(x, norm_g, final_g, ffn_w_gu, ffn_w_down, ev_w_in, ev_conv_w, ev_conv_b, ev_dt_bias,
                    ev_a_log, ev_d, ev_gate_norm, ev_w_out, od_w_in, od_ln_g, od_ln_b, od_w_s, od_b_s,
                    od_w_out)
```

```python
import functools
import math

import numpy as np
import jax
import jax.numpy as jnp
from jax import lax
from jax.experimental import pallas as pl
from jax.experimental.pallas import tpu as pltpu

D_MODEL = 1024
SSD_HEADS = 16
SSD_HEAD_DIM = 64
SSD_INNER = SSD_HEADS * SSD_HEAD_DIM
SSD_GROUPS = 2
SSD_STATE = 128
SSD_CONV = 4
SSD_CHUNK = 256
SSD_CONV_DIM = SSD_INNER + 2 * SSD_GROUPS * SSD_STATE
ATT_HEADS = 8
ATT_HEAD_DIM = 128
ATT_INNER = ATT_HEADS * ATT_HEAD_DIM
IDX_HEADS = 8
IDX_HEAD_DIM = 64
IDX_INNER = IDX_HEADS * IDX_HEAD_DIM
TOPK_MAX = 256
SG_CHUNK = 128
SG_GROUPS = 8
SG_INNER = 2 * D_MODEL
FFN_HIDDEN = 2816
ROPE_THETA = 500000.0
ROPE_FRACTION = 4
EPS = 1e-6

ATT_ROT_HALF = ATT_HEAD_DIM // ROPE_FRACTION // 2
IDX_ROT_HALF = IDX_HEAD_DIM // ROPE_FRACTION // 2

MXU_DTYPE = jnp.bfloat16
LANES = 128
VMEM_LIMIT_BYTES = 56 * 1024 * 1024

ROW_TILE = 512
FFN_HIDDEN_CHUNK = FFN_HIDDEN // 11
DSA_Q_TILE = 256
DSA_K_TILE = 512
DSA_CHUNK = 256
TIE_EXTRACT_MAX = 8
SEARCH_ROUND = 5
LEAD_MASK = -(2 ** 16)
QUIET_NAN_BITS = 0x7FC00000
FINITE_KEY = 0x7F7FFFFF
MIN_NORMAL_EXP = 0x00800000
MIN_NORMAL_CODE = MIN_NORMAL_EXP >> 16
INT_MIN = -(2 ** 31)
NEG_BIG = -0.7 * float(np.finfo(np.float32).max)

_N_Z = 0
_N_XBC = _N_Z + SSD_INNER
_N_K = _N_XBC + SSD_CONV_DIM
_N_SMALL = _N_K + ATT_INNER
_N_END = _N_SMALL + LANES
_SM_KI = 0
_SM_DT = IDX_HEAD_DIM
_T_Q = 0
_T_V = _T_Q + ATT_INNER
_T_QI = _T_V + ATT_INNER
_T_WI = _T_QI + IDX_INNER
_T_DT = _T_WI + 16
_T_END = _T_DT + SSD_HEADS


def _dot(a, b):
    return jnp.dot(a, b, preferred_element_type=jnp.float32)


def _rms(x, g):
    return x * lax.rsqrt(jnp.mean(x * x, axis=-1, keepdims=True) + EPS) * g


def _softplus(x):
    return jnp.maximum(x, 0.0) + jnp.log1p(jnp.exp(-jnp.abs(x)))


def _split3(x):
    hi = x.astype(MXU_DTYPE)
    r = x - hi.astype(jnp.float32)
    mid = r.astype(MXU_DTYPE)
    lo = (r - mid.astype(jnp.float32)).astype(MXU_DTYPE)
    return hi, mid, lo


def _const_spec(shape):
    zeros = (0,) * len(shape)
    return pl.BlockSpec(shape, lambda *_: zeros, pipeline_mode=pl.Buffered(1))


def _params(sem):
    return pltpu.CompilerParams(dimension_semantics=sem, vmem_limit_bytes=VMEM_LIMIT_BYTES)


def _ffn_body(h_ref, g_ref, wg_ref, wu_ref, wd_ref, fg_ref, o_ref, *, final):
    h = h_ref[...]
    xn = _rms(h, g_ref[...]).astype(MXU_DTYPE)
    acc = None
    for c in range(FFN_HIDDEN // FFN_HIDDEN_CHUNK):
        sl = slice(c * FFN_HIDDEN_CHUNK, (c + 1) * FFN_HIDDEN_CHUNK)
        gate = _dot(xn, wg_ref[:, sl])
        up = _dot(xn, wu_ref[:, sl])
        act = (gate * jax.nn.sigmoid(gate) * up).astype(MXU_DTYPE)
        part = _dot(act, wd_ref[sl, :])
        acc = part if acc is None else acc + part
    out = h + 0.5 * acc
    if final:
        out = _rms(out, fg_ref[...])
    o_ref[...] = out


def _ffn(h, g, wg, wu, wd, final_g, final):
    seq = h.shape[0]
    tm = min(ROW_TILE, seq)
    row = pl.BlockSpec((tm, D_MODEL), lambda i: (i, 0))
    return pl.pallas_call(
        functools.partial(_ffn_body, final=final),
        out_shape=jax.ShapeDtypeStruct((seq, D_MODEL), jnp.float32),
        grid=(seq // tm,),
        in_specs=[row, _const_spec((1, D_MODEL)), _const_spec((D_MODEL, FFN_HIDDEN)),
                  _const_spec((D_MODEL, FFN_HIDDEN)), _const_spec((FFN_HIDDEN, D_MODEL)),
                  _const_spec((1, D_MODEL))],
        out_specs=row,
        compiler_params=_params(("parallel",)),
        name="ffn",
    )(h, g, wg, wu, wd, final_g)


def _rope_lanes(x, c, s_lo, s_hi, half):
    return x * c + pltpu.roll(x, LANES - half, 1) * s_lo + pltpu.roll(x, half, 1) * s_hi


def _rope_rows(x, c, s, half):
    x1, x2 = x[:half], x[half:2 * half]
    return jnp.concatenate([x1 * c - x2 * s, x1 * s + x2 * c, x[2 * half:]], axis=0)


def _even_in_body(h_ref, g_ref, wn_ref, wt_ref, ca_ref, sal_ref, sah_ref, ci_ref, sil_ref, sih_ref,
                  cat_ref, sat_ref, cit_ref, sit_ref,
                  z_ref, xbc_ref, small_ref, k_ref, ki_ref, qt_ref, vt_ref, qit_ref, lohi_ref, dtt_ref):
    xn = _rms(h_ref[...], g_ref[...])
    xnb = xn.astype(MXU_DTYPE)
    xnt = xn.T.astype(MXU_DTYPE)

    z_ref[...] = _dot(xnb, wn_ref[:, _N_Z:_N_XBC])
    xbc_ref[...] = _dot(xnb, wn_ref[:, _N_XBC:_N_K])
    kk = _dot(xnb, wn_ref[:, _N_K:_N_SMALL])
    ca, sal, sah = ca_ref[...], sal_ref[...], sah_ref[...]
    for hd in range(ATT_HEADS):
        sl = slice(hd * ATT_HEAD_DIM, (hd + 1) * ATT_HEAD_DIM)
        k_ref[hd] = _rope_lanes(kk[:, sl], ca, sal, sah, ATT_ROT_HALF).astype(k_ref.dtype)
    sm = _dot(xnb, wn_ref[:, _N_SMALL:_N_END])
    small_ref[...] = sm
    smr = _rope_lanes(sm, ci_ref[...], sil_ref[...], sih_ref[...], IDX_ROT_HALF)
    ki_ref[...] = smr[:, _SM_KI:_SM_KI + IDX_HEAD_DIM].astype(ki_ref.dtype)

    att_scale = ATT_HEAD_DIM ** -0.5 * math.log2(math.e)
    qt = _dot(wt_ref[_T_Q:_T_V, :], xnt)
    cat, sat = cat_ref[...], sat_ref[...]
    def put(ref, where, val):
        tb = ref.shape[-1]
        for b in range(ref.shape[0]):
            ref[(b, *where, slice(None))] = val[:, b * tb:(b + 1) * tb].astype(ref.dtype)

    vt = _dot(wt_ref[_T_V:_T_QI, :], xnt)
    for hd in range(ATT_HEADS):
        sl = slice(hd * ATT_HEAD_DIM, (hd + 1) * ATT_HEAD_DIM)
        put(qt_ref, (hd, slice(None)), _rope_rows(qt[sl], cat, sat, ATT_ROT_HALF) * att_scale)
        put(vt_ref, (hd, slice(None)), vt[sl])
    tail = _dot(wt_ref[_T_WI:_T_END, :], xnt)
    wit = tail[0:IDX_HEADS] * (IDX_HEADS ** -0.5 * IDX_HEAD_DIM ** -0.5)
    dtt_ref[...] = tail[_T_DT - _T_WI:_T_END - _T_WI]
    pos = wit > 0.0
    put(lohi_ref, (slice(0, 8),), jnp.where(pos, 0.0, -jnp.inf))
    put(lohi_ref, (slice(8, 16),), jnp.where(pos, jnp.inf, 0.0))
    qit = _dot(wt_ref[_T_QI:_T_WI, :], xnt)
    cit, sit = cit_ref[...], sit_ref[...]
    for hd in range(IDX_HEADS):
        sl = slice(hd * IDX_HEAD_DIM, (hd + 1) * IDX_HEAD_DIM)
        put(qit_ref, (sl,), _rope_rows(qit[sl], cit, sit, IDX_ROT_HALF) * wit[hd:hd + 1])


def _even_in(h, g, wn, wt, rope):
    seq = h.shape[0]
    tm = min(ROW_TILE, seq)
    f32 = jnp.float32

    def rows(width):
        return pl.BlockSpec((tm, width), lambda i: (i, 0))

    def cols(height):
        return pl.BlockSpec((height, tm), lambda i: (0, i))

    def blocks(height, tb):
        return pl.BlockSpec((tm // tb, height, tb), lambda i: (i, 0, 0))

    def head_blocks(tb):
        return pl.BlockSpec((tm // tb, ATT_HEADS, ATT_HEAD_DIM, tb), lambda i: (i, 0, 0, 0))

    assert tm % DSA_K_TILE == 0 and tm % DSA_Q_TILE == 0
    nq, nk = seq // DSA_Q_TILE, seq // DSA_K_TILE
    out_shape = (
        jax.ShapeDtypeStruct((seq, SSD_INNER), f32),
        jax.ShapeDtypeStruct((seq, SSD_CONV_DIM), f32),
        jax.ShapeDtypeStruct((seq, LANES), f32),
        jax.ShapeDtypeStruct((ATT_HEADS, seq, ATT_HEAD_DIM), MXU_DTYPE),
        jax.ShapeDtypeStruct((seq, IDX_HEAD_DIM), MXU_DTYPE),
        jax.ShapeDtypeStruct((nq, ATT_HEADS, ATT_HEAD_DIM, DSA_Q_TILE), MXU_DTYPE),
        jax.ShapeDtypeStruct((nk, ATT_HEADS, ATT_HEAD_DIM, DSA_K_TILE), MXU_DTYPE),
        jax.ShapeDtypeStruct((nq, IDX_INNER, DSA_Q_TILE), MXU_DTYPE),
        jax.ShapeDtypeStruct((nq, 16, DSA_Q_TILE), f32),
        jax.ShapeDtypeStruct((SSD_HEADS, seq), f32),
    )
    out_specs = (rows(SSD_INNER), rows(SSD_CONV_DIM), rows(LANES),
                 pl.BlockSpec((ATT_HEADS, tm, ATT_HEAD_DIM), lambda i: (0, i, 0)), rows(IDX_HEAD_DIM),
                 head_blocks(DSA_Q_TILE), head_blocks(DSA_K_TILE), blocks(IDX_INNER, DSA_Q_TILE),
                 blocks(16, DSA_Q_TILE), cols(SSD_HEADS))
    in_specs = [rows(D_MODEL), _const_spec((1, D_MODEL)), _const_spec(wn.shape), _const_spec(wt.shape)]
    in_specs += [rows(LANES)] * 6
    in_specs += [cols(ATT_ROT_HALF)] * 2 + [cols(IDX_ROT_HALF)] * 2
    return pl.pallas_call(
        _even_in_body,
        out_shape=out_shape,
        grid=(seq // tm,),
        in_specs=in_specs,
        out_specs=out_specs,
        compiler_params=_params(("parallel",)),
        name="even_in",
    )(h, g, wn, wt, *rope)


def _ssd_body(xbc_ref, small_ref, dtt_ref, z_ref, cw_ref, cb_ref, dtb_ref, dtbt_ref, a_ref, at_ref,
              drep_ref, gn_ref, o_ref, tail_scr, st_scr, y_scr, xdec_scr, decay_scr):
    q = SSD_CHUNK
    f32 = jnp.float32

    @pl.when(pl.program_id(0) == 0)
    def _():
        tail_scr[...] = jnp.zeros_like(tail_scr)
        st_scr[...] = jnp.zeros_like(st_scr)

    x = xbc_ref[...]
    tail = tail_scr[...]
    row8 = lax.broadcasted_iota(jnp.int32, (8, SSD_CONV_DIM), 0)
    conv = x * cw_ref[SSD_CONV - 1:SSD_CONV, :] + cb_ref[...]
    for shift in range(1, SSD_CONV):
        rolled = pltpu.roll(x, shift, 0)
        head = jnp.where(row8 < shift, pltpu.roll(tail, shift, 0), rolled[0:8])
        shifted = jnp.concatenate([head, rolled[8:]], axis=0)
        conv = conv + shifted * cw_ref[SSD_CONV - 1 - shift:SSD_CONV - shift, :]
    tail_scr[...] = x[q - 8:q]
    xbc = conv * jax.nn.sigmoid(conv)
    xs = xbc[:, :SSD_INNER]
    bm = xbc[:, SSD_INNER:SSD_INNER + SSD_GROUPS * SSD_STATE]
    cm = xbc[:, SSD_INNER + SSD_GROUPS * SSD_STATE:]

    dt_col = _softplus(small_ref[:, _SM_DT:_SM_DT + SSD_HEADS] + dtb_ref[...])
    dt_row = _softplus(dtt_ref[...] + dtbt_ref[...])
    adt_col = dt_col * (-jnp.exp(a_ref[...]))
    adt_row = dt_row * (-jnp.exp(at_ref[...]))
    ri = lax.broadcasted_iota(jnp.int32, (q, q), 0)
    ci = lax.broadcasted_iota(jnp.int32, (q, q), 1)
    causal = ri >= ci
    tril = jnp.where(causal, 1.0, 0.0).astype(MXU_DTYPE)
    triu = jnp.where(ri <= ci, 1.0, 0.0).astype(MXU_DTYPE)
    cs_col = sum(_dot(tril, p) for p in _split3(adt_col))
    cs_row = sum(_dot(p, triu) for p in _split3(adt_row))

    cmb = cm.astype(MXU_DTYPE)
    bmb = bm.astype(MXU_DTYPE)
    for g in range(SSD_GROUPS):
        gs = slice(g * SSD_STATE, (g + 1) * SSD_STATE)
        cb = lax.dot_general(cmb[:, gs], bmb[:, gs], (((1,), (1,)), ((), ())),
                             preferred_element_type=f32)
        kpg = SSD_HEADS // SSD_GROUPS
        for hd in range(g * kpg, (g + 1) * kpg):
            hs = slice(hd * SSD_HEAD_DIM, (hd + 1) * SSD_HEAD_DIM)
            col = cs_col[:, hd:hd + 1]
            row = cs_row[hd:hd + 1, :]
            last = cs_row[hd:hd + 1, q - 1:q]
            decay = jnp.exp(jnp.where(causal, col - row, -jnp.inf))
            xdt = xs[:, hs] * dt_col[:, hd:hd + 1]
            y = _dot((cb * decay).astype(MXU_DTYPE), xdt.astype(MXU_DTYPE))
            y_off = _dot(cmb[:, gs], st_scr[:, hs].astype(MXU_DTYPE))
            y_scr[:, hs] = y + y_off * jnp.exp(col)
            xdec_scr[:, hs] = (xdt * jnp.exp(last - col)).astype(xdec_scr.dtype)
            decay_scr[:, hs] = jnp.broadcast_to(jnp.exp(last), (1, SSD_HEAD_DIM))
        ws = slice(g * kpg * SSD_HEAD_DIM, (g + 1) * kpg * SSD_HEAD_DIM)
        bmt = bm[:, gs].T.astype(MXU_DTYPE)
        st_scr[:, ws] = st_scr[:, ws] * decay_scr[:, ws] + _dot(bmt, xdec_scr[:, ws])

    z = z_ref[...]
    y = (y_scr[...] + drep_ref[...] * xs) * (z * jax.nn.sigmoid(z))
    gw = SSD_INNER // SSD_GROUPS
    for g in range(SSD_GROUPS):
        sl = slice(g * gw, (g + 1) * gw)
        seg = y[:, sl]
        seg = seg * lax.rsqrt(jnp.mean(seg * seg, axis=-1, keepdims=True) + EPS)
        o_ref[:, sl] = (seg * gn_ref[:, sl]).astype(o_ref.dtype)


def _ssd(xbc, small, dtt, z, conv_w, conv_b, dt_bias, a_log, d_skip, gate_norm):
    seq = xbc.shape[0]
    q = SSD_CHUNK
    f32 = jnp.float32

    def rows(width):
        return pl.BlockSpec((q, width), lambda i: (i, 0))

    drep = jnp.repeat(d_skip, SSD_HEAD_DIM)[None, :]
    return pl.pallas_call(
        _ssd_body,
        out_shape=jax.ShapeDtypeStruct((seq, SSD_INNER), MXU_DTYPE),
        grid=(seq // q,),
        in_specs=[rows(SSD_CONV_DIM), rows(LANES), pl.BlockSpec((SSD_HEADS, q), lambda i: (0, i)),
                  rows(SSD_INNER), _const_spec((SSD_CONV, SSD_CONV_DIM)), _const_spec((1, SSD_CONV_DIM)),
                  _const_spec((1, SSD_HEADS)), _const_spec((SSD_HEADS, 1)),
                  _const_spec((1, SSD_HEADS)), _const_spec((SSD_HEADS, 1)),
                  _const_spec((1, SSD_INNER)), _const_spec((1, SSD_INNER))],
        out_specs=rows(SSD_INNER),
        scratch_shapes=[pltpu.VMEM((8, SSD_CONV_DIM), f32),
                        pltpu.VMEM((SSD_STATE, SSD_INNER), f32),
                        pltpu.VMEM((q, SSD_INNER), f32),
                        pltpu.VMEM((q, SSD_INNER), MXU_DTYPE),
                        pltpu.VMEM((1, SSD_INNER), f32)],
        compiler_params=_params(("arbitrary",)),
        name="ssd",
    )(xbc, small, dtt, z, conv_w, conv_b[None, :], dt_bias[None, :], dt_bias[:, None],
      a_log[None, :], a_log[:, None], drep, gate_norm[None, :])


def _dsa_body(qi_ref, kj_ref, qit_ref, lohi_ref, ki_ref, qt_ref, k_ref, vt_ref, o_ref,
              keys_scr, lead_scr, y_scr, thr_scr, m_scr, l_scr, acc_scr, logit_scr, top_scr, *, top_k, idx_bits):
    tq, tk, ck = DSA_Q_TILE, DSA_K_TILE, DSA_CHUNK
    i32 = jnp.int32
    p = pl.program_id(0)
    qi = qi_ref[p]
    kj = kj_ref[p]

    @pl.when(kj == 0)
    def _select():
        chunks_per_tile = tk // ck
        n_tiles = ((qi + 1) * tq + tk - 1) // tk
        n_chunks = n_tiles * chunks_per_tile
        lo = lohi_ref[0:8, :]
        hi = lohi_ref[8:16, :]
        rows = 4 * 8
        t_pos = qi * tq + lax.broadcasted_iota(i32, (rows, tq), 1)
        s_off = lax.broadcasted_iota(i32, (ck, tq), 0)
        s_rows = lax.broadcasted_iota(i32, (rows, tq), 0)

        def project(c, slot):
            kic = ki_ref[pl.ds(pl.multiple_of(c * ck, ck), ck), :]
            for hd in range(IDX_HEADS):
                y_scr[slot, hd] = _dot(kic, qit_ref[hd * IDX_HEAD_DIM:(hd + 1) * IDX_HEAD_DIM, :])

        def emit_keys(c, slot, diagonal):
            for g in range(ck // rows):
                r0 = pl.multiple_of(c * ck + g * rows, rows)
                gs = slice(g * rows, (g + 1) * rows)
                sc = None
                for hd in range(IDX_HEADS):
                    term = jnp.minimum(jnp.maximum(y_scr[slot, hd, gs, :], lo[hd:hd + 1]), hi[hd:hd + 1])
                    sc = term if sc is None else sc + term
                bits = pltpu.bitcast(sc, i32)
                bits = jnp.where((bits & 0x7FFFFFFF) < MIN_NORMAL_EXP, 0, bits)
                key = bits ^ ((bits >> 31) & 0x7FFFFFFF)
                lead = bits & LEAD_MASK
                if diagonal:
                    causal = r0 + s_rows <= t_pos
                    key = jnp.where(causal, key, INT_MIN)
                    lead = jnp.where(causal, lead, QUIET_NAN_BITS)
                keys_scr[pl.ds(r0, rows), :] = key
                lead_scr[pl.ds(r0, rows), :] = pltpu.bitcast(lead, jnp.float32).astype(jnp.bfloat16)

        assert chunks_per_tile % 2 == 0

        def score_tile(t, diagonal):
            for j in range(chunks_per_tile):
                c = t * chunks_per_tile + j
                emit_keys(c, j % 2, diagonal)
                if not (diagonal and j == chunks_per_tile - 1):
                    project(c + 1, (j + 1) % 2)

        project(0, 0)
        lax.fori_loop(0, n_tiles - 1, lambda t, carry: (score_tile(t, False), carry)[1], 0)
        score_tile(n_tiles - 1, True)

        partial_rows = 4 * 8
        s_tile = lax.broadcasted_iota(i32, (partial_rows, tq), 0)

        def key_groups(c):
            for g in range(tk // partial_rows):
                r0 = pl.multiple_of(c * tk + g * partial_rows, partial_rows)
                yield keys_scr[pl.ds(r0, partial_rows), :], r0

        def count(pred):
            def body(c, acc):
                for kk, r0 in key_groups(c):
                    acc = acc + jnp.where(pred(kk, r0), 1.0, 0.0)
                return acc
            acc = lax.fori_loop(0, n_tiles, body, jnp.zeros((partial_rows, tq), jnp.float32))
            return acc.sum(axis=0, keepdims=True)

        def count_lead(cand_code):
            cand_code = jnp.where((cand_code > 0) & (cand_code < MIN_NORMAL_CODE), MIN_NORMAL_CODE,
                                  jnp.where((cand_code < 0) & (cand_code >= -MIN_NORMAL_CODE), 0, cand_code))
            cand_bits = (cand_code ^ ((cand_code >> 31) & 0x7FFF)) << 16
            cand = jnp.broadcast_to(pltpu.bitcast(cand_bits, jnp.float32), (16, tq)).astype(jnp.bfloat16)
            one, zero = jnp.ones((), jnp.bfloat16), jnp.zeros((), jnp.bfloat16)

            def body(c, acc):
                parts = []
                for g in range(tk // 128):
                    r0 = pl.multiple_of(c * tk + g * 128, 128)
                    lead = lead_scr[pl.ds(r0, 128), :].reshape(8, 16, tq)
                    parts += [jnp.where(lead[j] >= cand, one, zero) for j in range(8)]
                while len(parts) > 1:
                    parts = [a + b for a, b in zip(parts[0::2], parts[1::2])]
                return acc + parts[0].astype(jnp.float32)
            acc = lax.fori_loop(0, n_tiles, body, jnp.zeros((16, tq), jnp.float32))
            return acc.sum(axis=0, keepdims=True)

        k_f = jnp.float32(top_k)
        n_t = (qi * tq + 1 + lax.broadcasted_iota(i32, (1, tq), 1)).astype(jnp.float32)

        def lead_step(b, carry):
            code, c_lo, c_hi = carry
            cand = code + lax.shift_left(i32(1), i32(15) - b)
            cnt = count_lead(cand)
            ok = cnt >= k_f
            return jnp.where(ok, cand, code), jnp.where(ok, cnt, c_lo), jnp.where(ok, c_hi, cnt)

        code, c_lo, c_hi = lax.fori_loop(
            0, 16, lead_step, (jnp.full((1, tq), -(2 ** 15), i32), n_t, jnp.zeros((1, tq), jnp.float32)))

        lo_k = jnp.maximum(code << 16, INT_MIN + 1)
        hi_k = jnp.where(code == 2 ** 15 - 1, jnp.iinfo(i32).max, (code + 1) << 16)

        def finished(lo_k, hi_k, c_lo):
            return (c_lo <= k_f) | (lo_k + 1 >= hi_k)

        def key_value(kk):
            kk = jnp.clip(kk, -FINITE_KEY, FINITE_KEY)
            return pltpu.bitcast(kk ^ ((kk >> 31) & 0x7FFFFFFF), jnp.float32)

        def tighten(lo_k, hi_k, c_lo, c_hi, done, it):
            def body(c, carry):
                mn, mx = carry
                for kk, _ in key_groups(c):
                    mn = jnp.minimum(mn, jnp.where(kk >= lo_k, kk, jnp.iinfo(i32).max))
                    mx = jnp.maximum(mx, jnp.where(kk < hi_k, kk, INT_MIN))
                return mn, mx
            mn, mx = lax.fori_loop(0, n_tiles, body, (jnp.full((partial_rows, tq), jnp.iinfo(i32).max, i32),
                                                      jnp.full((partial_rows, tq), INT_MIN, i32)))
            mn, mx = mn.min(axis=0, keepdims=True), mx.max(axis=0, keepdims=True)
            return jnp.where(done, lo_k, mn), jnp.where(done, hi_k, mx + 1), c_lo, c_hi

        def probe(lo_k, hi_k, c_lo, c_hi, done, it):
            below, span = c_lo - k_f, c_lo - c_hi
            end = jnp.where(below + below >= span, hi_k - 1, lo_k + 1)
            lo_v, hi_v = key_value(lo_k), key_value(hi_k)
            guess_v = lo_v + (below + 0.5) / jnp.maximum(span, 1.0) * (hi_v - lo_v)
            guess_b = pltpu.bitcast(guess_v, i32)
            guess = guess_b ^ ((guess_b >> 31) & 0x7FFFFFFF)
            mid = (lo_k >> 1) + (hi_k >> 1) + (lo_k & hi_k & 1)
            phase = it % SEARCH_ROUND
            cand = jnp.where(phase == 1, end, jnp.where(phase == 3, mid, guess))
            cand = jnp.where(done, lo_k, jnp.maximum(lo_k + 1, jnp.minimum(hi_k - 1, cand)))
            cnt = count(lambda kk, r0: kk >= cand)
            up = jnp.logical_and(~done, cnt >= k_f)
            down = jnp.logical_and(~done, cnt < k_f)
            return (jnp.where(up, cand, lo_k), jnp.where(down, cand, hi_k),
                    jnp.where(up, cnt, c_lo), jnp.where(down, cnt, c_hi))

        def search_cond(carry):
            return jnp.logical_and(carry[0] < SEARCH_ROUND * 32, carry[1] > 0)

        def search_step(carry):
            it, _, lo_k, hi_k, c_lo, c_hi = carry
            done = finished(lo_k, hi_k, c_lo)
            lo_k, hi_k, c_lo, c_hi = lax.cond(it % SEARCH_ROUND == 0, tighten, probe, lo_k, hi_k, c_lo, c_hi, done, it)
            still = jnp.where(finished(lo_k, hi_k, c_lo), 0.0, 1.0)
            return it + 1, jnp.max(still).astype(i32), lo_k, hi_k, c_lo, c_hi

        active = jnp.max(jnp.where(finished(lo_k, hi_k, c_lo), 0.0, 1.0)).astype(i32)
        _, _, thr, _, n_ge, n_gt = lax.while_loop(search_cond, search_step, (i32(0), active, lo_k, hi_k, c_lo, c_hi))
        thr_scr[0:1, :] = thr

        tied = n_ge > k_f
        need = jnp.where(tied, k_f - n_gt, 0.0)
        max_need = jnp.max(need)

        @pl.when(max_need > 0.0)
        def _ties():
            int_max = jnp.iinfo(i32).max

            def by_extraction(_):
                def next_tie(j, cut):
                    def body(c, mn):
                        for kk, r0 in key_groups(c):
                            idx = r0 + s_tile
                            mn = jnp.minimum(mn, jnp.where((kk == thr) & (idx > cut), idx, int_max))
                        return mn
                    mn = lax.fori_loop(0, n_tiles, body, jnp.full((partial_rows, tq), int_max, i32))
                    return jnp.where(j.astype(jnp.float32) < need, mn.min(axis=0, keepdims=True), cut)
                return lax.fori_loop(0, max_need.astype(i32), next_tie, jnp.full((1, tq), -1, i32))

            def by_index_bits(_):
                def idx_step(b, cut):
                    cand = cut + lax.shift_left(i32(1), i32(idx_bits - 1) - b)
                    cnt = count(lambda kk, r0: jnp.where(kk == thr, r0 + s_tile, cand) < cand)
                    return jnp.where(cnt < need, cand, cut)
                return lax.fori_loop(0, idx_bits, idx_step, jnp.zeros((1, tq), i32))

            cut = lax.cond(max_need <= TIE_EXTRACT_MAX, by_extraction, by_index_bits, 0)
            cut = jnp.where(tied, cut, int_max)

            def drop(c, carry):
                r0 = pl.multiple_of(c * ck, ck)
                kk = keys_scr[pl.ds(r0, ck), :]
                excess = jnp.where(kk == thr, r0 + s_off, INT_MIN) > cut
                keys_scr[pl.ds(r0, ck), :] = jnp.where(excess, INT_MIN, kk)
                return carry

            lax.fori_loop(0, n_chunks, drop, 0)

        m_scr[...] = jnp.full_like(m_scr, NEG_BIG)
        l_scr[...] = jnp.zeros_like(l_scr)
        acc_scr[...] = jnp.zeros_like(acc_scr)
        top_scr[...] = jnp.full_like(top_scr, -jnp.inf)
        logit_scr[1] = jnp.full(logit_scr.shape[1:], -jnp.inf, jnp.float32)

    n_att = ((qi + 1) * tq - 1) // tk + 1
    row0 = pl.multiple_of(jnp.minimum(kj, n_att - 1) * tk, tk)
    thr_now = jnp.where(kj < n_att, thr_scr[0:1, :], jnp.iinfo(i32).max)
    bias = jnp.where(keys_scr[pl.ds(row0, tk), :] >= thr_now, 0.0, -jnp.inf)

    def attend(cur, prev):
        m_old = m_scr[...]
        m_new = jnp.maximum(m_old, top_scr[...])
        alpha = jnp.exp2(m_old - m_new)
        m_scr[...] = m_new
        col_sum, col_max = [], []
        for hd in range(ATT_HEADS):
            prob = jnp.exp2(logit_scr[prev, hd] - m_new[hd:hd + 1])
            col_sum.append(jnp.sum(prob, axis=0, keepdims=True))
            acc_scr[hd] = alpha[hd:hd + 1] * acc_scr[hd] + _dot(vt_ref[hd], prob.astype(MXU_DTYPE))
            logit = _dot(k_ref[hd], qt_ref[hd]) + bias
            logit_scr[cur, hd] = logit
            col_max.append(jnp.max(logit, axis=0, keepdims=True))
        l_scr[...] = alpha * l_scr[...] + jnp.concatenate(col_sum, axis=0)
        top_scr[...] = jnp.concatenate(col_max, axis=0)

    for parity in range(2):
        pl.when(kj % 2 == parity)(functools.partial(attend, parity, 1 - parity))

    @pl.when(kj == n_att)
    def _finish():
        for hd in range(ATT_HEADS):
            hs = slice(hd * ATT_HEAD_DIM, (hd + 1) * ATT_HEAD_DIM)
            o_ref[:, hs] = (acc_scr[hd] / l_scr[hd:hd + 1, :]).T.astype(o_ref.dtype)


def _dsa(qit, lohi, ki, qt, k, vt):
    seq = k.shape[1]
    tq, tk = DSA_Q_TILE, DSA_K_TILE
    assert seq % tk == 0 and tk % tq == 0 and tq % DSA_CHUNK == 0
    assert k.shape == (ATT_HEADS, seq, ATT_HEAD_DIM)
    assert qt.shape == (seq // tq, ATT_HEADS, ATT_HEAD_DIM, tq) and vt.shape == (seq // tk, ATT_HEADS, ATT_HEAD_DIM, tk)
    nq = seq // tq
    def n_att(i):
        return ((i + 1) * tq - 1) // tk + 1

    pairs = [(i, j) for i in range(nq) for j in range(n_att(i) + 1)]
    qi_idx = jnp.asarray([pq for pq, _ in pairs], jnp.int32)
    kj_idx = jnp.asarray([pk for _, pk in pairs], jnp.int32)
    top_k = min(TOPK_MAX, seq // 4)
    idx_bits = max(1, (seq - 1).bit_length())
    grid_spec = pltpu.PrefetchScalarGridSpec(
        num_scalar_prefetch=2,
        grid=(len(pairs),),
        in_specs=[
            pl.BlockSpec((None, IDX_INNER, tq), lambda p, qi, kj: (qi[p], 0, 0)),
            pl.BlockSpec((None, 16, tq), lambda p, qi, kj: (qi[p], 0, 0)),
            pl.BlockSpec((seq, IDX_HEAD_DIM), lambda p, qi, kj: (0, 0), pipeline_mode=pl.Buffered(1)),
            pl.BlockSpec((None, ATT_HEADS, ATT_HEAD_DIM, tq), lambda p, qi, kj: (qi[p], 0, 0, 0)),
            pl.BlockSpec((ATT_HEADS, tk, ATT_HEAD_DIM),
                         lambda p, qi, kj: (0, jnp.minimum(kj[p], n_att(qi[p]) - 1), 0)),
            pl.BlockSpec((None, ATT_HEADS, ATT_HEAD_DIM, tk),
                         lambda p, qi, kj: (jnp.maximum(kj[p] - 1, 0), 0, 0, 0)),
        ],
        out_specs=pl.BlockSpec((tq, ATT_INNER), lambda p, qi, kj: (qi[p], 0)),
        scratch_shapes=[pltpu.VMEM((seq, tq), jnp.int32),
                        pltpu.VMEM((seq, tq), jnp.bfloat16),
                        pltpu.VMEM((2, IDX_HEADS, DSA_CHUNK, tq), jnp.float32),
                        pltpu.VMEM((8, tq), jnp.int32),
                        pltpu.VMEM((ATT_HEADS, tq), jnp.float32),
                        pltpu.VMEM((ATT_HEADS, tq), jnp.float32),
                        pltpu.VMEM((ATT_HEADS, ATT_HEAD_DIM, tq), jnp.float32),
                        pltpu.VMEM((2, ATT_HEADS, tk, tq), jnp.float32),
                        pltpu.VMEM((ATT_HEADS, tq), jnp.float32)],
    )
    return pl.pallas_call(
        functools.partial(_dsa_body, top_k=top_k, idx_bits=idx_bits),
        out_shape=jax.ShapeDtypeStruct((seq, ATT_INNER), MXU_DTYPE),
        grid_spec=grid_spec,
        compiler_params=_params(("arbitrary",)),
        name="dsa",
    )(qi_idx, kj_idx, qit, lohi, ki, qt, k, vt)


def _even_out_body(h_ref, y_ref, o_ref, wy_ref, wo_ref, out_ref):
    out_ref[...] = h_ref[...] + _dot(y_ref[...], wy_ref[...]) + _dot(o_ref[...], wo_ref[...])


def _even_out(h, y, o, wy, wo):
    seq = h.shape[0]
    tm = min(ROW_TILE, seq)
    row = pl.BlockSpec((tm, D_MODEL), lambda i: (i, 0))
    return pl.pallas_call(
        _even_out_body,
        out_shape=jax.ShapeDtypeStruct((seq, D_MODEL), jnp.float32),
        grid=(seq // tm,),
        in_specs=[row, row, row, _const_spec((SSD_INNER, D_MODEL)), _const_spec((ATT_INNER, D_MODEL))],
        out_specs=row,
        compiler_params=_params(("parallel",)),
        name="even_out",
    )(h, y, o, wy, wo)


def _odd_body(h_ref, g_ref, win_ref, lng_ref, lnb_ref, ws_ref, bs_ref, wout_ref, o_ref, gated_scr):
    h = h_ref[...]
    tm = h.shape[0]
    xn = _rms(h, g_ref[...]).astype(MXU_DTYPE)
    u = jax.nn.gelu(_dot(xn, win_ref[:, :SG_INNER]))
    v = jax.nn.gelu(_dot(xn, win_ref[:, SG_INNER:]))
    mu = jnp.mean(v, axis=-1, keepdims=True)
    vc = v - mu
    var = jnp.mean(vc * vc, axis=-1, keepdims=True)
    v = (vc * lax.rsqrt(var + EPS) * lng_ref[...] + lnb_ref[...]).astype(MXU_DTYPE)
    ri = lax.broadcasted_iota(jnp.int32, (SG_CHUNK, SG_CHUNK), 0)
    ci = lax.broadcasted_iota(jnp.int32, (SG_CHUNK, SG_CHUNK), 1)
    gw = SG_INNER // SG_GROUPS
    for g in range(SG_GROUPS):
        w = jnp.where(ri >= ci, ws_ref[g], 0.0).astype(MXU_DTYPE)
        gs = slice(g * gw, (g + 1) * gw)
        for c in range(tm // SG_CHUNK):
            rs = slice(c * SG_CHUNK, (c + 1) * SG_CHUNK)
            mixed = _dot(w, v[rs, gs]) + bs_ref[:, gs]
            gated_scr[rs, gs] = (u[rs, gs] * mixed).astype(gated_scr.dtype)
    o_ref[...] = h + _dot(gated_scr[...], wout_ref[...])


def _odd(h, g, win, ln_g, ln_b, w_s, bs_full, wout):
    seq = h.shape[0]
    tm = min(ROW_TILE, seq)
    row = pl.BlockSpec((tm, D_MODEL), lambda i: (i, 0))
    return pl.pallas_call(
        _odd_body,
        out_shape=jax.ShapeDtypeStruct((seq, D_MODEL), jnp.float32),
        grid=(seq // tm,),
        in_specs=[row, _const_spec((1, D_MODEL)), _const_spec((D_MODEL, 2 * SG_INNER)),
                  _const_spec((1, SG_INNER)), _const_spec((1, SG_INNER)),
                  _const_spec((SG_GROUPS, SG_CHUNK, SG_CHUNK)), _const_spec((SG_CHUNK, SG_INNER)),
                  _const_spec((SG_INNER, D_MODEL))],
        out_specs=row,
        scratch_shapes=[pltpu.VMEM((tm, SG_INNER), MXU_DTYPE)],
        compiler_params=_params(("parallel",)),
        name="odd",
    )(h, g, win, ln_g, ln_b, w_s, bs_full, wout)


def _rope_inputs(seq):
    def tables(rot_dim):
        inv = 1.0 / (ROPE_THETA ** (jnp.arange(0, rot_dim, 2, dtype=jnp.float32) / rot_dim))
        ang = jnp.arange(seq, dtype=jnp.float32)[:, None] * inv[None, :]
        return jnp.cos(ang), jnp.sin(ang)

    def lane_tables(cos, sin):
        half = cos.shape[1]
        pad = LANES - 2 * half
        c = jnp.concatenate([cos, cos, jnp.ones((seq, pad), jnp.float32)], axis=1)
        s_lo = jnp.concatenate([-sin, jnp.zeros((seq, LANES - half), jnp.float32)], axis=1)
        s_hi = jnp.concatenate([jnp.zeros((seq, half), jnp.float32), sin,
                                jnp.zeros((seq, pad), jnp.float32)], axis=1)
        return c, s_lo, s_hi

    cos_a, sin_a = tables(ATT_HEAD_DIM // ROPE_FRACTION)
    cos_i, sin_i = tables(IDX_HEAD_DIM // ROPE_FRACTION)
    return (*lane_tables(cos_a, sin_a), *lane_tables(cos_i, sin_i), cos_a.T, sin_a.T, cos_i.T, sin_i.T)


def _even_weights(w_in):
    offs = np.cumsum((SSD_INNER, SSD_CONV_DIM, SSD_HEADS, ATT_INNER, ATT_INNER, ATT_INNER,
                      IDX_INNER, IDX_HEAD_DIM, IDX_HEADS))[:-1].tolist()
    z, xbc, dt, q, k, v, qi, ki, wi = jnp.split(w_in, offs, axis=-1)
    pad = jnp.zeros((D_MODEL, LANES - IDX_HEAD_DIM - SSD_HEADS), w_in.dtype)
    wn = jnp.concatenate([z, xbc, k, ki, dt, pad], axis=1).astype(MXU_DTYPE)
    wi_pad = jnp.zeros((D_MODEL, _T_DT - _T_WI - IDX_HEADS), w_in.dtype)
    wt = jnp.concatenate([q, v, qi, wi, wi_pad, dt], axis=1).T.astype(MXU_DTYPE)
    assert wn.shape == (D_MODEL, _N_END) and wt.shape == (_T_END, D_MODEL)
    return wn, wt


def _even_mixer(h, g, w_in, conv_w, conv_b, dt_bias, a_log, d_skip, gate_norm, w_out, rope):
    wn, wt = _even_weights(w_in)
    z, xbc, small, k, ki, qt, vt, qit, lohi, dtt = _even_in(h, g, wn, wt, rope)
    y = _ssd(xbc, small, dtt, z, conv_w, conv_b, dt_bias, a_log, d_skip, gate_norm)
    o = _dsa(qit, lohi, ki, qt, k, vt)
    w_out = w_out.astype(MXU_DTYPE)
    return _even_out(h, y, o, w_out[:SSD_INNER], w_out[SSD_INNER:])


def _forward(x, norm_g, final_g, ffn_w_gu, ffn_w_down, ev_w_in, ev_conv_w, ev_conv_b, ev_dt_bias,
             ev_a_log, ev_d, ev_gate_norm, ev_w_out, od_w_in, od_ln_g, od_ln_b, od_w_s, od_b_s, od_w_out):
    bsz, seq, _ = x.shape
    depth = norm_g.shape[0]
    rope = _rope_inputs(seq)
    wg = ffn_w_gu[..., :FFN_HIDDEN].astype(MXU_DTYPE)
    wu = ffn_w_gu[..., FFN_HIDDEN:].astype(MXU_DTYPE)
    wd = ffn_w_down.astype(MXU_DTYPE)
    fg = final_g[None, :]
    outs = []
    for b in range(bsz):
        h = x[b]
        for layer in range(depth):
            j = layer // 2
            h = _ffn(h, norm_g[layer, 0][None, :], wg[layer, 0], wu[layer, 0], wd[layer, 0], fg, False)
            g1 = norm_g[layer, 1][None, :]
            if layer % 2 == 0:
                h = _even_mixer(h, g1, ev_w_in[j], ev_conv_w[j], ev_conv_b[j], ev_dt_bias[j], ev_a_log[j],
                                ev_d[j], ev_gate_norm[j], ev_w_out[j], rope)
            else:
                bs_full = jnp.repeat(od_b_s[j].T, SG_INNER // SG_GROUPS, axis=1)
                h = _odd(h, g1, od_w_in[j].astype(MXU_DTYPE), od_ln_g[j][None, :], od_ln_b[j][None, :],
                         od_w_s[j], bs_full, od_w_out[j].astype(MXU_DTYPE))
            h = _ffn(h, norm_g[layer, 2][None, :], wg[layer, 1], wu[layer, 1], wd[layer, 1], fg,
                     layer == depth - 1)
        outs.append(h)
    return outs[0][None] if bsz == 1 else jnp.stack(outs, axis=0)


def kernel(x, norm_g, final_g, ffn_w_gu, ffn_w_down, ev_w_in, ev_conv_w, ev_conv_b, ev_dt_bias, ev_a_log,
           ev_d, ev_gate_norm, ev_w_out, od_w_in, od_ln_g, od_ln_b, od_w_s, od_b_s, od_w_out):
    return _forward(x, norm_g, final_g, ffn_w_gu, ffn_w_down, ev_w_in, ev_conv_w, ev_conv_b, ev_dt_bias,
                    ev_a_log, ev_d, ev_gate_norm, ev_w_out, od_w_in, od_ln_g, od_ln_b, od_w_s, od_b_s,
                    od_w_out)
```

```python
import functools
import math

import numpy as np
import jax
import jax.numpy as jnp
from jax import lax
from jax.experimental import pallas as pl
from jax.experimental.pallas import tpu as pltpu

D_MODEL = 1024
SSD_HEADS = 16
SSD_HEAD_DIM = 64
SSD_INNER = SSD_HEADS * SSD_HEAD_DIM
SSD_GROUPS = 2
SSD_STATE = 128
SSD_CONV = 4
SSD_CHUNK = 256
SSD_CONV_DIM = SSD_INNER + 2 * SSD_GROUPS * SSD_STATE
ATT_HEADS = 8
ATT_HEAD_DIM = 128
ATT_INNER = ATT_HEADS * ATT_HEAD_DIM
IDX_HEADS = 8
IDX_HEAD_DIM = 64
IDX_INNER = IDX_HEADS * IDX_HEAD_DIM
TOPK_MAX = 256
SG_CHUNK = 128
SG_GROUPS = 8
SG_INNER = 2 * D_MODEL
FFN_HIDDEN = 2816
ROPE_THETA = 500000.0
ROPE_FRACTION = 4
EPS = 1e-6

ATT_ROT_HALF = ATT_HEAD_DIM // ROPE_FRACTION // 2
IDX_ROT_HALF = IDX_HEAD_DIM // ROPE_FRACTION // 2

MXU_DTYPE = jnp.bfloat16
LANES = 128
VMEM_LIMIT_BYTES = 56 * 1024 * 1024

ROW_TILE = 512
FFN_HIDDEN_CHUNK = FFN_HIDDEN // 11
DSA_Q_TILE = 256
DSA_K_TILE = 512
DSA_CHUNK = 256
TIE_EXTRACT_MAX = 8
SEARCH_ROUND = 5
LEAD_MASK = -(2 ** 16)
QUIET_NAN_BITS = 0x7FC00000
FINITE_KEY = 0x7F7FFFFF
MIN_NORMAL_EXP = 0x00800000
MIN_NORMAL_CODE = MIN_NORMAL_EXP >> 16
INT_MIN = -(2 ** 31)
NEG_BIG = -0.7 * float(np.finfo(np.float32).max)

_N_Z = 0
_N_XBC = _N_Z + SSD_INNER
_N_K = _N_XBC + SSD_CONV_DIM
_N_SMALL = _N_K + ATT_INNER
_N_END = _N_SMALL + LANES
_SM_KI = 0
_SM_DT = IDX_HEAD_DIM
_T_Q = 0
_T_V = _T_Q + ATT_INNER
_T_QI = _T_V + ATT_INNER
_T_WI = _T_QI + IDX_INNER
_T_DT = _T_WI + 16
_T_END = _T_DT + SSD_HEADS


def _dot(a, b):
    return jnp.dot(a, b, preferred_element_type=jnp.float32)


def _rms(x, g):
    return x * lax.rsqrt(jnp.mean(x * x, axis=-1, keepdims=True) + EPS) * g


def _softplus(x):
    return jnp.maximum(x, 0.0) + jnp.log1p(jnp.exp(-jnp.abs(x)))


def _split3(x):
    hi = x.astype(MXU_DTYPE)
    r = x - hi.astype(jnp.float32)
    mid = r.astype(MXU_DTYPE)
    lo = (r - mid.astype(jnp.float32)).astype(MXU_DTYPE)
    return hi, mid, lo


def _const_spec(shape):
    zeros = (0,) * len(shape)
    return pl.BlockSpec(shape, lambda *_: zeros, pipeline_mode=pl.Buffered(1))


def _params(sem):
    return pltpu.CompilerParams(dimension_semantics=sem, vmem_limit_bytes=VMEM_LIMIT_BYTES)


def _ffn_body(h_ref, g_ref, wg_ref, wu_ref, wd_ref, fg_ref, o_ref, *, final):
    h = h_ref[...]
    xn = _rms(h, g_ref[...]).astype(MXU_DTYPE)
    acc = None
    for c in range(FFN_HIDDEN // FFN_HIDDEN_CHUNK):
        sl = slice(c * FFN_HIDDEN_CHUNK, (c + 1) * FFN_HIDDEN_CHUNK)
        gate = _dot(xn, wg_ref[:, sl])
        up = _dot(xn, wu_ref[:, sl])
        act = (gate * jax.nn.sigmoid(gate) * up).astype(MXU_DTYPE)
        part = _dot(act, wd_ref[sl, :])
        acc = part if acc is None else acc + part
    out = h + 0.5 * acc
    if final:
        out = _rms(out, fg_ref[...])
    o_ref[...] = out


def _ffn(h, g, wg, wu, wd, final_g, final):
    seq = h.shape[0]
    tm = min(ROW_TILE, seq)
    row = pl.BlockSpec((tm, D_MODEL), lambda i: (i, 0))
    return pl.pallas_call(
        functools.partial(_ffn_body, final=final),
        out_shape=jax.ShapeDtypeStruct((seq, D_MODEL), jnp.float32),
        grid=(seq // tm,),
        in_specs=[row, _const_spec((1, D_MODEL)), _const_spec((D_MODEL, FFN_HIDDEN)),
                  _const_spec((D_MODEL, FFN_HIDDEN)), _const_spec((FFN_HIDDEN, D_MODEL)),
                  _const_spec((1, D_MODEL))],
        out_specs=row,
        compiler_params=_params(("parallel",)),
        name="ffn",
    )(h, g, wg, wu, wd, final_g)


def _rope_lanes(x, c, s_lo, s_hi, half):
    return x * c + pltpu.roll(x, LANES - half, 1) * s_lo + pltpu.roll(x, half, 1) * s_hi


def _rope_rows(x, c, s, half):
    x1, x2 = x[:half], x[half:2 * half]
    return jnp.concatenate([x1 * c - x2 * s, x1 * s + x2 * c, x[2 * half:]], axis=0)


def _even_in_body(h_ref, g_ref, wn_ref, wt_ref, ca_ref, sal_ref, sah_ref, ci_ref, sil_ref, sih_ref,
                  cat_ref, sat_ref, cit_ref, sit_ref,
                  z_ref, xbc_ref, small_ref, k_ref, ki_ref, qt_ref, vt_ref, qit_ref, lohi_ref, dtt_ref):
    xn = _rms(h_ref[...], g_ref[...])
    xnb = xn.astype(MXU_DTYPE)
    xnt = xn.T.astype(MXU_DTYPE)

    z_ref[...] = _dot(xnb, wn_ref[:, _N_Z:_N_XBC])
    xbc_ref[...] = _dot(xnb, wn_ref[:, _N_XBC:_N_K])
    kk = _dot(xnb, wn_ref[:, _N_K:_N_SMALL])
    ca, sal, sah = ca_ref[...], sal_ref[...], sah_ref[...]
    for hd in range(ATT_HEADS):
        sl = slice(hd * ATT_HEAD_DIM, (hd + 1) * ATT_HEAD_DIM)
        k_ref[hd] = _rope_lanes(kk[:, sl], ca, sal, sah, ATT_ROT_HALF).astype(k_ref.dtype)
    sm = _dot(xnb, wn_ref[:, _N_SMALL:_N_END])
    small_ref[...] = sm
    smr = _rope_lanes(sm, ci_ref[...], sil_ref[...], sih_ref[...], IDX_ROT_HALF)
    ki_ref[...] = smr[:, _SM_KI:_SM_KI + IDX_HEAD_DIM].astype(ki_ref.dtype)

    att_scale = ATT_HEAD_DIM ** -0.5 * math.log2(math.e)
    qt = _dot(wt_ref[_T_Q:_T_V, :], xnt)
    cat, sat = cat_ref[...], sat_ref[...]
    def put(ref, where, val):
        tb = ref.shape[-1]
        for b in range(ref.shape[0]):
            ref[(b, *where, slice(None))] = val[:, b * tb:(b + 1) * tb].astype(ref.dtype)

    vt = _dot(wt_ref[_T_V:_T_QI, :], xnt)
    for hd in range(ATT_HEADS):
        sl = slice(hd * ATT_HEAD_DIM, (hd + 1) * ATT_HEAD_DIM)
        put(qt_ref, (hd, slice(None)), _rope_rows(qt[sl], cat, sat, ATT_ROT_HALF) * att_scale)
        put(vt_ref, (hd, slice(None)), vt[sl])
    tail = _dot(wt_ref[_T_WI:_T_END, :], xnt)
    wit = tail[0:IDX_HEADS] * (IDX_HEADS ** -0.5 * IDX_HEAD_DIM ** -0.5)
    dtt_ref[...] = tail[_T_DT - _T_WI:_T_END - _T_WI]
    pos = wit > 0.0
    put(lohi_ref, (slice(0, 8),), jnp.where(pos, 0.0, -jnp.inf))
    put(lohi_ref, (slice(8, 16),), jnp.where(pos, jnp.inf, 0.0))
    qit = _dot(wt_ref[_T_QI:_T_WI, :], xnt)
    cit, sit = cit_ref[...], sit_ref[...]
    for hd in range(IDX_HEADS):
        sl = slice(hd * IDX_HEAD_DIM, (hd + 1) * IDX_HEAD_DIM)
        put(qit_ref, (sl,), _rope_rows(qit[sl], cit, sit, IDX_ROT_HALF) * wit[hd:hd + 1])


def _even_in(h, g, wn, wt, rope):
    seq = h.shape[0]
    tm = min(ROW_TILE, seq)
    f32 = jnp.float32

    def rows(width):
        return pl.BlockSpec((tm, width), lambda i: (i, 0))

    def cols(height):
        return pl.BlockSpec((height, tm), lambda i: (0, i))

    def blocks(height, tb):
        return pl.BlockSpec((tm // tb, height, tb), lambda i: (i, 0, 0))

    def head_blocks(tb):
        return pl.BlockSpec((tm // tb, ATT_HEADS, ATT_HEAD_DIM, tb), lambda i: (i, 0, 0, 0))

    assert tm % DSA_K_TILE == 0 and tm % DSA_Q_TILE == 0
    nq, nk = seq // DSA_Q_TILE, seq // DSA_K_TILE
    out_shape = (
        jax.ShapeDtypeStruct((seq, SSD_INNER), f32),
        jax.ShapeDtypeStruct((seq, SSD_CONV_DIM), f32),
        jax.ShapeDtypeStruct((seq, LANES), f32),
        jax.ShapeDtypeStruct((ATT_HEADS, seq, ATT_HEAD_DIM), MXU_DTYPE),
        jax.ShapeDtypeStruct((seq, IDX_HEAD_DIM), MXU_DTYPE),
        jax.ShapeDtypeStruct((nq, ATT_HEADS, ATT_HEAD_DIM, DSA_Q_TILE), MXU_DTYPE),
        jax.ShapeDtypeStruct((nk, ATT_HEADS, ATT_HEAD_DIM, DSA_K_TILE), MXU_DTYPE),
        jax.ShapeDtypeStruct((nq, IDX_INNER, DSA_Q_TILE), MXU_DTYPE),
        jax.ShapeDtypeStruct((nq, 16, DSA_Q_TILE), f32),
        jax.ShapeDtypeStruct((SSD_HEADS, seq), f32),
    )
    out_specs = (rows(SSD_INNER), rows(SSD_CONV_DIM), rows(LANES),
                 pl.BlockSpec((ATT_HEADS, tm, ATT_HEAD_DIM), lambda i: (0, i, 0)), rows(IDX_HEAD_DIM),
                 head_blocks(DSA_Q_TILE), head_blocks(DSA_K_TILE), blocks(IDX_INNER, DSA_Q_TILE),
                 blocks(16, DSA_Q_TILE), cols(SSD_HEADS))
    in_specs = [rows(D_MODEL), _const_spec((1, D_MODEL)), _const_spec(wn.shape), _const_spec(wt.shape)]
    in_specs += [rows(LANES)] * 6
    in_specs += [cols(ATT_ROT_HALF)] * 2 + [cols(IDX_ROT_HALF)] * 2
    return pl.pallas_call(
        _even_in_body,
        out_shape=out_shape,
        grid=(seq // tm,),
        in_specs=in_specs,
        out_specs=out_specs,
        compiler_params=_params(("parallel",)),
        name="even_in",
    )(h, g, wn, wt, *rope)


def _ssd_body(xbc_ref, small_ref, dtt_ref, z_ref, cw_ref, cb_ref, dtb_ref, dtbt_ref, a_ref, at_ref,
              drep_ref, gn_ref, o_ref, tail_scr, st_scr, y_scr, xdec_scr, decay_scr):
    q = SSD_CHUNK
    f32 = jnp.float32

    @pl.when(pl.program_id(0) == 0)
    def _():
        tail_scr[...] = jnp.zeros_like(tail_scr)
        st_scr[...] = jnp.zeros_like(st_scr)

    x = xbc_ref[...]
    tail = tail_scr[...]
    row8 = lax.broadcasted_iota(jnp.int32, (8, SSD_CONV_DIM), 0)
    conv = x * cw_ref[SSD_CONV - 1:SSD_CONV, :] + cb_ref[...]
    for shift in range(1, SSD_CONV):
        rolled = pltpu.roll(x, shift, 0)
        head = jnp.where(row8 < shift, pltpu.roll(tail, shift, 0), rolled[0:8])
        shifted = jnp.concatenate([head, rolled[8:]], axis=0)
        conv = conv + shifted * cw_ref[SSD_CONV - 1 - shift:SSD_CONV - shift, :]
    tail_scr[...] = x[q - 8:q]
    xbc = conv * jax.nn.sigmoid(conv)
    xs = xbc[:, :SSD_INNER]
    bm = xbc[:, SSD_INNER:SSD_INNER + SSD_GROUPS * SSD_STATE]
    cm = xbc[:, SSD_INNER + SSD_GROUPS * SSD_STATE:]

    dt_col = _softplus(small_ref[:, _SM_DT:_SM_DT + SSD_HEADS] + dtb_ref[...])
    dt_row = _softplus(dtt_ref[...] + dtbt_ref[...])
    adt_col = dt_col * (-jnp.exp(a_ref[...]))
    adt_row = dt_row * (-jnp.exp(at_ref[...]))
    ri = lax.broadcasted_iota(jnp.int32, (q, q), 0)
    ci = lax.broadcasted_iota(jnp.int32, (q, q), 1)
    causal = ri >= ci
    tril = jnp.where(causal, 1.0, 0.0).astype(MXU_DTYPE)
    triu = jnp.where(ri <= ci, 1.0, 0.0).astype(MXU_DTYPE)
    cs_col = sum(_dot(tril, p) for p in _split3(adt_col))
    cs_row = sum(_dot(p, triu) for p in _split3(adt_row))

    cmb = cm.astype(MXU_DTYPE)
    bmb = bm.astype(MXU_DTYPE)
    for g in range(SSD_GROUPS):
        gs = slice(g * SSD_STATE, (g + 1) * SSD_STATE)
        cb = lax.dot_general(cmb[:, gs], bmb[:, gs], (((1,), (1,)), ((), ())),
                             preferred_element_type=f32)
        kpg = SSD_HEADS // SSD_GROUPS
        for hd in range(g * kpg, (g + 1) * kpg):
            hs = slice(hd * SSD_HEAD_DIM, (hd + 1) * SSD_HEAD_DIM)
            col = cs_col[:, hd:hd + 1]
            row = cs_row[hd:hd + 1, :]
            last = cs_row[hd:hd + 1, q - 1:q]
            decay = jnp.exp(jnp.where(causal, col - row, -jnp.inf))
            xdt = xs[:, hs] * dt_col[:, hd:hd + 1]
            y = _dot((cb * decay).astype(MXU_DTYPE), xdt.astype(MXU_DTYPE))
            y_off = _dot(cmb[:, gs], st_scr[:, hs].astype(MXU_DTYPE))
            y_scr[:, hs] = y + y_off * jnp.exp(col)
            xdec_scr[:, hs] = (xdt * jnp.exp(last - col)).astype(xdec_scr.dtype)
            decay_scr[:, hs] = jnp.broadcast_to(jnp.exp(last), (1, SSD_HEAD_DIM))
        ws = slice(g * kpg * SSD_HEAD_DIM, (g + 1) * kpg * SSD_HEAD_DIM)
        bmt = bm[:, gs].T.astype(MXU_DTYPE)
        st_scr[:, ws] = st_scr[:, ws] * decay_scr[:, ws] + _dot(bmt, xdec_scr[:, ws])

    z = z_ref[...]
    y = (y_scr[...] + drep_ref[...] * xs) * (z * jax.nn.sigmoid(z))
    gw = SSD_INNER // SSD_GROUPS
    for g in range(SSD_GROUPS):
        sl = slice(g * gw, (g + 1) * gw)
        seg = y[:, sl]
        seg = seg * lax.rsqrt(jnp.mean(seg * seg, axis=-1, keepdims=True) + EPS)
        o_ref[:, sl] = (seg * gn_ref[:, sl]).astype(o_ref.dtype)


def _ssd(xbc, small, dtt, z, conv_w, conv_b, dt_bias, a_log, d_skip, gate_norm):
    seq = xbc.shape[0]
    q = SSD_CHUNK
    f32 = jnp.float32

    def rows(width):
        return pl.BlockSpec((q, width), lambda i: (i, 0))

    drep = jnp.repeat(d_skip, SSD_HEAD_DIM)[None, :]
    return pl.pallas_call(
        _ssd_body,
        out_shape=jax.ShapeDtypeStruct((seq, SSD_INNER), MXU_DTYPE),
        grid=(seq // q,),
        in_specs=[rows(SSD_CONV_DIM), rows(LANES), pl.BlockSpec((SSD_HEADS, q), lambda i: (0, i)),
                  rows(SSD_INNER), _const_spec((SSD_CONV, SSD_CONV_DIM)), _const_spec((1, SSD_CONV_DIM)),
                  _const_spec((1, SSD_HEADS)), _const_spec((SSD_HEADS, 1)),
                  _const_spec((1, SSD_HEADS)), _const_spec((SSD_HEADS, 1)),
                  _const_spec((1, SSD_INNER)), _const_spec((1, SSD_INNER))],
        out_specs=rows(SSD_INNER),
        scratch_shapes=[pltpu.VMEM((8, SSD_CONV_DIM), f32),
                        pltpu.VMEM((SSD_STATE, SSD_INNER), f32),
                        pltpu.VMEM((q, SSD_INNER), f32),
                        pltpu.VMEM((q, SSD_INNER), MXU_DTYPE),
                        pltpu.VMEM((1, SSD_INNER), f32)],
        compiler_params=_params(("arbitrary",)),
        name="ssd",
    )(xbc, small, dtt, z, conv_w, conv_b[None, :], dt_bias[None, :], dt_bias[:, None],
      a_log[None, :], a_log[:, None], drep, gate_norm[None, :])


def _dsa_body(qi_ref, kj_ref, qit_ref, lohi_ref, ki_ref, qt_ref, k_ref, vt_ref, o_ref,
              keys_scr, lead_scr, y_scr, thr_scr, m_scr, l_scr, acc_scr, logit_scr, top_scr, *, top_k, idx_bits):
    tq, tk, ck = DSA_Q_TILE, DSA_K_TILE, DSA_CHUNK
    i32 = jnp.int32
    p = pl.program_id(0)
    qi = qi_ref[p]
    kj = kj_ref[p]

    @pl.when(kj == 0)
    def _select():
        chunks_per_tile = tk // ck
        n_tiles = ((qi + 1) * tq + tk - 1) // tk
        n_chunks = n_tiles * chunks_per_tile
        lo = lohi_ref[0:8, :]
        hi = lohi_ref[8:16, :]
        rows = 4 * 8
        t_pos = qi * tq + lax.broadcasted_iota(i32, (rows, tq), 1)
        s_off = lax.broadcasted_iota(i32, (ck, tq), 0)
        s_rows = lax.broadcasted_iota(i32, (rows, tq), 0)

        def project(c, slot):
            kic = ki_ref[pl.ds(pl.multiple_of(c * ck, ck), ck), :]
            for hd in range(IDX_HEADS):
                y_scr[slot, hd] = _dot(kic, qit_ref[hd * IDX_HEAD_DIM:(hd + 1) * IDX_HEAD_DIM, :])

        def emit_keys(c, slot, diagonal):
            for g in range(ck // rows):
                r0 = pl.multiple_of(c * ck + g * rows, rows)
                gs = slice(g * rows, (g + 1) * rows)
                sc = None
                for hd in range(IDX_HEADS):
                    term = jnp.minimum(jnp.maximum(y_scr[slot, hd, gs, :], lo[hd:hd + 1]), hi[hd:hd + 1])
                    sc = term if sc is None else sc + term
                bits = pltpu.bitcast(sc, i32)
                bits = jnp.where((bits & 0x7FFFFFFF) < MIN_NORMAL_EXP, 0, bits)
                key = bits ^ ((bits >> 31) & 0x7FFFFFFF)
                lead = bits & LEAD_MASK
                if diagonal:
                    causal = r0 + s_rows <= t_pos
                    key = jnp.where(causal, key, INT_MIN)
                    lead = jnp.where(causal, lead, QUIET_NAN_BITS)
                keys_scr[pl.ds(r0, rows), :] = key
                lead_scr[pl.ds(r0, rows), :] = pltpu.bitcast(lead, jnp.float32).astype(jnp.bfloat16)

        assert chunks_per_tile % 2 == 0

        def score_tile(t, diagonal):
            for j in range(chunks_per_tile):
                c = t * chunks_per_tile + j
                emit_keys(c, j % 2, diagonal)
                if not (diagonal and j == chunks_per_tile - 1):
                    project(c + 1, (j + 1) % 2)

        project(0, 0)
        lax.fori_loop(0, n_tiles - 1, lambda t, carry: (score_tile(t, False), carry)[1], 0)
        score_tile(n_tiles - 1, True)

        partial_rows = 4 * 8
        s_tile = lax.broadcasted_iota(i32, (partial_rows, tq), 0)

        def key_groups(c):
            for g in range(tk // partial_rows):
                r0 = pl.multiple_of(c * tk + g * partial_rows, partial_rows)
                yield keys_scr[pl.ds(r0, partial_rows), :], r0

        def count(pred):
            def body(c, acc):
                for kk, r0 in key_groups(c):
                    acc = acc + jnp.where(pred(kk, r0), 1.0, 0.0)
                return acc
            acc = lax.fori_loop(0, n_tiles, body, jnp.zeros((partial_rows, tq), jnp.float32))
            return acc.sum(axis=0, keepdims=True)

        def count_lead(cand_code):
            cand_code = jnp.where((cand_code > 0) & (cand_code < MIN_NORMAL_CODE), MIN_NORMAL_CODE,
                                  jnp.where((cand_code < 0) & (cand_code >= -MIN_NORMAL_CODE), 0, cand_code))
            cand_bits = (cand_code ^ ((cand_code >> 31) & 0x7FFF)) << 16
            cand = jnp.broadcast_to(pltpu.bitcast(cand_bits, jnp.float32), (16, tq)).astype(jnp.bfloat16)
            one, zero = jnp.ones((), jnp.bfloat16), jnp.zeros((), jnp.bfloat16)

            def body(c, acc):
                parts = []
                for g in range(tk // 128):
                    r0 = pl.multiple_of(c * tk + g * 128, 128)
                    lead = lead_scr[pl.ds(r0, 128), :].reshape(8, 16, tq)
                    parts += [jnp.where(lead[j] >= cand, one, zero) for j in range(8)]
                while len(parts) > 1:
                    parts = [a + b for a, b in zip(parts[0::2], parts[1::2])]
                return acc + parts[0].astype(jnp.float32)
            acc = lax.fori_loop(0, n_tiles, body, jnp.zeros((16, tq), jnp.float32))
            return acc.sum(axis=0, keepdims=True)

        k_f = jnp.float32(top_k)
        n_t = (qi * tq + 1 + lax.broadcasted_iota(i32, (1, tq), 1)).astype(jnp.float32)

        def lead_step(b, carry):
            code, c_lo, c_hi = carry
            cand = code + lax.shift_left(i32(1), i32(15) - b)
            cnt = count_lead(cand)
            ok = cnt >= k_f
            return jnp.where(ok, cand, code), jnp.where(ok, cnt, c_lo), jnp.where(ok, c_hi, cnt)

        code, c_lo, c_hi = lax.fori_loop(
            0, 16, lead_step, (jnp.full((1, tq), -(2 ** 15), i32), n_t, jnp.zeros((1, tq), jnp.float32)))

        lo_k = jnp.maximum(code << 16, INT_MIN + 1)
        hi_k = jnp.where(code == 2 ** 15 - 1, jnp.iinfo(i32).max, (code + 1) << 16)

        def finished(lo_k, hi_k, c_lo):
            return (c_lo <= k_f) | (lo_k + 1 >= hi_k)

        def key_value(kk):
            kk = jnp.clip(kk, -FINITE_KEY, FINITE_KEY)
            return pltpu.bitcast(kk ^ ((kk >> 31) & 0x7FFFFFFF), jnp.float32)

        def tighten(lo_k, hi_k, c_lo, c_hi, done, it):
            def body(c, carry):
                mn, mx = carry
                for kk, _ in key_groups(c):
                    mn = jnp.minimum(mn, jnp.where(kk >= lo_k, kk, jnp.iinfo(i32).max))
                    mx = jnp.maximum(mx, jnp.where(kk < hi_k, kk, INT_MIN))
                return mn, mx
            mn, mx = lax.fori_loop(0, n_tiles, body, (jnp.full((partial_rows, tq), jnp.iinfo(i32).max, i32),
                                                      jnp.full((partial_rows, tq), INT_MIN, i32)))
            mn, mx = mn.min(axis=0, keepdims=True), mx.max(axis=0, keepdims=True)
            return jnp.where(done, lo_k, mn), jnp.where(done, hi_k, mx + 1), c_lo, c_hi

        def probe(lo_k, hi_k, c_lo, c_hi, done, it):
            below, span = c_lo - k_f, c_lo - c_hi
            end = jnp.where(below + below >= span, hi_k - 1, lo_k + 1)
            lo_v, hi_v = key_value(lo_k), key_value(hi_k)
            guess_v = lo_v + (below + 0.5) / jnp.maximum(span, 1.0) * (hi_v - lo_v)
            guess_b = pltpu.bitcast(guess_v, i32)
            guess = guess_b ^ ((guess_b >> 31) & 0x7FFFFFFF)
            mid = (lo_k >> 1) + (hi_k >> 1) + (lo_k & hi_k & 1)
            phase = it % SEARCH_ROUND
            cand = jnp.where(phase == 1, end, jnp.where(phase == 3, mid, guess))
            cand = jnp.where(done, lo_k, jnp.maximum(lo_k + 1, jnp.minimum(hi_k - 1, cand)))
            cnt = count(lambda kk, r0: kk >= cand)
            up = jnp.logical_and(~done, cnt >= k_f)
            down = jnp.logical_and(~done, cnt < k_f)
            return (jnp.where(up, cand, lo_k), jnp.where(down, cand, hi_k),
                    jnp.where(up, cnt, c_lo), jnp.where(down, cnt, c_hi))

        def search_cond(carry):
            return jnp.logical_and(carry[0] < SEARCH_ROUND * 32, carry[1] > 0)

        def search_step(carry):
            it, _, lo_k, hi_k, c_lo, c_hi = carry
            done = finished(lo_k, hi_k, c_lo)
            lo_k, hi_k, c_lo, c_hi = lax.cond(it % SEARCH_ROUND == 0, tighten, probe, lo_k, hi_k, c_lo, c_hi, done, it)
            still = jnp.where(finished(lo_k, hi_k, c_lo), 0.0, 1.0)
            return it + 1, jnp.max(still).astype(i32), lo_k, hi_k, c_lo, c_hi

        active = jnp.max(jnp.where(finished(lo_k, hi_k, c_lo), 0.0, 1.0)).astype(i32)
        _, _, thr, _, n_ge, n_gt = lax.while_loop(search_cond, search_step, (i32(0), active, lo_k, hi_k, c_lo, c_hi))
        thr_scr[0:1, :] = thr

        tied = n_ge > k_f
        need = jnp.where(tied, k_f - n_gt, 0.0)
        max_need = jnp.max(need)

        @pl.when(max_need > 0.0)
        def _ties():
            int_max = jnp.iinfo(i32).max

            def by_extraction(_):
                def next_tie(j, cut):
                    def body(c, mn):
                        for kk, r0 in key_groups(c):
                            idx = r0 + s_tile
                            mn = jnp.minimum(mn, jnp.where((kk == thr) & (idx > cut), idx, int_max))
                        return mn
                    mn = lax.fori_loop(0, n_tiles, body, jnp.full((partial_rows, tq), int_max, i32))
                    return jnp.where(j.astype(jnp.float32) < need, mn.min(axis=0, keepdims=True), cut)
                return lax.fori_loop(0, max_need.astype(i32), next_tie, jnp.full((1, tq), -1, i32))

            def by_index_bits(_):
                def idx_step(b, cut):
                    cand = cut + lax.shift_left(i32(1), i32(idx_bits - 1) - b)
                    cnt = count(lambda kk, r0: jnp.where(kk == thr, r0 + s_tile, cand) < cand)
                    return jnp.where(cnt < need, cand, cut)
                return lax.fori_loop(0, idx_bits, idx_step, jnp.zeros((1, tq), i32))

            cut = lax.cond(max_need <= TIE_EXTRACT_MAX, by_extraction, by_index_bits, 0)
            cut = jnp.where(tied, cut, int_max)

            def drop(c, carry):
                r0 = pl.multiple_of(c * ck, ck)
                kk = keys_scr[pl.ds(r0, ck), :]
                excess = jnp.where(kk == thr, r0 + s_off, INT_MIN) > cut
                keys_scr[pl.ds(r0, ck), :] = jnp.where(excess, INT_MIN, kk)
                return carry

            lax.fori_loop(0, n_chunks, drop, 0)

        m_scr[...] = jnp.full_like(m_scr, NEG_BIG)
        l_scr[...] = jnp.zeros_like(l_scr)
        acc_scr[...] = jnp.zeros_like(acc_scr)
        top_scr[...] = jnp.full_like(top_scr, -jnp.inf)
        logit_scr[1] = jnp.full(logit_scr.shape[1:], -jnp.inf, jnp.float32)

    n_att = ((qi + 1) * tq - 1) // tk + 1
    row0 = pl.multiple_of(jnp.minimum(kj, n_att - 1) * tk, tk)
    thr_now = jnp.where(kj < n_att, thr_scr[0:1, :], jnp.iinfo(i32).max)
    bias = jnp.where(keys_scr[pl.ds(row0, tk), :] >= thr_now, 0.0, -jnp.inf)

    def attend(cur, prev):
        m_old = m_scr[...]
        m_new = jnp.maximum(m_old, top_scr[...])
        alpha = jnp.exp2(m_old - m_new)
        m_scr[...] = m_new
        col_sum, col_max = [], []
        for hd in range(ATT_HEADS):
            logit = _dot(k_ref[hd], qt_ref[hd]) + bias
            logit_scr[cur, hd] = logit
            col_max.append(jnp.max(logit, axis=0, keepdims=True))
            prob = jnp.exp2(logit_scr[prev, hd] - m_new[hd:hd + 1])
            col_sum.append(jnp.sum(prob, axis=0, keepdims=True))
            acc_scr[hd] = alpha[hd:hd + 1] * acc_scr[hd] + _dot(vt_ref[hd], prob.astype(MXU_DTYPE))
        l_scr[...] = alpha * l_scr[...] + jnp.concatenate(col_sum, axis=0)
        top_scr[...] = jnp.concatenate(col_max, axis=0)

    for parity in range(2):
        pl.when(kj % 2 == parity)(functools.partial(attend, parity, 1 - parity))

    @pl.when(kj == n_att)
    def _finish():
        for hd in range(ATT_HEADS):
            hs = slice(hd * ATT_HEAD_DIM, (hd + 1) * ATT_HEAD_DIM)
            o_ref[:, hs] = (acc_scr[hd] / l_scr[hd:hd + 1, :]).T.astype(o_ref.dtype)


def _dsa(qit, lohi, ki, qt, k, vt):
    seq = k.shape[1]
    tq, tk = DSA_Q_TILE, DSA_K_TILE
    assert seq % tk == 0 and tk % tq == 0 and tq % DSA_CHUNK == 0
    assert k.shape == (ATT_HEADS, seq, ATT_HEAD_DIM)
    assert qt.shape == (seq // tq, ATT_HEADS, ATT_HEAD_DIM, tq) and vt.shape == (seq // tk, ATT_HEADS, ATT_HEAD_DIM, tk)
    nq = seq // tq
    def n_att(i):
        return ((i + 1) * tq - 1) // tk + 1

    pairs = [(i, j) for i in range(nq) for j in range(n_att(i) + 1)]
    qi_idx = jnp.asarray([pq for pq, _ in pairs], jnp.int32)
    kj_idx = jnp.asarray([pk for _, pk in pairs], jnp.int32)
    top_k = min(TOPK_MAX, seq // 4)
    idx_bits = max(1, (seq - 1).bit_length())
    grid_spec = pltpu.PrefetchScalarGridSpec(
        num_scalar_prefetch=2,
        grid=(len(pairs),),
        in_specs=[
            pl.BlockSpec((None, IDX_INNER, tq), lambda p, qi, kj: (qi[p], 0, 0)),
            pl.BlockSpec((None, 16, tq), lambda p, qi, kj: (qi[p], 0, 0)),
            pl.BlockSpec((seq, IDX_HEAD_DIM), lambda p, qi, kj: (0, 0), pipeline_mode=pl.Buffered(1)),
            pl.BlockSpec((None, ATT_HEADS, ATT_HEAD_DIM, tq), lambda p, qi, kj: (qi[p], 0, 0, 0)),
            pl.BlockSpec((ATT_HEADS, tk, ATT_HEAD_DIM),
                         lambda p, qi, kj: (0, jnp.minimum(kj[p], n_att(qi[p]) - 1), 0)),
            pl.BlockSpec((None, ATT_HEADS, ATT_HEAD_DIM, tk),
                         lambda p, qi, kj: (jnp.maximum(kj[p] - 1, 0), 0, 0, 0)),
        ],
        out_specs=pl.BlockSpec((tq, ATT_INNER), lambda p, qi, kj: (qi[p], 0)),
        scratch_shapes=[pltpu.VMEM((seq, tq), jnp.int32),
                        pltpu.VMEM((seq, tq), jnp.bfloat16),
                        pltpu.VMEM((2, IDX_HEADS, DSA_CHUNK, tq), jnp.float32),
                        pltpu.VMEM((8, tq), jnp.int32),
                        pltpu.VMEM((ATT_HEADS, tq), jnp.float32),
                        pltpu.VMEM((ATT_HEADS, tq), jnp.float32),
                        pltpu.VMEM((ATT_HEADS, ATT_HEAD_DIM, tq), jnp.float32),
                        pltpu.VMEM((2, ATT_HEADS, tk, tq), jnp.float32),
                        pltpu.VMEM((ATT_HEADS, tq), jnp.float32)],
    )
    return pl.pallas_call(
        functools.partial(_dsa_body, top_k=top_k, idx_bits=idx_bits),
        out_shape=jax.ShapeDtypeStruct((seq, ATT_INNER), MXU_DTYPE),
        grid_spec=grid_spec,
        compiler_params=_params(("arbitrary",)),
        name="dsa",
    )(qi_idx, kj_idx, qit, lohi, ki, qt, k, vt)


def _even_out_body(h_ref, y_ref, o_ref, wy_ref, wo_ref, out_ref):
    out_ref[...] = h_ref[...] + _dot(y_ref[...], wy_ref[...]) + _dot(o_ref[...], wo_ref[...])


def _even_out(h, y, o, wy, wo):
    seq = h.shape[0]
    tm = min(ROW_TILE, seq)
    row = pl.BlockSpec((tm, D_MODEL), lambda i: (i, 0))
    return pl.pallas_call(
        _even_out_body,
        out_shape=jax.ShapeDtypeStruct((seq, D_MODEL), jnp.float32),
        grid=(seq // tm,),
        in_specs=[row, row, row, _const_spec((SSD_INNER, D_MODEL)), _const_spec((ATT_INNER, D_MODEL))],
        out_specs=row,
        compiler_params=_params(("parallel",)),
        name="even_out",
    )(h, y, o, wy, wo)


def _odd_body(h_ref, g_ref, win_ref, lng_ref, lnb_ref, ws_ref, bs_ref, wout_ref, o_ref, gated_scr):
    h = h_ref[...]
    tm = h.shape[0]
    xn = _rms(h, g_ref[...]).astype(MXU_DTYPE)
    u = jax.nn.gelu(_dot(xn, win_ref[:, :SG_INNER]))
    v = jax.nn.gelu(_dot(xn, win_ref[:, SG_INNER:]))
    mu = jnp.mean(v, axis=-1, keepdims=True)
    vc = v - mu
    var = jnp.mean(vc * vc, axis=-1, keepdims=True)
    v = (vc * lax.rsqrt(var + EPS) * lng_ref[...] + lnb_ref[...]).astype(MXU_DTYPE)
    ri = lax.broadcasted_iota(jnp.int32, (SG_CHUNK, SG_CHUNK), 0)
    ci = lax.broadcasted_iota(jnp.int32, (SG_CHUNK, SG_CHUNK), 1)
    gw = SG_INNER // SG_GROUPS
    for g in range(SG_GROUPS):
        w = jnp.where(ri >= ci, ws_ref[g], 0.0).astype(MXU_DTYPE)
        gs = slice(g * gw, (g + 1) * gw)
        for c in range(tm // SG_CHUNK):
            rs = slice(c * SG_CHUNK, (c + 1) * SG_CHUNK)
            mixed = _dot(w, v[rs, gs]) + bs_ref[:, gs]
            gated_scr[rs, gs] = (u[rs, gs] * mixed).astype(gated_scr.dtype)
    o_ref[...] = h + _dot(gated_scr[...], wout_ref[...])


def _odd(h, g, win, ln_g, ln_b, w_s, bs_full, wout):
    seq = h.shape[0]
    tm = min(ROW_TILE, seq)
    row = pl.BlockSpec((tm, D_MODEL), lambda i: (i, 0))
    return pl.pallas_call(
        _odd_body,
        out_shape=jax.ShapeDtypeStruct((seq, D_MODEL), jnp.float32),
        grid=(seq // tm,),
        in_specs=[row, _const_spec((1, D_MODEL)), _const_spec((D_MODEL, 2 * SG_INNER)),
                  _const_spec((1, SG_INNER)), _const_spec((1, SG_INNER)),
                  _const_spec((SG_GROUPS, SG_CHUNK, SG_CHUNK)), _const_spec((SG_CHUNK, SG_INNER)),
                  _const_spec((SG_INNER, D_MODEL))],
        out_specs=row,
        scratch_shapes=[pltpu.VMEM((tm, SG_INNER), MXU_DTYPE)],
        compiler_params=_params(("parallel",)),
        name="odd",
    )(h, g, win, ln_g, ln_b, w_s, bs_full, wout)


def _rope_inputs(seq):
    def tables(rot_dim):
        inv = 1.0 / (ROPE_THETA ** (jnp.arange(0, rot_dim, 2, dtype=jnp.float32) / rot_dim))
        ang = jnp.arange(seq, dtype=jnp.float32)[:, None] * inv[None, :]
        return jnp.cos(ang), jnp.sin(ang)

    def lane_tables(cos, sin):
        half = cos.shape[1]
        pad = LANES - 2 * half
        c = jnp.concatenate([cos, cos, jnp.ones((seq, pad), jnp.float32)], axis=1)
        s_lo = jnp.concatenate([-sin, jnp.zeros((seq, LANES - half), jnp.float32)], axis=1)
        s_hi = jnp.concatenate([jnp.zeros((seq, half), jnp.float32), sin,
                                jnp.zeros((seq, pad), jnp.float32)], axis=1)
        return c, s_lo, s_hi

    cos_a, sin_a = tables(ATT_HEAD_DIM // ROPE_FRACTION)
    cos_i, sin_i = tables(IDX_HEAD_DIM // ROPE_FRACTION)
    return (*lane_tables(cos_a, sin_a), *lane_tables(cos_i, sin_i), cos_a.T, sin_a.T, cos_i.T, sin_i.T)


def _even_weights(w_in):
    offs = np.cumsum((SSD_INNER, SSD_CONV_DIM, SSD_HEADS, ATT_INNER, ATT_INNER, ATT_INNER,
                      IDX_INNER, IDX_HEAD_DIM, IDX_HEADS))[:-1].tolist()
    z, xbc, dt, q, k, v, qi, ki, wi = jnp.split(w_in, offs, axis=-1)
    pad = jnp.zeros((D_MODEL, LANES - IDX_HEAD_DIM - SSD_HEADS), w_in.dtype)
    wn = jnp.concatenate([z, xbc, k, ki, dt, pad], axis=1).astype(MXU_DTYPE)
    wi_pad = jnp.zeros((D_MODEL, _T_DT - _T_WI - IDX_HEADS), w_in.dtype)
    wt = jnp.concatenate([q, v, qi, wi, wi_pad, dt], axis=1).T.astype(MXU_DTYPE)
    assert wn.shape == (D_MODEL, _N_END) and wt.shape == (_T_END, D_MODEL)
    return wn, wt


def _even_mixer(h, g, w_in, conv_w, conv_b, dt_bias, a_log, d_skip, gate_norm, w_out, rope):
    wn, wt = _even_weights(w_in)
    z, xbc, small, k, ki, qt, vt, qit, lohi, dtt = _even_in(h, g, wn, wt, rope)
    y = _ssd(xbc, small, dtt, z, conv_w, conv_b, dt_bias, a_log, d_skip, gate_norm)
    o = _dsa(qit, lohi, ki, qt, k, vt)
    w_out = w_out.astype(MXU_DTYPE)
    return _even_out(h, y, o, w_out[:SSD_INNER], w_out[SSD_INNER:])


def _forward(x, norm_g, final_g, ffn_w_gu, ffn_w_down, ev_w_in, ev_conv_w, ev_conv_b, ev_dt_bias,
             ev_a_log, ev_d, ev_gate_norm, ev_w_out, od_w_in, od_ln_g, od_ln_b, od_w_s, od_b_s, od_w_out):
    bsz, seq, _ = x.shape
    depth = norm_g.shape[0]
    rope = _rope_inputs(seq)
    wg = ffn_w_gu[..., :FFN_HIDDEN].astype(MXU_DTYPE)
    wu = ffn_w_gu[..., FFN_HIDDEN:].astype(MXU_DTYPE)
    wd = ffn_w_down.astype(MXU_DTYPE)
    fg = final_g[None, :]
    outs = []
    for b in range(bsz):
        h = x[b]
        for layer in range(depth):
            j = layer // 2
            h = _ffn(h, norm_g[layer, 0][None, :], wg[layer, 0], wu[layer, 0], wd[layer, 0], fg, False)
            g1 = norm_g[layer, 1][None, :]
            if layer % 2 == 0:
                h = _even_mixer(h, g1, ev_w_in[j], ev_conv_w[j], ev_conv_b[j], ev_dt_bias[j], ev_a_log[j],
                                ev_d[j], ev_gate_norm[j], ev_w_out[j], rope)
            else:
                bs_full = jnp.repeat(od_b_s[j].T, SG_INNER // SG_GROUPS, axis=1)
                h = _odd(h, g1, od_w_in[j].astype(MXU_DTYPE), od_ln_g[j][None, :], od_ln_b[j][None, :],
                         od_w_s[j], bs_full, od_w_out[j].astype(MXU_DTYPE))
            h = _ffn(h, norm_g[layer, 2][None, :], wg[layer, 1], wu[layer, 1], wd[layer, 1], fg,
                     layer == depth - 1)
        outs.append(h)
    return outs[0][None] if bsz == 1 else jnp.stack(outs, axis=0)


def kernel(x, norm_g, final_g, ffn_w_gu, ffn_w_down, ev_w_in, ev_conv_w, ev_conv_b, ev_dt_bias, ev_a_log,
           ev_d, ev_gate_norm, ev_w_out, od_w_in, od_ln_g, od_ln_b, od_w_s, od_b_s, od_w_out):
    return _forward(x, norm_g, final_g, ffn_w_gu, ffn_w_down, ev_w_in, ev_conv_w, ev_conv_b, ev_dt_bias,
                    ev_a_log, ev_d, ev_gate_norm, ev_w_out, od_w_in, od_ln_g, od_ln_b, od_w_s, od_b_s,
                    od_w_out)
```

```python
import functools
import math

import numpy as np
import jax
import jax.numpy as jnp
from jax import lax
from jax.experimental import pallas as pl
from jax.experimental.pallas import tpu as pltpu

D_MODEL = 1024
SSD_HEADS = 16
SSD_HEAD_DIM = 64
SSD_INNER = SSD_HEADS * SSD_HEAD_DIM
SSD_GROUPS = 2
SSD_STATE = 128
SSD_CONV = 4
SSD_CHUNK = 512
SSD_CONV_DIM = SSD_INNER + 2 * SSD_GROUPS * SSD_STATE
ATT_HEADS = 8
ATT_HEAD_DIM = 128
ATT_INNER = ATT_HEADS * ATT_HEAD_DIM
IDX_HEADS = 8
IDX_HEAD_DIM = 64
IDX_INNER = IDX_HEADS * IDX_HEAD_DIM
TOPK_MAX = 256
SG_CHUNK = 128
SG_GROUPS = 8
SG_INNER = 2 * D_MODEL
FFN_HIDDEN = 2816
ROPE_THETA = 500000.0
ROPE_FRACTION = 4
EPS = 1e-6

ATT_ROT_HALF = ATT_HEAD_DIM // ROPE_FRACTION // 2
IDX_ROT_HALF = IDX_HEAD_DIM // ROPE_FRACTION // 2

MXU_DTYPE = jnp.bfloat16
LANES = 128
VMEM_LIMIT_BYTES = 56 * 1024 * 1024

ROW_TILE = 512
FFN_HIDDEN_CHUNK = FFN_HIDDEN // 11
DSA_Q_TILE = 256
DSA_K_TILE = 512
DSA_CHUNK = 256
TIE_EXTRACT_MAX = 8
SEARCH_ROUND = 5
LEAD_MASK = -(2 ** 16)
QUIET_NAN_BITS = 0x7FC00000
FINITE_KEY = 0x7F7FFFFF
MIN_NORMAL_EXP = 0x00800000
MIN_NORMAL_CODE = MIN_NORMAL_EXP >> 16
INT_MIN = -(2 ** 31)
NEG_BIG = -0.7 * float(np.finfo(np.float32).max)

_N_Z = 0
_N_XBC = _N_Z + SSD_INNER
_N_K = _N_XBC + SSD_CONV_DIM
_N_SMALL = _N_K + ATT_INNER
_N_END = _N_SMALL + LANES
_SM_KI = 0
_SM_DT = IDX_HEAD_DIM
_T_Q = 0
_T_V = _T_Q + ATT_INNER
_T_QI = _T_V + ATT_INNER
_T_WI = _T_QI + IDX_INNER
_T_DT = _T_WI + 16
_T_END = _T_DT + SSD_HEADS


def _dot(a, b):
    return jnp.dot(a, b, preferred_element_type=jnp.float32)


def _rms(x, g):
    return x * lax.rsqrt(jnp.mean(x * x, axis=-1, keepdims=True) + EPS) * g


def _softplus(x):
    return jnp.maximum(x, 0.0) + jnp.log1p(jnp.exp(-jnp.abs(x)))


def _split3(x):
    hi = x.astype(MXU_DTYPE)
    r = x - hi.astype(jnp.float32)
    mid = r.astype(MXU_DTYPE)
    lo = (r - mid.astype(jnp.float32)).astype(MXU_DTYPE)
    return hi, mid, lo


def _const_spec(shape):
    zeros = (0,) * len(shape)
    return pl.BlockSpec(shape, lambda *_: zeros, pipeline_mode=pl.Buffered(1))


def _params(sem):
    return pltpu.CompilerParams(dimension_semantics=sem, vmem_limit_bytes=VMEM_LIMIT_BYTES)


def _ffn_body(h_ref, g_ref, wg_ref, wu_ref, wd_ref, fg_ref, o_ref, *, final):
    h = h_ref[...]
    xn = _rms(h, g_ref[...]).astype(MXU_DTYPE)
    acc = None
    for c in range(FFN_HIDDEN // FFN_HIDDEN_CHUNK):
        sl = slice(c * FFN_HIDDEN_CHUNK, (c + 1) * FFN_HIDDEN_CHUNK)
        gate = _dot(xn, wg_ref[:, sl])
        up = _dot(xn, wu_ref[:, sl])
        act = (gate * jax.nn.sigmoid(gate) * up).astype(MXU_DTYPE)
        part = _dot(act, wd_ref[sl, :])
        acc = part if acc is None else acc + part
    out = h + 0.5 * acc
    if final:
        out = _rms(out, fg_ref[...])
    o_ref[...] = out


def _ffn(h, g, wg, wu, wd, final_g, final):
    seq = h.shape[0]
    tm = min(ROW_TILE, seq)
    row = pl.BlockSpec((tm, D_MODEL), lambda i: (i, 0))
    return pl.pallas_call(
        functools.partial(_ffn_body, final=final),
        out_shape=jax.ShapeDtypeStruct((seq, D_MODEL), jnp.float32),
        grid=(seq // tm,),
        in_specs=[row, _const_spec((1, D_MODEL)), _const_spec((D_MODEL, FFN_HIDDEN)),
                  _const_spec((D_MODEL, FFN_HIDDEN)), _const_spec((FFN_HIDDEN, D_MODEL)),
                  _const_spec((1, D_MODEL))],
        out_specs=row,
        compiler_params=_params(("parallel",)),
        name="ffn",
    )(h, g, wg, wu, wd, final_g)


def _rope_lanes(x, c, s_lo, s_hi, half):
    return x * c + pltpu.roll(x, LANES - half, 1) * s_lo + pltpu.roll(x, half, 1) * s_hi


def _rope_rows(x, c, s, half):
    x1, x2 = x[:half], x[half:2 * half]
    return jnp.concatenate([x1 * c - x2 * s, x1 * s + x2 * c, x[2 * half:]], axis=0)


def _even_in_body(h_ref, g_ref, wn_ref, wt_ref, ca_ref, sal_ref, sah_ref, ci_ref, sil_ref, sih_ref,
                  cat_ref, sat_ref, cit_ref, sit_ref,
                  z_ref, xbc_ref, small_ref, k_ref, ki_ref, qt_ref, vt_ref, qit_ref, lohi_ref, dtt_ref):
    xn = _rms(h_ref[...], g_ref[...])
    xnb = xn.astype(MXU_DTYPE)
    xnt = xn.T.astype(MXU_DTYPE)

    z_ref[...] = _dot(xnb, wn_ref[:, _N_Z:_N_XBC])
    xbc_ref[...] = _dot(xnb, wn_ref[:, _N_XBC:_N_K])
    kk = _dot(xnb, wn_ref[:, _N_K:_N_SMALL])
    ca, sal, sah = ca_ref[...], sal_ref[...], sah_ref[...]
    for hd in range(ATT_HEADS):
        sl = slice(hd * ATT_HEAD_DIM, (hd + 1) * ATT_HEAD_DIM)
        k_ref[hd] = _rope_lanes(kk[:, sl], ca, sal, sah, ATT_ROT_HALF).astype(k_ref.dtype)
    sm = _dot(xnb, wn_ref[:, _N_SMALL:_N_END])
    small_ref[...] = sm
    smr = _rope_lanes(sm, ci_ref[...], sil_ref[...], sih_ref[...], IDX_ROT_HALF)
    ki_ref[...] = smr[:, _SM_KI:_SM_KI + IDX_HEAD_DIM].astype(ki_ref.dtype)

    att_scale = ATT_HEAD_DIM ** -0.5 * math.log2(math.e)
    qt = _dot(wt_ref[_T_Q:_T_V, :], xnt)
    cat, sat = cat_ref[...], sat_ref[...]
    def put(ref, where, val):
        tb = ref.shape[-1]
        for b in range(ref.shape[0]):
            ref[(b, *where, slice(None))] = val[:, b * tb:(b + 1) * tb].astype(ref.dtype)

    vt = _dot(wt_ref[_T_V:_T_QI, :], xnt)
    for hd in range(ATT_HEADS):
        sl = slice(hd * ATT_HEAD_DIM, (hd + 1) * ATT_HEAD_DIM)
        put(qt_ref, (hd, slice(None)), _rope_rows(qt[sl], cat, sat, ATT_ROT_HALF) * att_scale)
        put(vt_ref, (hd, slice(None)), vt[sl])
    tail = _dot(wt_ref[_T_WI:_T_END, :], xnt)
    wit = tail[0:IDX_HEADS] * (IDX_HEADS ** -0.5 * IDX_HEAD_DIM ** -0.5)
    dtt_ref[...] = tail[_T_DT - _T_WI:_T_END - _T_WI]
    pos = wit > 0.0
    put(lohi_ref, (slice(0, 8),), jnp.where(pos, 0.0, -jnp.inf))
    put(lohi_ref, (slice(8, 16),), jnp.where(pos, jnp.inf, 0.0))
    qit = _dot(wt_ref[_T_QI:_T_WI, :], xnt)
    cit, sit = cit_ref[...], sit_ref[...]
    for hd in range(IDX_HEADS):
        sl = slice(hd * IDX_HEAD_DIM, (hd + 1) * IDX_HEAD_DIM)
        put(qit_ref, (sl,), _rope_rows(qit[sl], cit, sit, IDX_ROT_HALF) * wit[hd:hd + 1])


def _even_in(h, g, wn, wt, rope):
    seq = h.shape[0]
    tm = min(ROW_TILE, seq)
    f32 = jnp.float32

    def rows(width):
        return pl.BlockSpec((tm, width), lambda i: (i, 0))

    def cols(height):
        return pl.BlockSpec((height, tm), lambda i: (0, i))

    def blocks(height, tb):
        return pl.BlockSpec((tm // tb, height, tb), lambda i: (i, 0, 0))

    def head_blocks(tb):
        return pl.BlockSpec((tm // tb, ATT_HEADS, ATT_HEAD_DIM, tb), lambda i: (i, 0, 0, 0))

    assert tm % DSA_K_TILE == 0 and tm % DSA_Q_TILE == 0
    nq, nk = seq // DSA_Q_TILE, seq // DSA_K_TILE
    out_shape = (
        jax.ShapeDtypeStruct((seq, SSD_INNER), f32),
        jax.ShapeDtypeStruct((seq, SSD_CONV_DIM), f32),
        jax.ShapeDtypeStruct((seq, LANES), f32),
        jax.ShapeDtypeStruct((ATT_HEADS, seq, ATT_HEAD_DIM), MXU_DTYPE),
        jax.ShapeDtypeStruct((seq, IDX_HEAD_DIM), MXU_DTYPE),
        jax.ShapeDtypeStruct((nq, ATT_HEADS, ATT_HEAD_DIM, DSA_Q_TILE), MXU_DTYPE),
        jax.ShapeDtypeStruct((nk, ATT_HEADS, ATT_HEAD_DIM, DSA_K_TILE), MXU_DTYPE),
        jax.ShapeDtypeStruct((nq, IDX_INNER, DSA_Q_TILE), MXU_DTYPE),
        jax.ShapeDtypeStruct((nq, 16, DSA_Q_TILE), f32),
        jax.ShapeDtypeStruct((SSD_HEADS, seq), f32),
    )
    out_specs = (rows(SSD_INNER), rows(SSD_CONV_DIM), rows(LANES),
                 pl.BlockSpec((ATT_HEADS, tm, ATT_HEAD_DIM), lambda i: (0, i, 0)), rows(IDX_HEAD_DIM),
                 head_blocks(DSA_Q_TILE), head_blocks(DSA_K_TILE), blocks(IDX_INNER, DSA_Q_TILE),
                 blocks(16, DSA_Q_TILE), cols(SSD_HEADS))
    in_specs = [rows(D_MODEL), _const_spec((1, D_MODEL)), _const_spec(wn.shape), _const_spec(wt.shape)]
    in_specs += [rows(LANES)] * 6
    in_specs += [cols(ATT_ROT_HALF)] * 2 + [cols(IDX_ROT_HALF)] * 2
    return pl.pallas_call(
        _even_in_body,
        out_shape=out_shape,
        grid=(seq // tm,),
        in_specs=in_specs,
        out_specs=out_specs,
        compiler_params=_params(("parallel",)),
        name="even_in",
    )(h, g, wn, wt, *rope)


def _ssd_body(xbc_ref, small_ref, dtt_ref, z_ref, cw_ref, cb_ref, dtb_ref, dtbt_ref, a_ref, at_ref,
              drep_ref, gn_ref, o_ref, tail_scr, st_scr, y_scr, xdec_scr, decay_scr):
    q = SSD_CHUNK
    f32 = jnp.float32

    @pl.when(pl.program_id(0) == 0)
    def _():
        tail_scr[...] = jnp.zeros_like(tail_scr)
        st_scr[...] = jnp.zeros_like(st_scr)

    x = xbc_ref[...]
    tail = tail_scr[...]
    row8 = lax.broadcasted_iota(jnp.int32, (8, SSD_CONV_DIM), 0)
    conv = x * cw_ref[SSD_CONV - 1:SSD_CONV, :] + cb_ref[...]
    for shift in range(1, SSD_CONV):
        rolled = pltpu.roll(x, shift, 0)
        head = jnp.where(row8 < shift, pltpu.roll(tail, shift, 0), rolled[0:8])
        shifted = jnp.concatenate([head, rolled[8:]], axis=0)
        conv = conv + shifted * cw_ref[SSD_CONV - 1 - shift:SSD_CONV - shift, :]
    tail_scr[...] = x[q - 8:q]
    xbc = conv * jax.nn.sigmoid(conv)
    xs = xbc[:, :SSD_INNER]
    bm = xbc[:, SSD_INNER:SSD_INNER + SSD_GROUPS * SSD_STATE]
    cm = xbc[:, SSD_INNER + SSD_GROUPS * SSD_STATE:]

    dt_col = _softplus(small_ref[:, _SM_DT:_SM_DT + SSD_HEADS] + dtb_ref[...])
    dt_row = _softplus(dtt_ref[...] + dtbt_ref[...])
    adt_col = dt_col * (-jnp.exp(a_ref[...]))
    adt_row = dt_row * (-jnp.exp(at_ref[...]))
    ri = lax.broadcasted_iota(jnp.int32, (q, q), 0)
    ci = lax.broadcasted_iota(jnp.int32, (q, q), 1)
    causal = ri >= ci
    tril = jnp.where(causal, 1.0, 0.0).astype(MXU_DTYPE)
    triu = jnp.where(ri <= ci, 1.0, 0.0).astype(MXU_DTYPE)
    cs_col = sum(_dot(tril, p) for p in _split3(adt_col))
    cs_row = sum(_dot(p, triu) for p in _split3(adt_row))

    cmb = cm.astype(MXU_DTYPE)
    bmb = bm.astype(MXU_DTYPE)
    for g in range(SSD_GROUPS):
        gs = slice(g * SSD_STATE, (g + 1) * SSD_STATE)
        cb = lax.dot_general(cmb[:, gs], bmb[:, gs], (((1,), (1,)), ((), ())),
                             preferred_element_type=f32)
        kpg = SSD_HEADS // SSD_GROUPS
        for hd in range(g * kpg, (g + 1) * kpg):
            hs = slice(hd * SSD_HEAD_DIM, (hd + 1) * SSD_HEAD_DIM)
            col = cs_col[:, hd:hd + 1]
            row = cs_row[hd:hd + 1, :]
            last = cs_row[hd:hd + 1, q - 1:q]
            decay = jnp.exp(jnp.where(causal, col - row, -jnp.inf))
            xdt = xs[:, hs] * dt_col[:, hd:hd + 1]
            y = _dot((cb * decay).astype(MXU_DTYPE), xdt.astype(MXU_DTYPE))
            y_off = _dot(cmb[:, gs], st_scr[:, hs].astype(MXU_DTYPE))
            y_scr[:, hs] = y + y_off * jnp.exp(col)
            xdec_scr[:, hs] = (xdt * jnp.exp(last - col)).astype(xdec_scr.dtype)
            decay_scr[:, hs] = jnp.broadcast_to(jnp.exp(last), (1, SSD_HEAD_DIM))
        ws = slice(g * kpg * SSD_HEAD_DIM, (g + 1) * kpg * SSD_HEAD_DIM)
        bmt = bm[:, gs].T.astype(MXU_DTYPE)
        st_scr[:, ws] = st_scr[:, ws] * decay_scr[:, ws] + _dot(bmt, xdec_scr[:, ws])

    z = z_ref[...]
    y = (y_scr[...] + drep_ref[...] * xs) * (z * jax.nn.sigmoid(z))
    gw = SSD_INNER // SSD_GROUPS
    for g in range(SSD_GROUPS):
        sl = slice(g * gw, (g + 1) * gw)
        seg = y[:, sl]
        seg = seg * lax.rsqrt(jnp.mean(seg * seg, axis=-1, keepdims=True) + EPS)
        o_ref[:, sl] = (seg * gn_ref[:, sl]).astype(o_ref.dtype)


def _ssd(xbc, small, dtt, z, conv_w, conv_b, dt_bias, a_log, d_skip, gate_norm):
    seq = xbc.shape[0]
    q = SSD_CHUNK
    f32 = jnp.float32

    def rows(width):
        return pl.BlockSpec((q, width), lambda i: (i, 0))

    drep = jnp.repeat(d_skip, SSD_HEAD_DIM)[None, :]
    return pl.pallas_call(
        _ssd_body,
        out_shape=jax.ShapeDtypeStruct((seq, SSD_INNER), MXU_DTYPE),
        grid=(seq // q,),
        in_specs=[rows(SSD_CONV_DIM), rows(LANES), pl.BlockSpec((SSD_HEADS, q), lambda i: (0, i)),
                  rows(SSD_INNER), _const_spec((SSD_CONV, SSD_CONV_DIM)), _const_spec((1, SSD_CONV_DIM)),
                  _const_spec((1, SSD_HEADS)), _const_spec((SSD_HEADS, 1)),
                  _const_spec((1, SSD_HEADS)), _const_spec((SSD_HEADS, 1)),
                  _const_spec((1, SSD_INNER)), _const_spec((1, SSD_INNER))],
        out_specs=rows(SSD_INNER),
        scratch_shapes=[pltpu.VMEM((8, SSD_CONV_DIM), f32),
                        pltpu.VMEM((SSD_STATE, SSD_INNER), f32),
                        pltpu.VMEM((q, SSD_INNER), f32),
                        pltpu.VMEM((q, SSD_INNER), MXU_DTYPE),
                        pltpu.VMEM((1, SSD_INNER), f32)],
        compiler_params=_params(("arbitrary",)),
        name="ssd",
    )(xbc, small, dtt, z, conv_w, conv_b[None, :], dt_bias[None, :], dt_bias[:, None],
      a_log[None, :], a_log[:, None], drep, gate_norm[None, :])


def _dsa_body(qi_ref, kj_ref, qit_ref, lohi_ref, ki_ref, qt_ref, k_ref, vt_ref, o_ref,
              keys_scr, lead_scr, y_scr, thr_scr, m_scr, l_scr, acc_scr, logit_scr, top_scr, *, top_k, idx_bits):
    tq, tk, ck = DSA_Q_TILE, DSA_K_TILE, DSA_CHUNK
    i32 = jnp.int32
    p = pl.program_id(0)
    qi = qi_ref[p]
    kj = kj_ref[p]

    @pl.when(kj == 0)
    def _select():
        chunks_per_tile = tk // ck
        n_tiles = ((qi + 1) * tq + tk - 1) // tk
        n_chunks = n_tiles * chunks_per_tile
        lo = lohi_ref[0:8, :]
        hi = lohi_ref[8:16, :]
        rows = 4 * 8
        t_pos = qi * tq + lax.broadcasted_iota(i32, (rows, tq), 1)
        s_off = lax.broadcasted_iota(i32, (ck, tq), 0)
        s_rows = lax.broadcasted_iota(i32, (rows, tq), 0)

        def project(c, slot):
            kic = ki_ref[pl.ds(pl.multiple_of(c * ck, ck), ck), :]
            for hd in range(IDX_HEADS):
                y_scr[slot, hd] = _dot(kic, qit_ref[hd * IDX_HEAD_DIM:(hd + 1) * IDX_HEAD_DIM, :])

        def emit_keys(c, slot, diagonal):
            for g in range(ck // rows):
                r0 = pl.multiple_of(c * ck + g * rows, rows)
                gs = slice(g * rows, (g + 1) * rows)
                sc = None
                for hd in range(IDX_HEADS):
                    term = jnp.minimum(jnp.maximum(y_scr[slot, hd, gs, :], lo[hd:hd + 1]), hi[hd:hd + 1])
                    sc = term if sc is None else sc + term
                bits = pltpu.bitcast(sc, i32)
                bits = jnp.where((bits & 0x7FFFFFFF) < MIN_NORMAL_EXP, 0, bits)
                key = bits ^ ((bits >> 31) & 0x7FFFFFFF)
                lead = bits & LEAD_MASK
                if diagonal:
                    causal = r0 + s_rows <= t_pos
                    key = jnp.where(causal, key, INT_MIN)
                    lead = jnp.where(causal, lead, QUIET_NAN_BITS)
                keys_scr[pl.ds(r0, rows), :] = key
                lead_scr[pl.ds(r0, rows), :] = pltpu.bitcast(lead, jnp.float32).astype(jnp.bfloat16)

        assert chunks_per_tile % 2 == 0

        def score_tile(t, diagonal):
            for j in range(chunks_per_tile):
                c = t * chunks_per_tile + j
                emit_keys(c, j % 2, diagonal)
                if not (diagonal and j == chunks_per_tile - 1):
                    project(c + 1, (j + 1) % 2)

        project(0, 0)
        lax.fori_loop(0, n_tiles - 1, lambda t, carry: (score_tile(t, False), carry)[1], 0)
        score_tile(n_tiles - 1, True)

        partial_rows = 4 * 8
        s_tile = lax.broadcasted_iota(i32, (partial_rows, tq), 0)

        def key_groups(c):
            for g in range(tk // partial_rows):
                r0 = pl.multiple_of(c * tk + g * partial_rows, partial_rows)
                yield keys_scr[pl.ds(r0, partial_rows), :], r0

        def count(pred):
            def body(c, acc):
                for kk, r0 in key_groups(c):
                    acc = acc + jnp.where(pred(kk, r0), 1.0, 0.0)
                return acc
            acc = lax.fori_loop(0, n_tiles, body, jnp.zeros((partial_rows, tq), jnp.float32))
            return acc.sum(axis=0, keepdims=True)

        def count_lead(cand_code):
            cand_code = jnp.where((cand_code > 0) & (cand_code < MIN_NORMAL_CODE), MIN_NORMAL_CODE,
                                  jnp.where((cand_code < 0) & (cand_code >= -MIN_NORMAL_CODE), 0, cand_code))
            cand_bits = (cand_code ^ ((cand_code >> 31) & 0x7FFF)) << 16
            cand = jnp.broadcast_to(pltpu.bitcast(cand_bits, jnp.float32), (16, tq)).astype(jnp.bfloat16)
            one, zero = jnp.ones((), jnp.bfloat16), jnp.zeros((), jnp.bfloat16)

            def body(c, acc):
                parts = []
                for g in range(tk // 128):
                    r0 = pl.multiple_of(c * tk + g * 128, 128)
                    lead = lead_scr[pl.ds(r0, 128), :].reshape(8, 16, tq)
                    parts += [jnp.where(lead[j] >= cand, one, zero) for j in range(8)]
                while len(parts) > 1:
                    parts = [a + b for a, b in zip(parts[0::2], parts[1::2])]
                return acc + parts[0].astype(jnp.float32)
            acc = lax.fori_loop(0, n_tiles, body, jnp.zeros((16, tq), jnp.float32))
            return acc.sum(axis=0, keepdims=True)

        k_f = jnp.float32(top_k)
        n_t = (qi * tq + 1 + lax.broadcasted_iota(i32, (1, tq), 1)).astype(jnp.float32)

        def lead_step(b, carry):
            code, c_lo, c_hi = carry
            cand = code + lax.shift_left(i32(1), i32(15) - b)
            cnt = count_lead(cand)
            ok = cnt >= k_f
            return jnp.where(ok, cand, code), jnp.where(ok, cnt, c_lo), jnp.where(ok, c_hi, cnt)

        code, c_lo, c_hi = lax.fori_loop(
            0, 16, lead_step, (jnp.full((1, tq), -(2 ** 15), i32), n_t, jnp.zeros((1, tq), jnp.float32)))

        lo_k = jnp.maximum(code << 16, INT_MIN + 1)
        hi_k = jnp.where(code == 2 ** 15 - 1, jnp.iinfo(i32).max, (code + 1) << 16)

        def finished(lo_k, hi_k, c_lo):
            return (c_lo <= k_f) | (lo_k + 1 >= hi_k)

        def key_value(kk):
            kk = jnp.clip(kk, -FINITE_KEY, FINITE_KEY)
            return pltpu.bitcast(kk ^ ((kk >> 31) & 0x7FFFFFFF), jnp.float32)

        def tighten(lo_k, hi_k, c_lo, c_hi, done, it):
            def body(c, carry):
                mn, mx = carry
                for kk, _ in key_groups(c):
                    mn = jnp.minimum(mn, jnp.where(kk >= lo_k, kk, jnp.iinfo(i32).max))
                    mx = jnp.maximum(mx, jnp.where(kk < hi_k, kk, INT_MIN))
                return mn, mx
            mn, mx = lax.fori_loop(0, n_tiles, body, (jnp.full((partial_rows, tq), jnp.iinfo(i32).max, i32),
                                                      jnp.full((partial_rows, tq), INT_MIN, i32)))
            mn, mx = mn.min(axis=0, keepdims=True), mx.max(axis=0, keepdims=True)
            return jnp.where(done, lo_k, mn), jnp.where(done, hi_k, mx + 1), c_lo, c_hi

        def probe(lo_k, hi_k, c_lo, c_hi, done, it):
            below, span = c_lo - k_f, c_lo - c_hi
            end = jnp.where(below + below >= span, hi_k - 1, lo_k + 1)
            lo_v, hi_v = key_value(lo_k), key_value(hi_k)
            guess_v = lo_v + (below + 0.5) / jnp.maximum(span, 1.0) * (hi_v - lo_v)
            guess_b = pltpu.bitcast(guess_v, i32)
            guess = guess_b ^ ((guess_b >> 31) & 0x7FFFFFFF)
            mid = (lo_k >> 1) + (hi_k >> 1) + (lo_k & hi_k & 1)
            phase = it % SEARCH_ROUND
            cand = jnp.where(phase == 1, end, jnp.where(phase == 3, mid, guess))
            cand = jnp.where(done, lo_k, jnp.maximum(lo_k + 1, jnp.minimum(hi_k - 1, cand)))
            cnt = count(lambda kk, r0: kk >= cand)
            up = jnp.logical_and(~done, cnt >= k_f)
            down = jnp.logical_and(~done, cnt < k_f)
            return (jnp.where(up, cand, lo_k), jnp.where(down, cand, hi_k),
                    jnp.where(up, cnt, c_lo), jnp.where(down, cnt, c_hi))

        def search_cond(carry):
            return jnp.logical_and(carry[0] < SEARCH_ROUND * 32, carry[1] > 0)

        def search_step(carry):
            it, _, lo_k, hi_k, c_lo, c_hi = carry
            done = finished(lo_k, hi_k, c_lo)
            lo_k, hi_k, c_lo, c_hi = lax.cond(it % SEARCH_ROUND == 0, tighten, probe, lo_k, hi_k, c_lo, c_hi, done, it)
            still = jnp.where(finished(lo_k, hi_k, c_lo), 0.0, 1.0)
            return it + 1, jnp.max(still).astype(i32), lo_k, hi_k, c_lo, c_hi

        active = jnp.max(jnp.where(finished(lo_k, hi_k, c_lo), 0.0, 1.0)).astype(i32)
        _, _, thr, _, n_ge, n_gt = lax.while_loop(search_cond, search_step, (i32(0), active, lo_k, hi_k, c_lo, c_hi))
        thr_scr[0:1, :] = thr

        tied = n_ge > k_f
        need = jnp.where(tied, k_f - n_gt, 0.0)
        max_need = jnp.max(need)

        @pl.when(max_need > 0.0)
        def _ties():
            int_max = jnp.iinfo(i32).max

            def by_extraction(_):
                def next_tie(j, cut):
                    def body(c, mn):
                        for kk, r0 in key_groups(c):
                            idx = r0 + s_tile
                            mn = jnp.minimum(mn, jnp.where((kk == thr) & (idx > cut), idx, int_max))
                        return mn
                    mn = lax.fori_loop(0, n_tiles, body, jnp.full((partial_rows, tq), int_max, i32))
                    return jnp.where(j.astype(jnp.float32) < need, mn.min(axis=0, keepdims=True), cut)
                return lax.fori_loop(0, max_need.astype(i32), next_tie, jnp.full((1, tq), -1, i32))

            def by_index_bits(_):
                def idx_step(b, cut):
                    cand = cut + lax.shift_left(i32(1), i32(idx_bits - 1) - b)
                    cnt = count(lambda kk, r0: jnp.where(kk == thr, r0 + s_tile, cand) < cand)
                    return jnp.where(cnt < need, cand, cut)
                return lax.fori_loop(0, idx_bits, idx_step, jnp.zeros((1, tq), i32))

            cut = lax.cond(max_need <= TIE_EXTRACT_MAX, by_extraction, by_index_bits, 0)
            cut = jnp.where(tied, cut, int_max)

            def drop(c, carry):
                r0 = pl.multiple_of(c * ck, ck)
                kk = keys_scr[pl.ds(r0, ck), :]
                excess = jnp.where(kk == thr, r0 + s_off, INT_MIN) > cut
                keys_scr[pl.ds(r0, ck), :] = jnp.where(excess, INT_MIN, kk)
                return carry

            lax.fori_loop(0, n_chunks, drop, 0)

        m_scr[...] = jnp.full_like(m_scr, NEG_BIG)
        l_scr[...] = jnp.zeros_like(l_scr)
        acc_scr[...] = jnp.zeros_like(acc_scr)
        top_scr[...] = jnp.full_like(top_scr, -jnp.inf)
        logit_scr[1] = jnp.full(logit_scr.shape[1:], -jnp.inf, jnp.float32)

    n_att = ((qi + 1) * tq - 1) // tk + 1
    row0 = pl.multiple_of(jnp.minimum(kj, n_att - 1) * tk, tk)
    thr_now = jnp.where(kj < n_att, thr_scr[0:1, :], jnp.iinfo(i32).max)
    bias = jnp.where(keys_scr[pl.ds(row0, tk), :] >= thr_now, 0.0, -jnp.inf)

    def attend(cur, prev):
        m_old = m_scr[...]
        m_new = jnp.maximum(m_old, top_scr[...])
        alpha = jnp.exp2(m_old - m_new)
        m_scr[...] = m_new
        col_sum, col_max = [], []
        for hd in range(ATT_HEADS):
            logit = _dot(k_ref[hd], qt_ref[hd]) + bias
            logit_scr[cur, hd] = logit
            col_max.append(jnp.max(logit, axis=0, keepdims=True))
            prob = jnp.exp2(logit_scr[prev, hd] - m_new[hd:hd + 1])
            col_sum.append(jnp.sum(prob, axis=0, keepdims=True))
            acc_scr[hd] = alpha[hd:hd + 1] * acc_scr[hd] + _dot(vt_ref[hd], prob.astype(MXU_DTYPE))
        l_scr[...] = alpha * l_scr[...] + jnp.concatenate(col_sum, axis=0)
        top_scr[...] = jnp.concatenate(col_max, axis=0)

    for parity in range(2):
        pl.when(kj % 2 == parity)(functools.partial(attend, parity, 1 - parity))

    @pl.when(kj == n_att)
    def _finish():
        for hd in range(ATT_HEADS):
            hs = slice(hd * ATT_HEAD_DIM, (hd + 1) * ATT_HEAD_DIM)
            o_ref[:, hs] = (acc_scr[hd] / l_scr[hd:hd + 1, :]).T.astype(o_ref.dtype)


def _dsa(qit, lohi, ki, qt, k, vt):
    seq = k.shape[1]
    tq, tk = DSA_Q_TILE, DSA_K_TILE
    assert seq % tk == 0 and tk % tq == 0 and tq % DSA_CHUNK == 0
    assert k.shape == (ATT_HEADS, seq, ATT_HEAD_DIM)
    assert qt.shape == (seq // tq, ATT_HEADS, ATT_HEAD_DIM, tq) and vt.shape == (seq // tk, ATT_HEADS, ATT_HEAD_DIM, tk)
    nq = seq // tq
    def n_att(i):
        return ((i + 1) * tq - 1) // tk + 1

    pairs = [(i, j) for i in range(nq) for j in range(n_att(i) + 1)]
    qi_idx = jnp.asarray([pq for pq, _ in pairs], jnp.int32)
    kj_idx = jnp.asarray([pk for _, pk in pairs], jnp.int32)
    top_k = min(TOPK_MAX, seq // 4)
    idx_bits = max(1, (seq - 1).bit_length())
    grid_spec = pltpu.PrefetchScalarGridSpec(
        num_scalar_prefetch=2,
        grid=(len(pairs),),
        in_specs=[
            pl.BlockSpec((None, IDX_INNER, tq), lambda p, qi, kj: (qi[p], 0, 0)),
            pl.BlockSpec((None, 16, tq), lambda p, qi, kj: (qi[p], 0, 0)),
            pl.BlockSpec((seq, IDX_HEAD_DIM), lambda p, qi, kj: (0, 0), pipeline_mode=pl.Buffered(1)),
            pl.BlockSpec((None, ATT_HEADS, ATT_HEAD_DIM, tq), lambda p, qi, kj: (qi[p], 0, 0, 0)),
            pl.BlockSpec((ATT_HEADS, tk, ATT_HEAD_DIM),
                         lambda p, qi, kj: (0, jnp.minimum(kj[p], n_att(qi[p]) - 1), 0)),
            pl.BlockSpec((None, ATT_HEADS, ATT_HEAD_DIM, tk),
                         lambda p, qi, kj: (jnp.maximum(kj[p] - 1, 0), 0, 0, 0)),
        ],
        out_specs=pl.BlockSpec((tq, ATT_INNER), lambda p, qi, kj: (qi[p], 0)),
        scratch_shapes=[pltpu.VMEM((seq, tq), jnp.int32),
                        pltpu.VMEM((seq, tq), jnp.bfloat16),
                        pltpu.VMEM((2, IDX_HEADS, DSA_CHUNK, tq), jnp.float32),
                        pltpu.VMEM((8, tq), jnp.int32),
                        pltpu.VMEM((ATT_HEADS, tq), jnp.float32),
                        pltpu.VMEM((ATT_HEADS, tq), jnp.float32),
                        pltpu.VMEM((ATT_HEADS, ATT_HEAD_DIM, tq), jnp.float32),
                        pltpu.VMEM((2, ATT_HEADS, tk, tq), jnp.float32),
                        pltpu.VMEM((ATT_HEADS, tq), jnp.float32)],
    )
    return pl.pallas_call(
        functools.partial(_dsa_body, top_k=top_k, idx_bits=idx_bits),
        out_shape=jax.ShapeDtypeStruct((seq, ATT_INNER), MXU_DTYPE),
        grid_spec=grid_spec,
        compiler_params=_params(("arbitrary",)),
        name="dsa",
    )(qi_idx, kj_idx, qit, lohi, ki, qt, k, vt)


def _even_out_body(h_ref, y_ref, o_ref, wy_ref, wo_ref, out_ref):
    out_ref[...] = h_ref[...] + _dot(y_ref[...], wy_ref[...]) + _dot(o_ref[...], wo_ref[...])


def _even_out(h, y, o, wy, wo):
    seq = h.shape[0]
    tm = min(ROW_TILE, seq)
    row = pl.BlockSpec((tm, D_MODEL), lambda i: (i, 0))
    return pl.pallas_call(
        _even_out_body,
        out_shape=jax.ShapeDtypeStruct((seq, D_MODEL), jnp.float32),
        grid=(seq // tm,),
        in_specs=[row, row, row, _const_spec((SSD_INNER, D_MODEL)), _const_spec((ATT_INNER, D_MODEL))],
        out_specs=row,
        compiler_params=_params(("parallel",)),
        name="even_out",
    )(h, y, o, wy, wo)


def _odd_body(h_ref, g_ref, win_ref, lng_ref, lnb_ref, ws_ref, bs_ref, wout_ref, o_ref, gated_scr):
    h = h_ref[...]
    tm = h.shape[0]
    xn = _rms(h, g_ref[...]).astype(MXU_DTYPE)
    u = jax.nn.gelu(_dot(xn, win_ref[:, :SG_INNER]))
    v = jax.nn.gelu(_dot(xn, win_ref[:, SG_INNER:]))
    mu = jnp.mean(v, axis=-1, keepdims=True)
    vc = v - mu
    var = jnp.mean(vc * vc, axis=-1, keepdims=True)
    v = (vc * lax.rsqrt(var + EPS) * lng_ref[...] + lnb_ref[...]).astype(MXU_DTYPE)
    ri = lax.broadcasted_iota(jnp.int32, (SG_CHUNK, SG_CHUNK), 0)
    ci = lax.broadcasted_iota(jnp.int32, (SG_CHUNK, SG_CHUNK), 1)
    gw = SG_INNER // SG_GROUPS
    for g in range(SG_GROUPS):
        w = jnp.where(ri >= ci, ws_ref[g], 0.0).astype(MXU_DTYPE)
        gs = slice(g * gw, (g + 1) * gw)
        for c in range(tm // SG_CHUNK):
            rs = slice(c * SG_CHUNK, (c + 1) * SG_CHUNK)
            mixed = _dot(w, v[rs, gs]) + bs_ref[:, gs]
            gated_scr[rs, gs] = (u[rs, gs] * mixed).astype(gated_scr.dtype)
    o_ref[...] = h + _dot(gated_scr[...], wout_ref[...])


def _odd(h, g, win, ln_g, ln_b, w_s, bs_full, wout):
    seq = h.shape[0]
    tm = min(ROW_TILE, seq)
    row = pl.BlockSpec((tm, D_MODEL), lambda i: (i, 0))
    return pl.pallas_call(
        _odd_body,
        out_shape=jax.ShapeDtypeStruct((seq, D_MODEL), jnp.float32),
        grid=(seq // tm,),
        in_specs=[row, _const_spec((1, D_MODEL)), _const_spec((D_MODEL, 2 * SG_INNER)),
                  _const_spec((1, SG_INNER)), _const_spec((1, SG_INNER)),
                  _const_spec((SG_GROUPS, SG_CHUNK, SG_CHUNK)), _const_spec((SG_CHUNK, SG_INNER)),
                  _const_spec((SG_INNER, D_MODEL))],
        out_specs=row,
        scratch_shapes=[pltpu.VMEM((tm, SG_INNER), MXU_DTYPE)],
        compiler_params=_params(("parallel",)),
        name="odd",
    )(h, g, win, ln_g, ln_b, w_s, bs_full, wout)


def _rope_inputs(seq):
    def tables(rot_dim):
        inv = 1.0 / (ROPE_THETA ** (jnp.arange(0, rot_dim, 2, dtype=jnp.float32) / rot_dim))
        ang = jnp.arange(seq, dtype=jnp.float32)[:, None] * inv[None, :]
        return jnp.cos(ang), jnp.sin(ang)

    def lane_tables(cos, sin):
        half = cos.shape[1]
        pad = LANES - 2 * half
        c = jnp.concatenate([cos, cos, jnp.ones((seq, pad), jnp.float32)], axis=1)
        s_lo = jnp.concatenate([-sin, jnp.zeros((seq, LANES - half), jnp.float32)], axis=1)
        s_hi = jnp.concatenate([jnp.zeros((seq, half), jnp.float32), sin,
                                jnp.zeros((seq, pad), jnp.float32)], axis=1)
        return c, s_lo, s_hi

    cos_a, sin_a = tables(ATT_HEAD_DIM // ROPE_FRACTION)
    cos_i, sin_i = tables(IDX_HEAD_DIM // ROPE_FRACTION)
    return (*lane_tables(cos_a, sin_a), *lane_tables(cos_i, sin_i), cos_a.T, sin_a.T, cos_i.T, sin_i.T)


def _even_weights(w_in):
    offs = np.cumsum((SSD_INNER, SSD_CONV_DIM, SSD_HEADS, ATT_INNER, ATT_INNER, ATT_INNER,
                      IDX_INNER, IDX_HEAD_DIM, IDX_HEADS))[:-1].tolist()
    z, xbc, dt, q, k, v, qi, ki, wi = jnp.split(w_in, offs, axis=-1)
    pad = jnp.zeros((D_MODEL, LANES - IDX_HEAD_DIM - SSD_HEADS), w_in.dtype)
    wn = jnp.concatenate([z, xbc, k, ki, dt, pad], axis=1).astype(MXU_DTYPE)
    wi_pad = jnp.zeros((D_MODEL, _T_DT - _T_WI - IDX_HEADS), w_in.dtype)
    wt = jnp.concatenate([q, v, qi, wi, wi_pad, dt], axis=1).T.astype(MXU_DTYPE)
    assert wn.shape == (D_MODEL, _N_END) and wt.shape == (_T_END, D_MODEL)
    return wn, wt


def _even_mixer(h, g, w_in, conv_w, conv_b, dt_bias, a_log, d_skip, gate_norm, w_out, rope):
    wn, wt = _even_weights(w_in)
    z, xbc, small, k, ki, qt, vt, qit, lohi, dtt = _even_in(h, g, wn, wt, rope)
    y = _ssd(xbc, small, dtt, z, conv_w, conv_b, dt_bias, a_log, d_skip, gate_norm)
    o = _dsa(qit, lohi, ki, qt, k, vt)
    w_out = w_out.astype(MXU_DTYPE)
    return _even_out(h, y, o, w_out[:SSD_INNER], w_out[SSD_INNER:])


def _forward(x, norm_g, final_g, ffn_w_gu, ffn_w_down, ev_w_in, ev_conv_w, ev_conv_b, ev_dt_bias,
             ev_a_log, ev_d, ev_gate_norm, ev_w_out, od_w_in, od_ln_g, od_ln_b, od_w_s, od_b_s, od_w_out):
    bsz, seq, _ = x.shape
    depth = norm_g.shape[0]
    rope = _rope_inputs(seq)
    wg = ffn_w_gu[..., :FFN_HIDDEN].astype(MXU_DTYPE)
    wu = ffn_w_gu[..., FFN_HIDDEN:].astype(MXU_DTYPE)
    wd = ffn_w_down.astype(MXU_DTYPE)
    fg = final_g[None, :]
    outs = []
    for b in range(bsz):
        h = x[b]
        for layer in range(depth):
            j = layer // 2
            h = _ffn(h, norm_g[layer, 0][None, :], wg[layer, 0], wu[layer, 0], wd[layer, 0], fg, False)
            g1 = norm_g[layer, 1][None, :]
            if layer % 2 == 0:
                h = _even_mixer(h, g1, ev_w_in[j], ev_conv_w[j], ev_conv_b[j], ev_dt_bias[j], ev_a_log[j],
                                ev_d[j], ev_gate_norm[j], ev_w_out[j], rope)
            else:
                bs_full = jnp.repeat(od_b_s[j].T, SG_INNER // SG_GROUPS, axis=1)
                h = _odd(h, g1, od_w_in[j].astype(MXU_DTYPE), od_ln_g[j][None, :], od_ln_b[j][None, :],
                         od_w_s[j], bs_full, od_w_out[j].astype(MXU_DTYPE))
            h = _ffn(h, norm_g[layer, 2][None, :], wg[layer, 1], wu[layer, 1], wd[layer, 1], fg,
                     layer == depth - 1)
        outs.append(h)
    return outs[0][None] if bsz == 1 else jnp.stack(outs, axis=0)


def kernel(x, norm_g, final_g, ffn_w_gu, ffn_w_down, ev_w_in, ev_conv_w, ev_conv_b, ev_dt_bias, ev_a_log,
           ev_d, ev_gate_norm, ev_w_out, od_w_in, od_ln_g, od_ln_b, od_w_s, od_b_s, od_w_out):
    return _forward(x, norm_g, final_g, ffn_w_gu, ffn_w_down, ev_w_in, ev_conv_w, ev_conv_b, ev_dt_bias,
                    ev_a_log, ev_d, ev_gate_norm, ev_w_out, od_w_in, od_ln_g, od_ln_b, od_w_s, od_b_s,
                    od_w_out)
```

```python
import functools
import math

import numpy as np
import jax
import jax.numpy as jnp
from jax import lax
from jax.experimental import pallas as pl
from jax.experimental.pallas import tpu as pltpu

D_MODEL = 1024
SSD_HEADS = 16
SSD_HEAD_DIM = 64
SSD_INNER = SSD_HEADS * SSD_HEAD_DIM
SSD_GROUPS = 2
SSD_STATE = 128
SSD_CONV = 4
SSD_CHUNK = 512
SSD_CONV_DIM = SSD_INNER + 2 * SSD_GROUPS * SSD_STATE
ATT_HEADS = 8
ATT_HEAD_DIM = 128
ATT_INNER = ATT_HEADS * ATT_HEAD_DIM
IDX_HEADS = 8
IDX_HEAD_DIM = 64
IDX_INNER = IDX_HEADS * IDX_HEAD_DIM
TOPK_MAX = 256
SG_CHUNK = 128
SG_GROUPS = 8
SG_INNER = 2 * D_MODEL
FFN_HIDDEN = 2816
ROPE_THETA = 500000.0
ROPE_FRACTION = 4
EPS = 1e-6

ATT_ROT_HALF = ATT_HEAD_DIM // ROPE_FRACTION // 2
IDX_ROT_HALF = IDX_HEAD_DIM // ROPE_FRACTION // 2

MXU_DTYPE = jnp.bfloat16
LANES = 128
VMEM_LIMIT_BYTES = 56 * 1024 * 1024

ROW_TILE = 512
FFN_HIDDEN_CHUNK = FFN_HIDDEN // 11
DSA_Q_TILE = 256
DSA_K_TILE = 512
DSA_CHUNK = 256
TIE_EXTRACT_MAX = 8
SEARCH_ROUND = 5
LEAD_MASK = -(2 ** 16)
QUIET_NAN_BITS = 0x7FC00000
FINITE_KEY = 0x7F7FFFFF
MIN_NORMAL_EXP = 0x00800000
MIN_NORMAL_CODE = MIN_NORMAL_EXP >> 16
INT_MIN = -(2 ** 31)
NEG_BIG = -0.7 * float(np.finfo(np.float32).max)

_N_Z = 0
_N_XBC = _N_Z + SSD_INNER
_N_K = _N_XBC + SSD_CONV_DIM
_N_SMALL = _N_K + ATT_INNER
_N_END = _N_SMALL + LANES
_SM_KI = 0
_SM_DT = IDX_HEAD_DIM
_T_Q = 0
_T_V = _T_Q + ATT_INNER
_T_QI = _T_V + ATT_INNER
_T_WI = _T_QI + IDX_INNER
_T_DT = _T_WI + 16
_T_END = _T_DT + SSD_HEADS


def _dot(a, b):
    return jnp.dot(a, b, preferred_element_type=jnp.float32)


def _rms(x, g):
    return x * lax.rsqrt(jnp.mean(x * x, axis=-1, keepdims=True) + EPS) * g


def _softplus(x):
    return jnp.maximum(x, 0.0) + jnp.log1p(jnp.exp(-jnp.abs(x)))


def _split3(x):
    hi = x.astype(MXU_DTYPE)
    r = x - hi.astype(jnp.float32)
    mid = r.astype(MXU_DTYPE)
    lo = (r - mid.astype(jnp.float32)).astype(MXU_DTYPE)
    return hi, mid, lo


def _const_spec(shape):
    zeros = (0,) * len(shape)
    return pl.BlockSpec(shape, lambda *_: zeros, pipeline_mode=pl.Buffered(1))


def _params(sem):
    return pltpu.CompilerParams(dimension_semantics=sem, vmem_limit_bytes=VMEM_LIMIT_BYTES)


def _ffn_body(h_ref, g_ref, wgu_ref, wd_ref, fg_ref, o_ref, *, final):
    h = h_ref[...]
    xn = _rms(h, g_ref[...]).astype(MXU_DTYPE)
    acc = None
    for c in range(FFN_HIDDEN // FFN_HIDDEN_CHUNK):
        sl = slice(c * FFN_HIDDEN_CHUNK, (c + 1) * FFN_HIDDEN_CHUNK)
        gate = _dot(xn, wgu_ref[:, sl])
        up = _dot(xn, wgu_ref[:, FFN_HIDDEN + sl.start:FFN_HIDDEN + sl.stop])
        act = (gate * jax.nn.sigmoid(gate) * up).astype(MXU_DTYPE)
        part = _dot(act, wd_ref[sl, :])
        acc = part if acc is None else acc + part
    out = h + 0.5 * acc
    if final:
        out = _rms(out, fg_ref[...])
    o_ref[...] = out


def _ffn(h, g, wgu, wd, final_g, final):
    seq = h.shape[0]
    tm = min(ROW_TILE, seq)
    row = pl.BlockSpec((tm, D_MODEL), lambda i: (i, 0))
    return pl.pallas_call(
        functools.partial(_ffn_body, final=final),
        out_shape=jax.ShapeDtypeStruct((seq, D_MODEL), jnp.float32),
        grid=(seq // tm,),
        in_specs=[row, _const_spec((1, D_MODEL)), _const_spec((D_MODEL, 2 * FFN_HIDDEN)),
                  _const_spec((FFN_HIDDEN, D_MODEL)), _const_spec((1, D_MODEL))],
        out_specs=row,
        compiler_params=_params(("parallel",)),
        name="ffn",
    )(h, g, wgu, wd, final_g)


def _rope_lanes(x, c, s_lo, s_hi, half):
    return x * c + pltpu.roll(x, LANES - half, 1) * s_lo + pltpu.roll(x, half, 1) * s_hi


def _rope_rows(x, c, s, half):
    x1, x2 = x[:half], x[half:2 * half]
    return jnp.concatenate([x1 * c - x2 * s, x1 * s + x2 * c, x[2 * half:]], axis=0)


def _even_in_body(h_ref, g_ref, wn_ref, wt_ref, ca_ref, sal_ref, sah_ref, ci_ref, sil_ref, sih_ref,
                  cat_ref, sat_ref, cit_ref, sit_ref,
                  z_ref, xbc_ref, small_ref, k_ref, ki_ref, qt_ref, vt_ref, qit_ref, lohi_ref, dtt_ref):
    xn = _rms(h_ref[...], g_ref[...])
    xnb = xn.astype(MXU_DTYPE)
    xnt = xn.T.astype(MXU_DTYPE)

    z_ref[...] = _dot(xnb, wn_ref[:, _N_Z:_N_XBC])
    xbc_ref[...] = _dot(xnb, wn_ref[:, _N_XBC:_N_K])
    kk = _dot(xnb, wn_ref[:, _N_K:_N_SMALL])
    ca, sal, sah = ca_ref[...], sal_ref[...], sah_ref[...]
    for hd in range(ATT_HEADS):
        sl = slice(hd * ATT_HEAD_DIM, (hd + 1) * ATT_HEAD_DIM)
        k_ref[hd] = _rope_lanes(kk[:, sl], ca, sal, sah, ATT_ROT_HALF).astype(k_ref.dtype)
    sm = _dot(xnb, wn_ref[:, _N_SMALL:_N_END])
    small_ref[...] = sm
    smr = _rope_lanes(sm, ci_ref[...], sil_ref[...], sih_ref[...], IDX_ROT_HALF)
    ki_ref[...] = smr[:, _SM_KI:_SM_KI + IDX_HEAD_DIM].astype(ki_ref.dtype)

    att_scale = ATT_HEAD_DIM ** -0.5 * math.log2(math.e)
    qt = _dot(wt_ref[_T_Q:_T_V, :], xnt)
    cat, sat = cat_ref[...], sat_ref[...]
    def put(ref, where, val):
        tb = ref.shape[-1]
        for b in range(ref.shape[0]):
            ref[(b, *where, slice(None))] = val[:, b * tb:(b + 1) * tb].astype(ref.dtype)

    vt = _dot(wt_ref[_T_V:_T_QI, :], xnt)
    for hd in range(ATT_HEADS):
        sl = slice(hd * ATT_HEAD_DIM, (hd + 1) * ATT_HEAD_DIM)
        put(qt_ref, (hd, slice(None)), _rope_rows(qt[sl], cat, sat, ATT_ROT_HALF) * att_scale)
        put(vt_ref, (hd, slice(None)), vt[sl])
    tail = _dot(wt_ref[_T_WI:_T_END, :], xnt)
    wit = tail[0:IDX_HEADS] * (IDX_HEADS ** -0.5 * IDX_HEAD_DIM ** -0.5)
    dtt_ref[...] = tail[_T_DT - _T_WI:_T_END - _T_WI]
    pos = wit > 0.0
    put(lohi_ref, (slice(0, 8),), jnp.where(pos, 0.0, -jnp.inf))
    put(lohi_ref, (slice(8, 16),), jnp.where(pos, jnp.inf, 0.0))
    qit = _dot(wt_ref[_T_QI:_T_WI, :], xnt)
    cit, sit = cit_ref[...], sit_ref[...]
    for hd in range(IDX_HEADS):
        sl = slice(hd * IDX_HEAD_DIM, (hd + 1) * IDX_HEAD_DIM)
        put(qit_ref, (sl,), _rope_rows(qit[sl], cit, sit, IDX_ROT_HALF) * wit[hd:hd + 1])


def _even_in(h, g, wn, wt, rope):
    seq = h.shape[0]
    tm = min(ROW_TILE, seq)
    f32 = jnp.float32

    def rows(width):
        return pl.BlockSpec((tm, width), lambda i: (i, 0))

    def cols(height):
        return pl.BlockSpec((height, tm), lambda i: (0, i))

    def blocks(height, tb):
        return pl.BlockSpec((tm // tb, height, tb), lambda i: (i, 0, 0))

    def head_blocks(tb):
        return pl.BlockSpec((tm // tb, ATT_HEADS, ATT_HEAD_DIM, tb), lambda i: (i, 0, 0, 0))

    assert tm % DSA_K_TILE == 0 and tm % DSA_Q_TILE == 0
    nq, nk = seq // DSA_Q_TILE, seq // DSA_K_TILE
    out_shape = (
        jax.ShapeDtypeStruct((seq, SSD_INNER), f32),
        jax.ShapeDtypeStruct((seq, SSD_CONV_DIM), f32),
        jax.ShapeDtypeStruct((seq, LANES), f32),
        jax.ShapeDtypeStruct((ATT_HEADS, seq, ATT_HEAD_DIM), MXU_DTYPE),
        jax.ShapeDtypeStruct((seq, IDX_HEAD_DIM), MXU_DTYPE),
        jax.ShapeDtypeStruct((nq, ATT_HEADS, ATT_HEAD_DIM, DSA_Q_TILE), MXU_DTYPE),
        jax.ShapeDtypeStruct((nk, ATT_HEADS, ATT_HEAD_DIM, DSA_K_TILE), MXU_DTYPE),
        jax.ShapeDtypeStruct((nq, IDX_INNER, DSA_Q_TILE), MXU_DTYPE),
        jax.ShapeDtypeStruct((nq, 16, DSA_Q_TILE), f32),
        jax.ShapeDtypeStruct((SSD_HEADS, seq), f32),
    )
    out_specs = (rows(SSD_INNER), rows(SSD_CONV_DIM), rows(LANES),
                 pl.BlockSpec((ATT_HEADS, tm, ATT_HEAD_DIM), lambda i: (0, i, 0)), rows(IDX_HEAD_DIM),
                 head_blocks(DSA_Q_TILE), head_blocks(DSA_K_TILE), blocks(IDX_INNER, DSA_Q_TILE),
                 blocks(16, DSA_Q_TILE), cols(SSD_HEADS))
    in_specs = [rows(D_MODEL), _const_spec((1, D_MODEL)), _const_spec(wn.shape), _const_spec(wt.shape)]
    in_specs += [rows(LANES)] * 6
    in_specs += [cols(ATT_ROT_HALF)] * 2 + [cols(IDX_ROT_HALF)] * 2
    return pl.pallas_call(
        _even_in_body,
        out_shape=out_shape,
        grid=(seq // tm,),
        in_specs=in_specs,
        out_specs=out_specs,
        compiler_params=_params(("parallel",)),
        name="even_in",
    )(h, g, wn, wt, *rope)


def _ssd_body(xbc_ref, small_ref, dtt_ref, z_ref, cw_ref, cb_ref, dtb_ref, dtbt_ref, a_ref, at_ref,
              drep_ref, gn_ref, o_ref, tail_scr, st_scr, y_scr, xdec_scr, decay_scr):
    q = SSD_CHUNK
    f32 = jnp.float32

    @pl.when(pl.program_id(0) == 0)
    def _():
        tail_scr[...] = jnp.zeros_like(tail_scr)
        st_scr[...] = jnp.zeros_like(st_scr)

    x = xbc_ref[...]
    tail = tail_scr[...]
    row8 = lax.broadcasted_iota(jnp.int32, (8, SSD_CONV_DIM), 0)
    conv = x * cw_ref[SSD_CONV - 1:SSD_CONV, :] + cb_ref[...]
    for shift in range(1, SSD_CONV):
        rolled = pltpu.roll(x, shift, 0)
        head = jnp.where(row8 < shift, pltpu.roll(tail, shift, 0), rolled[0:8])
        shifted = jnp.concatenate([head, rolled[8:]], axis=0)
        conv = conv + shifted * cw_ref[SSD_CONV - 1 - shift:SSD_CONV - shift, :]
    tail_scr[...] = x[q - 8:q]
    xbc = conv * jax.nn.sigmoid(conv)
    xs = xbc[:, :SSD_INNER]
    bm = xbc[:, SSD_INNER:SSD_INNER + SSD_GROUPS * SSD_STATE]
    cm = xbc[:, SSD_INNER + SSD_GROUPS * SSD_STATE:]

    dt_col = _softplus(small_ref[:, _SM_DT:_SM_DT + SSD_HEADS] + dtb_ref[...])
    dt_row = _softplus(dtt_ref[...] + dtbt_ref[...])
    adt_col = dt_col * (-jnp.exp(a_ref[...]))
    adt_row = dt_row * (-jnp.exp(at_ref[...]))
    ri = lax.broadcasted_iota(jnp.int32, (q, q), 0)
    ci = lax.broadcasted_iota(jnp.int32, (q, q), 1)
    causal = ri >= ci
    tril = jnp.where(causal, 1.0, 0.0).astype(MXU_DTYPE)
    triu = jnp.where(ri <= ci, 1.0, 0.0).astype(MXU_DTYPE)
    cs_col = sum(_dot(tril, p) for p in _split3(adt_col))
    cs_row = sum(_dot(p, triu) for p in _split3(adt_row))

    cmb = cm.astype(MXU_DTYPE)
    bmb = bm.astype(MXU_DTYPE)
    for g in range(SSD_GROUPS):
        gs = slice(g * SSD_STATE, (g + 1) * SSD_STATE)
        cb = lax.dot_general(cmb[:, gs], bmb[:, gs], (((1,), (1,)), ((), ())),
                             preferred_element_type=f32)
        kpg = SSD_HEADS // SSD_GROUPS
        for hd in range(g * kpg, (g + 1) * kpg):
            hs = slice(hd * SSD_HEAD_DIM, (hd + 1) * SSD_HEAD_DIM)
            col = cs_col[:, hd:hd + 1]
            row = cs_row[hd:hd + 1, :]
            last = cs_row[hd:hd + 1, q - 1:q]
            decay = jnp.exp(jnp.where(causal, col - row, -jnp.inf))
            xdt = xs[:, hs] * dt_col[:, hd:hd + 1]
            y = _dot((cb * decay).astype(MXU_DTYPE), xdt.astype(MXU_DTYPE))
            y_off = _dot(cmb[:, gs], st_scr[:, hs].astype(MXU_DTYPE))
            y_scr[:, hs] = y + y_off * jnp.exp(col)
            xdec_scr[:, hs] = (xdt * jnp.exp(last - col)).astype(xdec_scr.dtype)
            decay_scr[:, hs] = jnp.broadcast_to(jnp.exp(last), (1, SSD_HEAD_DIM))
        ws = slice(g * kpg * SSD_HEAD_DIM, (g + 1) * kpg * SSD_HEAD_DIM)
        bmt = bm[:, gs].T.astype(MXU_DTYPE)
        st_scr[:, ws] = st_scr[:, ws] * decay_scr[:, ws] + _dot(bmt, xdec_scr[:, ws])

    z = z_ref[...]
    y = (y_scr[...] + drep_ref[...] * xs) * (z * jax.nn.sigmoid(z))
    gw = SSD_INNER // SSD_GROUPS
    for g in range(SSD_GROUPS):
        sl = slice(g * gw, (g + 1) * gw)
        seg = y[:, sl]
        seg = seg * lax.rsqrt(jnp.mean(seg * seg, axis=-1, keepdims=True) + EPS)
        o_ref[:, sl] = (seg * gn_ref[:, sl]).astype(o_ref.dtype)


def _ssd(xbc, small, dtt, z, conv_w, conv_b, dt_bias, a_log, d_skip, gate_norm):
    seq = xbc.shape[0]
    q = SSD_CHUNK
    f32 = jnp.float32

    def rows(width):
        return pl.BlockSpec((q, width), lambda i: (i, 0))

    drep = jnp.repeat(d_skip, SSD_HEAD_DIM)[None, :]
    return pl.pallas_call(
        _ssd_body,
        out_shape=jax.ShapeDtypeStruct((seq, SSD_INNER), MXU_DTYPE),
        grid=(seq // q,),
        in_specs=[rows(SSD_CONV_DIM), rows(LANES), pl.BlockSpec((SSD_HEADS, q), lambda i: (0, i)),
                  rows(SSD_INNER), _const_spec((SSD_CONV, SSD_CONV_DIM)), _const_spec((1, SSD_CONV_DIM)),
                  _const_spec((1, SSD_HEADS)), _const_spec((SSD_HEADS, 1)),
                  _const_spec((1, SSD_HEADS)), _const_spec((SSD_HEADS, 1)),
                  _const_spec((1, SSD_INNER)), _const_spec((1, SSD_INNER))],
        out_specs=rows(SSD_INNER),
        scratch_shapes=[pltpu.VMEM((8, SSD_CONV_DIM), f32),
                        pltpu.VMEM((SSD_STATE, SSD_INNER), f32),
                        pltpu.VMEM((q, SSD_INNER), f32),
                        pltpu.VMEM((q, SSD_INNER), MXU_DTYPE),
                        pltpu.VMEM((1, SSD_INNER), f32)],
        compiler_params=_params(("arbitrary",)),
        name="ssd",
    )(xbc, small, dtt, z, conv_w, conv_b[None, :], dt_bias[None, :], dt_bias[:, None],
      a_log[None, :], a_log[:, None], drep, gate_norm[None, :])


def _dsa_body(qi_ref, kj_ref, qit_ref, lohi_ref, ki_ref, qt_ref, k_ref, vt_ref, o_ref,
              keys_scr, lead_scr, y_scr, thr_scr, m_scr, l_scr, acc_scr, logit_scr, top_scr, *, top_k, idx_bits):
    tq, tk, ck = DSA_Q_TILE, DSA_K_TILE, DSA_CHUNK
    i32 = jnp.int32
    p = pl.program_id(0)
    qi = qi_ref[p]
    kj = kj_ref[p]

    @pl.when(kj == 0)
    def _select():
        chunks_per_tile = tk // ck
        n_tiles = ((qi + 1) * tq + tk - 1) // tk
        n_chunks = n_tiles * chunks_per_tile
        lo = lohi_ref[0:8, :]
        hi = lohi_ref[8:16, :]
        rows = 4 * 8
        t_pos = qi * tq + lax.broadcasted_iota(i32, (rows, tq), 1)
        s_off = lax.broadcasted_iota(i32, (ck, tq), 0)
        s_rows = lax.broadcasted_iota(i32, (rows, tq), 0)

        def project(c, slot):
            kic = ki_ref[pl.ds(pl.multiple_of(c * ck, ck), ck), :]
            for hd in range(IDX_HEADS):
                y_scr[slot, hd] = _dot(kic, qit_ref[hd * IDX_HEAD_DIM:(hd + 1) * IDX_HEAD_DIM, :])

        def emit_keys(c, slot, diagonal):
            for g in range(ck // rows):
                r0 = pl.multiple_of(c * ck + g * rows, rows)
                gs = slice(g * rows, (g + 1) * rows)
                sc = None
                for hd in range(IDX_HEADS):
                    term = jnp.minimum(jnp.maximum(y_scr[slot, hd, gs, :], lo[hd:hd + 1]), hi[hd:hd + 1])
                    sc = term if sc is None else sc + term
                bits = pltpu.bitcast(sc, i32)
                bits = jnp.where((bits & 0x7FFFFFFF) < MIN_NORMAL_EXP, 0, bits)
                key = bits ^ ((bits >> 31) & 0x7FFFFFFF)
                lead = bits & LEAD_MASK
                if diagonal:
                    causal = r0 + s_rows <= t_pos
                    key = jnp.where(causal, key, INT_MIN)
                    lead = jnp.where(causal, lead, QUIET_NAN_BITS)
                keys_scr[pl.ds(r0, rows), :] = key
                lead_scr[pl.ds(r0, rows), :] = pltpu.bitcast(lead, jnp.float32).astype(jnp.bfloat16)

        assert chunks_per_tile % 2 == 0

        def score_tile(t, diagonal):
            for j in range(chunks_per_tile):
                c = t * chunks_per_tile + j
                emit_keys(c, j % 2, diagonal)
                if not (diagonal and j == chunks_per_tile - 1):
                    project(c + 1, (j + 1) % 2)

        project(0, 0)
        lax.fori_loop(0, n_tiles - 1, lambda t, carry: (score_tile(t, False), carry)[1], 0)
        score_tile(n_tiles - 1, True)

        partial_rows = 4 * 8
        s_tile = lax.broadcasted_iota(i32, (partial_rows, tq), 0)

        def key_groups(c):
            for g in range(tk // partial_rows):
                r0 = pl.multiple_of(c * tk + g * partial_rows, partial_rows)
                yield keys_scr[pl.ds(r0, partial_rows), :], r0

        def count(pred):
            def body(c, acc):
                for kk, r0 in key_groups(c):
                    acc = acc + jnp.where(pred(kk, r0), 1.0, 0.0)
                return acc
            acc = lax.fori_loop(0, n_tiles, body, jnp.zeros((partial_rows, tq), jnp.float32))
            return acc.sum(axis=0, keepdims=True)

        def count_lead(cand_code):
            cand_code = jnp.where((cand_code > 0) & (cand_code < MIN_NORMAL_CODE), MIN_NORMAL_CODE,
                                  jnp.where((cand_code < 0) & (cand_code >= -MIN_NORMAL_CODE), 0, cand_code))
            cand_bits = (cand_code ^ ((cand_code >> 31) & 0x7FFF)) << 16
            cand = jnp.broadcast_to(pltpu.bitcast(cand_bits, jnp.float32), (16, tq)).astype(jnp.bfloat16)
            one, zero = jnp.ones((), jnp.bfloat16), jnp.zeros((), jnp.bfloat16)

            def body(c, acc):
                parts = []
                for g in range(tk // 128):
                    r0 = pl.multiple_of(c * tk + g * 128, 128)
                    lead = lead_scr[pl.ds(r0, 128), :].reshape(8, 16, tq)
                    parts += [jnp.where(lead[j] >= cand, one, zero) for j in range(8)]
                while len(parts) > 1:
                    parts = [a + b for a, b in zip(parts[0::2], parts[1::2])]
                return acc + parts[0].astype(jnp.float32)
            acc = lax.fori_loop(0, n_tiles, body, jnp.zeros((16, tq), jnp.float32))
            return acc.sum(axis=0, keepdims=True)

        k_f = jnp.float32(top_k)
        n_t = (qi * tq + 1 + lax.broadcasted_iota(i32, (1, tq), 1)).astype(jnp.float32)

        def lead_step(b, carry):
            code, c_lo, c_hi = carry
            cand = code + lax.shift_left(i32(1), i32(15) - b)
            cnt = count_lead(cand)
            ok = cnt >= k_f
            return jnp.where(ok, cand, code), jnp.where(ok, cnt, c_lo), jnp.where(ok, c_hi, cnt)

        code, c_lo, c_hi = lax.fori_loop(
            0, 16, lead_step, (jnp.full((1, tq), -(2 ** 15), i32), n_t, jnp.zeros((1, tq), jnp.float32)))

        lo_k = jnp.maximum(code << 16, INT_MIN + 1)
        hi_k = jnp.where(code == 2 ** 15 - 1, jnp.iinfo(i32).max, (code + 1) << 16)

        def finished(lo_k, hi_k, c_lo):
            return (c_lo <= k_f) | (lo_k + 1 >= hi_k)

        def key_value(kk):
            kk = jnp.clip(kk, -FINITE_KEY, FINITE_KEY)
            return pltpu.bitcast(kk ^ ((kk >> 31) & 0x7FFFFFFF), jnp.float32)

        def tighten(lo_k, hi_k, c_lo, c_hi, done, it):
            def body(c, carry):
                mn, mx = carry
                for kk, _ in key_groups(c):
                    mn = jnp.minimum(mn, jnp.where(kk >= lo_k, kk, jnp.iinfo(i32).max))
                    mx = jnp.maximum(mx, jnp.where(kk < hi_k, kk, INT_MIN))
                return mn, mx
            mn, mx = lax.fori_loop(0, n_tiles, body, (jnp.full((partial_rows, tq), jnp.iinfo(i32).max, i32),
                                                      jnp.full((partial_rows, tq), INT_MIN, i32)))
            mn, mx = mn.min(axis=0, keepdims=True), mx.max(axis=0, keepdims=True)
            return jnp.where(done, lo_k, mn), jnp.where(done, hi_k, mx + 1), c_lo, c_hi

        def probe(lo_k, hi_k, c_lo, c_hi, done, it):
            below, span = c_lo - k_f, c_lo - c_hi
            end = jnp.where(below + below >= span, hi_k - 1, lo_k + 1)
            lo_v, hi_v = key_value(lo_k), key_value(hi_k)
            guess_v = lo_v + (below + 0.5) / jnp.maximum(span, 1.0) * (hi_v - lo_v)
            guess_b = pltpu.bitcast(guess_v, i32)
            guess = guess_b ^ ((guess_b >> 31) & 0x7FFFFFFF)
            mid = (lo_k >> 1) + (hi_k >> 1) + (lo_k & hi_k & 1)
            phase = it % SEARCH_ROUND
            cand = jnp.where(phase == 1, end, jnp.where(phase == 3, mid, guess))
            cand = jnp.where(done, lo_k, jnp.maximum(lo_k + 1, jnp.minimum(hi_k - 1, cand)))
            cnt = count(lambda kk, r0: kk >= cand)
            up = jnp.logical_and(~done, cnt >= k_f)
            down = jnp.logical_and(~done, cnt < k_f)
            return (jnp.where(up, cand, lo_k), jnp.where(down, cand, hi_k),
                    jnp.where(up, cnt, c_lo), jnp.where(down, cnt, c_hi))

        def search_cond(carry):
            return jnp.logical_and(carry[0] < SEARCH_ROUND * 32, carry[1] > 0)

        def search_step(carry):
            it, _, lo_k, hi_k, c_lo, c_hi = carry
            done = finished(lo_k, hi_k, c_lo)
            lo_k, hi_k, c_lo, c_hi = lax.cond(it % SEARCH_ROUND == 0, tighten, probe, lo_k, hi_k, c_lo, c_hi, done, it)
            still = jnp.where(finished(lo_k, hi_k, c_lo), 0.0, 1.0)
            return it + 1, jnp.max(still).astype(i32), lo_k, hi_k, c_lo, c_hi

        active = jnp.max(jnp.where(finished(lo_k, hi_k, c_lo), 0.0, 1.0)).astype(i32)
        _, _, thr, _, n_ge, n_gt = lax.while_loop(search_cond, search_step, (i32(0), active, lo_k, hi_k, c_lo, c_hi))
        thr_scr[0:1, :] = thr

        tied = n_ge > k_f
        need = jnp.where(tied, k_f - n_gt, 0.0)
        max_need = jnp.max(need)

        @pl.when(max_need > 0.0)
        def _ties():
            int_max = jnp.iinfo(i32).max

            def by_extraction(_):
                def next_tie(j, cut):
                    def body(c, mn):
                        for kk, r0 in key_groups(c):
                            idx = r0 + s_tile
                            mn = jnp.minimum(mn, jnp.where((kk == thr) & (idx > cut), idx, int_max))
                        return mn
                    mn = lax.fori_loop(0, n_tiles, body, jnp.full((partial_rows, tq), int_max, i32))
                    return jnp.where(j.astype(jnp.float32) < need, mn.min(axis=0, keepdims=True), cut)
                return lax.fori_loop(0, max_need.astype(i32), next_tie, jnp.full((1, tq), -1, i32))

            def by_index_bits(_):
                def idx_step(b, cut):
                    cand = cut + lax.shift_left(i32(1), i32(idx_bits - 1) - b)
                    cnt = count(lambda kk, r0: jnp.where(kk == thr, r0 + s_tile, cand) < cand)
                    return jnp.where(cnt < need, cand, cut)
                return lax.fori_loop(0, idx_bits, idx_step, jnp.zeros((1, tq), i32))

            cut = lax.cond(max_need <= TIE_EXTRACT_MAX, by_extraction, by_index_bits, 0)
            cut = jnp.where(tied, cut, int_max)

            def drop(c, carry):
                r0 = pl.multiple_of(c * ck, ck)
                kk = keys_scr[pl.ds(r0, ck), :]
                excess = jnp.where(kk == thr, r0 + s_off, INT_MIN) > cut
                keys_scr[pl.ds(r0, ck), :] = jnp.where(excess, INT_MIN, kk)
                return carry

            lax.fori_loop(0, n_chunks, drop, 0)

        m_scr[...] = jnp.full_like(m_scr, NEG_BIG)
        l_scr[...] = jnp.zeros_like(l_scr)
        acc_scr[...] = jnp.zeros_like(acc_scr)
        top_scr[...] = jnp.full_like(top_scr, -jnp.inf)
        logit_scr[1] = jnp.full(logit_scr.shape[1:], -jnp.inf, jnp.float32)

    n_att = ((qi + 1) * tq - 1) // tk + 1
    row0 = pl.multiple_of(jnp.minimum(kj, n_att - 1) * tk, tk)
    thr_now = jnp.where(kj < n_att, thr_scr[0:1, :], jnp.iinfo(i32).max)
    bias = jnp.where(keys_scr[pl.ds(row0, tk), :] >= thr_now, 0.0, -jnp.inf)

    def attend(cur, prev):
        m_old = m_scr[...]
        m_new = jnp.maximum(m_old, top_scr[...])
        alpha = jnp.exp2(m_old - m_new)
        m_scr[...] = m_new
        col_sum, col_max = [], []
        for hd in range(ATT_HEADS):
            logit = _dot(k_ref[hd], qt_ref[hd]) + bias
            logit_scr[cur, hd] = logit
            col_max.append(jnp.max(logit, axis=0, keepdims=True))
            prob = jnp.exp2(logit_scr[prev, hd] - m_new[hd:hd + 1])
            col_sum.append(jnp.sum(prob, axis=0, keepdims=True))
            acc_scr[hd] = alpha[hd:hd + 1] * acc_scr[hd] + _dot(vt_ref[hd], prob.astype(MXU_DTYPE))
        l_scr[...] = alpha * l_scr[...] + jnp.concatenate(col_sum, axis=0)
        top_scr[...] = jnp.concatenate(col_max, axis=0)

    for parity in range(2):
        pl.when(kj % 2 == parity)(functools.partial(attend, parity, 1 - parity))

    @pl.when(kj == n_att)
    def _finish():
        for hd in range(ATT_HEADS):
            hs = slice(hd * ATT_HEAD_DIM, (hd + 1) * ATT_HEAD_DIM)
            o_ref[:, hs] = (acc_scr[hd] / l_scr[hd:hd + 1, :]).T.astype(o_ref.dtype)


def _dsa(qit, lohi, ki, qt, k, vt):
    seq = k.shape[1]
    tq, tk = DSA_Q_TILE, DSA_K_TILE
    assert seq % tk == 0 and tk % tq == 0 and tq % DSA_CHUNK == 0
    assert k.shape == (ATT_HEADS, seq, ATT_HEAD_DIM)
    assert qt.shape == (seq // tq, ATT_HEADS, ATT_HEAD_DIM, tq) and vt.shape == (seq // tk, ATT_HEADS, ATT_HEAD_DIM, tk)
    nq = seq // tq
    def n_att(i):
        return ((i + 1) * tq - 1) // tk + 1

    pairs = [(i, j) for i in range(nq) for j in range(n_att(i) + 1)]
    qi_idx = jnp.asarray([pq for pq, _ in pairs], jnp.int32)
    kj_idx = jnp.asarray([pk for _, pk in pairs], jnp.int32)
    top_k = min(TOPK_MAX, seq // 4)
    idx_bits = max(1, (seq - 1).bit_length())
    grid_spec = pltpu.PrefetchScalarGridSpec(
        num_scalar_prefetch=2,
        grid=(len(pairs),),
        in_specs=[
            pl.BlockSpec((None, IDX_INNER, tq), lambda p, qi, kj: (qi[p], 0, 0)),
            pl.BlockSpec((None, 16, tq), lambda p, qi, kj: (qi[p], 0, 0)),
            pl.BlockSpec((seq, IDX_HEAD_DIM), lambda p, qi, kj: (0, 0), pipeline_mode=pl.Buffered(1)),
            pl.BlockSpec((None, ATT_HEADS, ATT_HEAD_DIM, tq), lambda p, qi, kj: (qi[p], 0, 0, 0)),
            pl.BlockSpec((ATT_HEADS, tk, ATT_HEAD_DIM),
                         lambda p, qi, kj: (0, jnp.minimum(kj[p], n_att(qi[p]) - 1), 0)),
            pl.BlockSpec((None, ATT_HEADS, ATT_HEAD_DIM, tk),
                         lambda p, qi, kj: (jnp.maximum(kj[p] - 1, 0), 0, 0, 0)),
        ],
        out_specs=pl.BlockSpec((tq, ATT_INNER), lambda p, qi, kj: (qi[p], 0)),
        scratch_shapes=[pltpu.VMEM((seq, tq), jnp.int32),
                        pltpu.VMEM((seq, tq), jnp.bfloat16),
                        pltpu.VMEM((2, IDX_HEADS, DSA_CHUNK, tq), jnp.float32),
                        pltpu.VMEM((8, tq), jnp.int32),
                        pltpu.VMEM((ATT_HEADS, tq), jnp.float32),
                        pltpu.VMEM((ATT_HEADS, tq), jnp.float32),
                        pltpu.VMEM((ATT_HEADS, ATT_HEAD_DIM, tq), jnp.float32),
                        pltpu.VMEM((2, ATT_HEADS, tk, tq), jnp.float32),
                        pltpu.VMEM((ATT_HEADS, tq), jnp.float32)],
    )
    return pl.pallas_call(
        functools.partial(_dsa_body, top_k=top_k, idx_bits=idx_bits),
        out_shape=jax.ShapeDtypeStruct((seq, ATT_INNER), MXU_DTYPE),
        grid_spec=grid_spec,
        compiler_params=_params(("arbitrary",)),
        name="dsa",
    )(qi_idx, kj_idx, qit, lohi, ki, qt, k, vt)


def _even_out_body(h_ref, y_ref, o_ref, wy_ref, wo_ref, out_ref):
    out_ref[...] = h_ref[...] + _dot(y_ref[...], wy_ref[...]) + _dot(o_ref[...], wo_ref[...])


def _even_out(h, y, o, wy, wo):
    seq = h.shape[0]
    tm = min(ROW_TILE, seq)
    row = pl.BlockSpec((tm, D_MODEL), lambda i: (i, 0))
    return pl.pallas_call(
        _even_out_body,
        out_shape=jax.ShapeDtypeStruct((seq, D_MODEL), jnp.float32),
        grid=(seq // tm,),
        in_specs=[row, row, row, _const_spec((SSD_INNER, D_MODEL)), _const_spec((ATT_INNER, D_MODEL))],
        out_specs=row,
        compiler_params=_params(("parallel",)),
        name="even_out",
    )(h, y, o, wy, wo)


def _odd_body(h_ref, g_ref, win_ref, lng_ref, lnb_ref, ws_ref, bs_ref, wout_ref, o_ref, gated_scr):
    h = h_ref[...]
    tm = h.shape[0]
    xn = _rms(h, g_ref[...]).astype(MXU_DTYPE)
    u = jax.nn.gelu(_dot(xn, win_ref[:, :SG_INNER]))
    v = jax.nn.gelu(_dot(xn, win_ref[:, SG_INNER:]))
    mu = jnp.mean(v, axis=-1, keepdims=True)
    vc = v - mu
    var = jnp.mean(vc * vc, axis=-1, keepdims=True)
    v = (vc * lax.rsqrt(var + EPS) * lng_ref[...] + lnb_ref[...]).astype(MXU_DTYPE)
    ri = lax.broadcasted_iota(jnp.int32, (SG_CHUNK, SG_CHUNK), 0)
    ci = lax.broadcasted_iota(jnp.int32, (SG_CHUNK, SG_CHUNK), 1)
    gw = SG_INNER // SG_GROUPS
    for g in range(SG_GROUPS):
        w = jnp.where(ri >= ci, ws_ref[g], 0.0).astype(MXU_DTYPE)
        gs = slice(g * gw, (g + 1) * gw)
        for c in range(tm // SG_CHUNK):
            rs = slice(c * SG_CHUNK, (c + 1) * SG_CHUNK)
            mixed = _dot(w, v[rs, gs]) + bs_ref[:, gs]
            gated_scr[rs, gs] = (u[rs, gs] * mixed).astype(gated_scr.dtype)
    o_ref[...] = h + _dot(gated_scr[...], wout_ref[...])


def _odd(h, g, win, ln_g, ln_b, w_s, bs_full, wout):
    seq = h.shape[0]
    tm = min(ROW_TILE, seq)
    row = pl.BlockSpec((tm, D_MODEL), lambda i: (i, 0))
    return pl.pallas_call(
        _odd_body,
        out_shape=jax.ShapeDtypeStruct((seq, D_MODEL), jnp.float32),
        grid=(seq // tm,),
        in_specs=[row, _const_spec((1, D_MODEL)), _const_spec((D_MODEL, 2 * SG_INNER)),
                  _const_spec((1, SG_INNER)), _const_spec((1, SG_INNER)),
                  _const_spec((SG_GROUPS, SG_CHUNK, SG_CHUNK)), _const_spec((SG_CHUNK, SG_INNER)),
                  _const_spec((SG_INNER, D_MODEL))],
        out_specs=row,
        scratch_shapes=[pltpu.VMEM((tm, SG_INNER), MXU_DTYPE)],
        compiler_params=_params(("parallel",)),
        name="odd",
    )(h, g, win, ln_g, ln_b, w_s, bs_full, wout)


def _rope_inputs(seq):
    def tables(rot_dim):
        inv = 1.0 / (ROPE_THETA ** (jnp.arange(0, rot_dim, 2, dtype=jnp.float32) / rot_dim))
        ang = jnp.arange(seq, dtype=jnp.float32)[:, None] * inv[None, :]
        return jnp.cos(ang), jnp.sin(ang)

    def lane_tables(cos, sin):
        half = cos.shape[1]
        pad = LANES - 2 * half
        c = jnp.concatenate([cos, cos, jnp.ones((seq, pad), jnp.float32)], axis=1)
        s_lo = jnp.concatenate([-sin, jnp.zeros((seq, LANES - half), jnp.float32)], axis=1)
        s_hi = jnp.concatenate([jnp.zeros((seq, half), jnp.float32), sin,
                                jnp.zeros((seq, pad), jnp.float32)], axis=1)
        return c, s_lo, s_hi

    cos_a, sin_a = tables(ATT_HEAD_DIM // ROPE_FRACTION)
    cos_i, sin_i = tables(IDX_HEAD_DIM // ROPE_FRACTION)
    return (*lane_tables(cos_a, sin_a), *lane_tables(cos_i, sin_i), cos_a.T, sin_a.T, cos_i.T, sin_i.T)


def _even_weights(w_in):
    offs = np.cumsum((SSD_INNER, SSD_CONV_DIM, SSD_HEADS, ATT_INNER, ATT_INNER, ATT_INNER,
                      IDX_INNER, IDX_HEAD_DIM, IDX_HEADS))[:-1].tolist()
    z, xbc, dt, q, k, v, qi, ki, wi = jnp.split(w_in, offs, axis=-1)
    pad = jnp.zeros((D_MODEL, LANES - IDX_HEAD_DIM - SSD_HEADS), w_in.dtype)
    wn = jnp.concatenate([z, xbc, k, ki, dt, pad], axis=1).astype(MXU_DTYPE)
    wi_pad = jnp.zeros((D_MODEL, _T_DT - _T_WI - IDX_HEADS), w_in.dtype)
    wt = jnp.concatenate([q, v, qi, wi, wi_pad, dt], axis=1).T.astype(MXU_DTYPE)
    assert wn.shape == (D_MODEL, _N_END) and wt.shape == (_T_END, D_MODEL)
    return wn, wt


def _even_mixer(h, g, w_in, conv_w, conv_b, dt_bias, a_log, d_skip, gate_norm, w_out, rope):
    wn, wt = _even_weights(w_in)
    z, xbc, small, k, ki, qt, vt, qit, lohi, dtt = _even_in(h, g, wn, wt, rope)
    y = _ssd(xbc, small, dtt, z, conv_w, conv_b, dt_bias, a_log, d_skip, gate_norm)
    o = _dsa(qit, lohi, ki, qt, k, vt)
    w_out = w_out.astype(MXU_DTYPE)
    return _even_out(h, y, o, w_out[:SSD_INNER], w_out[SSD_INNER:])


def _forward(x, norm_g, final_g, ffn_w_gu, ffn_w_down, ev_w_in, ev_conv_w, ev_conv_b, ev_dt_bias,
             ev_a_log, ev_d, ev_gate_norm, ev_w_out, od_w_in, od_ln_g, od_ln_b, od_w_s, od_b_s, od_w_out):
    bsz, seq, _ = x.shape
    depth = norm_g.shape[0]
    rope = _rope_inputs(seq)
    wgu = ffn_w_gu.astype(MXU_DTYPE)
    wd = ffn_w_down.astype(MXU_DTYPE)
    fg = final_g[None, :]
    outs = []
    for b in range(bsz):
        h = x[b]
        for layer in range(depth):
            j = layer // 2
            h = _ffn(h, norm_g[layer, 0][None, :], wgu[layer, 0], wd[layer, 0], fg, False)
            g1 = norm_g[layer, 1][None, :]
            if layer % 2 == 0:
                h = _even_mixer(h, g1, ev_w_in[j], ev_conv_w[j], ev_conv_b[j], ev_dt_bias[j], ev_a_log[j],
                                ev_d[j], ev_gate_norm[j], ev_w_out[j], rope)
            else:
                bs_full = jnp.repeat(od_b_s[j].T, SG_INNER // SG_GROUPS, axis=1)
                h = _odd(h, g1, od_w_in[j].astype(MXU_DTYPE), od_ln_g[j][None, :], od_ln_b[j][None, :],
                         od_w_s[j], bs_full, od_w_out[j].astype(MXU_DTYPE))
            h = _ffn(h, norm_g[layer, 2][None, :], wgu[layer, 1], wd[layer, 1], fg,
                     layer == depth - 1)
        outs.append(h)
    return outs[0][None] if bsz == 1 else jnp.stack(outs, axis=0)


def kernel(x, norm_g, final_g, ffn_w_gu, ffn_w_down, ev_w_in, ev_conv_w, ev_conv_b, ev_dt_bias, ev_a_log,
           ev_d, ev_gate_norm, ev_w_out, od_w_in, od_ln_g, od_ln_b, od_w_s, od_b_s, od_w_out):
    return _forward(x, norm_g, final_g, ffn_w_gu, ffn_w_down, ev_w_in, ev_conv_w, ev_conv_b, ev_dt_bias,
                    ev_a_log, ev_d, ev_gate_norm, ev_w_out, od_w_in, od_ln_g, od_ln_b, od_w_s, od_b_s,
                    od_w_out)
```
